```python
import jax, jax.numpy as jnp
from jax import lax
import numpy as np

D_MODEL = 1024
BATCH = 4
SEQ = 4096
DEPTH = 2
DEC_BATCH = 8
DEC_SEQ = 64
PAST_LEN = 2048

CHUNK = 64
N_META = 16
PADF = CHUNK - N_META
A_HEADS = 8
A_KV_HEADS = 2
HEAD_DIM = 64
WINDOW = 128
WIN_CHUNKS = WINDOW // CHUNK
B_WIDTH = 512
B_CONV = 3
C_WIDTH = 512
C_CONV = 31
D_HEADS = 4
D_KEY = 128
D_VAL = 128
D_WIDTH = D_HEADS * D_VAL
FFN_HIDDEN = -(-8 * D_MODEL // (3 * 256)) * 256
ALPHA = (2 * DEPTH) ** 0.25
BETA = (8 * DEPTH) ** -0.25
LN_EPS = 1e-5
RMS_EPS = 1e-6
NEG = -1e30

AB_SPLITS = (A_HEADS * HEAD_DIM, A_KV_HEADS * HEAD_DIM, A_KV_HEADS * HEAD_DIM, B_WIDTH, B_WIDTH, B_WIDTH)
AB_IN = sum(AB_SPLITS)
AB_OUT = A_HEADS * HEAD_DIM + B_WIDTH
CD_SPLITS = (C_WIDTH, C_WIDTH, D_HEADS * D_KEY, D_HEADS * D_KEY, D_WIDTH, D_WIDTH)
CD_IN = sum(CD_SPLITS)
CD_OUT = C_WIDTH + D_WIDTH

kernel_name = 'hybrid_stream_encoder_step'


def split_cols(y, sizes):
    idx = [int(s) for s in np.cumsum(sizes)[:-1]]
    return jnp.split(y, idx, axis=-1)


def layer_norm(x, g, b):
    xf = x.astype(jnp.float32)
    mu = xf.mean(-1, keepdims=True)
    var = jnp.mean(jnp.square(xf - mu), -1, keepdims=True)
    return ((xf - mu) * lax.rsqrt(var + LN_EPS) * g + b).astype(x.dtype)


def ffn(h, w_gu, w_down):
    gate, up = jnp.split(h @ w_gu, 2, axis=-1)
    return (jax.nn.silu(gate) * up) @ w_down


def residual_block(h, mix, g1, b1, g2, b2, w_gu, w_down):
    h = layer_norm(ALPHA * h + mix, g1, b1)
    return layer_norm(ALPHA * h + ffn(h, w_gu, w_down), g2, b2)


def causal_dwconv(u, hist, w):
    width, ch = w.shape
    full = jnp.concatenate([hist.astype(u.dtype), u], axis=1)
    y = lax.conv_general_dilated(full, w.astype(u.dtype)[:, None, :], (1,), 'VALID',
                                 dimension_numbers=('NWC', 'WIO', 'NWC'), feature_group_count=ch)
    return y, full[:, full.shape[1] - (width - 1):]


def alibi_slopes(n):
    return jnp.asarray(np.exp2(-8.0 * np.arange(1, n + 1, dtype=np.float32) / n), jnp.float32)


def sink_attention(q, k, v, qpos, kpos, kvalid, sinks):
    bn, nblk, nq, nh, hd = q.shape
    nkv = k.shape[3]
    grp = nh // nkv
    qg = q.reshape(bn, nblk, nq, nkv, grp, hd)
    s = jnp.einsum('bnqkgd,bnskd->bnkgqs', qg, k, preferred_element_type=jnp.float32) * (hd ** -0.5)
    dist = jnp.abs(qpos[:, :, None] - kpos[:, None, :]).astype(jnp.float32)
    slopes = alibi_slopes(nh).reshape(nkv, grp)
    s = s - slopes[None, None, :, :, None, None] * dist[None, :, None, None]
    s = jnp.where(kvalid[None, :, None, None, None, :], s, NEG)
    sink = sinks.astype(jnp.float32).reshape(nkv, grp)[None, None, :, :, None, None]
    m = jnp.maximum(s.max(-1, keepdims=True), sink)
    p = jnp.exp(s - m)
    p = (p / (p.sum(-1, keepdims=True) + jnp.exp(sink - m))).astype(v.dtype)
    o = jnp.einsum('bnkgqs,bnskd->bnqkgd', p, v)
    return o.reshape(bn, nblk, nq, nh * hd)


def window_attention_prompt(q, k, v, sinks):
    bn, length = q.shape[:2]
    lp = length + PADF
    nb = lp // CHUNK
    front = lambda a, n: jnp.pad(a, ((0, 0), (n, 0), (0, 0), (0, 0)))
    qb = front(q, PADF).reshape(bn, nb, CHUNK, A_HEADS, HEAD_DIM)
    kp = front(k, PADF + WIN_CHUNKS * CHUNK).reshape(bn, nb + WIN_CHUNKS, CHUNK, A_KV_HEADS, HEAD_DIM)
    vp = front(v, PADF + WIN_CHUNKS * CHUNK).reshape(bn, nb + WIN_CHUNKS, CHUNK, A_KV_HEADS, HEAD_DIM)
    kband = jnp.concatenate([kp[:, j:j + nb] for j in range(WIN_CHUNKS + 1)], axis=2)
    vband = jnp.concatenate([vp[:, j:j + nb] for j in range(WIN_CHUNKS + 1)], axis=2)
    blk = jnp.arange(nb)[:, None]
    qpos = blk * CHUNK + jnp.arange(CHUNK)[None] - PADF
    kpos = (blk - WIN_CHUNKS) * CHUNK + jnp.arange((WIN_CHUNKS + 1) * CHUNK)[None] - PADF
    o = sink_attention(qb, kband, vband, qpos, kpos, kpos >= 0, sinks)
    return o.reshape(bn, lp, A_HEADS * HEAD_DIM)[:, PADF:]


def window_attention_sample(q, k_new, v_new, cache_k, cache_v, sinks):
    bn, t = q.shape[:2]
    w = cache_k.shape[1]
    kall = jnp.concatenate([cache_k.astype(k_new.dtype), k_new], axis=1)
    vall = jnp.concatenate([cache_v.astype(v_new.dtype), v_new], axis=1)
    qpos = (w + jnp.arange(t))[None]
    kpos = jnp.arange(w + t)[None]
    o = sink_attention(q[:, None], kall[:, None], vall[:, None], qpos, kpos, jnp.ones(kpos.shape, bool), sinks)
    return o.reshape(bn, t, A_HEADS * HEAD_DIM), kall[:, -w:], vall[:, -w:]


def ab_project(x, w_in, b_in):
    q, k, v, bg, cg, hb = split_cols(x @ w_in + b_in, AB_SPLITS)
    bn, t = x.shape[:2]
    q = q.reshape(bn, t, A_HEADS, HEAD_DIM)
    k = k.reshape(bn, t, A_KV_HEADS, HEAD_DIM)
    v = v.reshape(bn, t, A_KV_HEADS, HEAD_DIM)
    return q, k, v, bg, cg * hb


def mixer_ab_prompt(x, w_in, b_in, sinks, conv_w, w_o, win):
    q, k, v, bg, u = ab_project(x, w_in, b_in)
    attn = window_attention_prompt(q, k, v, sinks)
    cb, conv_state = causal_dwconv(u, jnp.zeros((x.shape[0], B_CONV - 1, B_WIDTH), x.dtype), conv_w)
    out = jnp.concatenate([attn, bg * cb], axis=-1) @ w_o
    return out, k[:, -win:], v[:, -win:], conv_state


def mixer_ab_sample(x, cache_k, cache_v, conv_hist, w_in, b_in, sinks, conv_w, w_o):
    q, k, v, bg, u = ab_project(x, w_in, b_in)
    attn, new_k, new_v = window_attention_sample(q, k, v, cache_k, cache_v, sinks)
    cb, conv_state = causal_dwconv(u, conv_hist, conv_w)
    out = jnp.concatenate([attn, bg * cb], axis=-1) @ w_o
    return out, new_k, new_v, conv_state


def hgrn_lower_bound(lower_bounds, layer):
    p = jax.nn.softmax(lower_bounds.astype(jnp.float32), axis=0)
    return (jnp.cumsum(p, axis=0) - p[0])[layer]


def cd_project(x, w_in, b_in, lb):
    a, gc, q, f, i, g = split_cols(x @ w_in + b_in, CD_SPLITS)
    bn, t = x.shape[:2]
    lbk = lb.reshape(D_HEADS, D_KEY)
    forget = lbk + (1.0 - lbk) * jax.nn.sigmoid(f.reshape(bn, t, D_HEADS, D_KEY).astype(jnp.float32))
    q = q.reshape(bn, t, D_HEADS, D_KEY).astype(jnp.float32)
    v = i.reshape(bn, t, D_HEADS, D_VAL).astype(jnp.float32)
    return a * jax.nn.sigmoid(gc), q, 1.0 - forget, v, jnp.log(forget), g


def conformer_conv(u, hist, conv_w, conv_b, ln_g, ln_b):
    c, new_hist = causal_dwconv(u, hist, conv_w)
    return jax.nn.silu(layer_norm(c + conv_b, ln_g, ln_b)), new_hist


def hgrn_block(S, qb, kb, vb, lfb):
    t = qb.shape[1]
    cum = jnp.cumsum(lfb, axis=1)
    o_inter = jnp.einsum('bthk,bhkv->bthv', qb * jnp.exp(cum), S)
    tri = jnp.tril(jnp.ones((t, t), bool))
    diff = cum[:, :, None] - cum[:, None, :]
    decay = jnp.exp(jnp.where(tri[None, :, :, None, None], diff, -jnp.inf))
    att = jnp.einsum('bthk,bshk,btshk->bhts', qb, kb, decay)
    o_intra = jnp.einsum('bhts,bshv->bthv', att, vb)
    tot = cum[:, -1]
    k_dec = kb * jnp.exp(tot[:, None] - cum)
    S_new = jnp.exp(tot)[..., None] * S + jnp.einsum('bshk,bshv->bhkv', k_dec, vb)
    return S_new, o_inter + o_intra


def hgrn_prompt(q, k, v, lf):
    bn = q.shape[0]
    pad = ((0, 0), (PADF, 0), (0, 0), (0, 0))
    q, k, v, lf = [jnp.pad(a, pad) for a in (q, k, v, lf)]
    nb = q.shape[1] // CHUNK
    blocks = lambda a: a.reshape(bn, nb, CHUNK, *a.shape[2:]).swapaxes(0, 1)
    S0 = jnp.zeros((bn, D_HEADS, D_KEY, D_VAL), jnp.float32)
    S, o = lax.scan(lambda s, xs: hgrn_block(s, *xs), S0, (blocks(q), blocks(k), blocks(v), blocks(lf)))
    o = o.swapaxes(0, 1).reshape(bn, nb * CHUNK, D_HEADS, D_VAL)[:, PADF:]
    return S, o


def hgrn_readout(o, g, norm_g):
    bn, t = o.shape[:2]
    o = o * lax.rsqrt(jnp.mean(o * o, -1, keepdims=True) + RMS_EPS)
    return (o.reshape(bn, t, D_WIDTH) * norm_g * jax.nn.silu(g.astype(jnp.float32))).astype(g.dtype)


def mixer_cd_prompt(x, w_in, b_in, conv_w, conv_b, ln_g, ln_b, lb, norm_g, w_o):
    u, q, k, v, lf, g = cd_project(x, w_in, b_in, lb)
    yc, conv_state = conformer_conv(u, jnp.zeros((x.shape[0], C_CONV - 1, C_WIDTH), x.dtype), conv_w, conv_b, ln_g, ln_b)
    S, o = hgrn_prompt(q, k, v, lf)
    out = jnp.concatenate([yc, hgrn_readout(o, g, norm_g)], axis=-1) @ w_o
    return out, conv_state, S.astype(x.dtype)


def mixer_cd_sample(x, conv_hist, S, w_in, b_in, conv_w, conv_b, ln_g, ln_b, lb, norm_g, w_o):
    u, q, k, v, lf, g = cd_project(x, w_in, b_in, lb)
    yc, conv_state = conformer_conv(u, conv_hist, conv_w, conv_b, ln_g, ln_b)
    S_new, o = hgrn_block(S.astype(jnp.float32), q, k, v, lf)
    out = jnp.concatenate([yc, hgrn_readout(o, g, norm_g)], axis=-1) @ w_o
    return out, conv_state, S_new.astype(x.dtype)


def setup_inputs(seed: int = 0) -> dict:
    key = jax.random.key(seed)
    ks = jax.random.split(key, 28)
    nrm = lambda k, shape, scale: scale * jax.random.normal(k, shape, jnp.float32)
    win = min(WINDOW, PAST_LEN)
    return {
        'x_prompt': nrm(ks[0], (BATCH, SEQ, D_MODEL), 1.0),
        'x_sample': nrm(ks[1], (DEC_BATCH, DEC_SEQ, D_MODEL), 1.0),
        'cache_k_a': nrm(ks[2], (DEC_BATCH, win, A_KV_HEADS, HEAD_DIM), 1.0),
        'cache_v_a': nrm(ks[3], (DEC_BATCH, win, A_KV_HEADS, HEAD_DIM), 1.0),
        'state_conv_b': nrm(ks[4], (DEC_BATCH, B_CONV - 1, B_WIDTH), 1.0),
        'state_conv_c': nrm(ks[5], (DEC_BATCH, C_CONV - 1, C_WIDTH), 0.5),
        'state_hgrn': nrm(ks[6], (DEC_BATCH, D_HEADS, D_KEY, D_VAL), 0.5),
        'meta_tokens': nrm(ks[7], (N_META, D_MODEL), 1.0),
        'ab_w_in': nrm(ks[8], (D_MODEL, AB_IN), D_MODEL ** -0.5),
        'ab_b_in': nrm(ks[9], (AB_IN,), 0.01),
        'a_sinks': nrm(ks[10], (A_HEADS,), 0.5),
        'b_conv_w': nrm(ks[11], (B_CONV, B_WIDTH), B_CONV ** -0.5),
        'ab_w_o': nrm(ks[12], (AB_OUT, D_MODEL), BETA * AB_OUT ** -0.5),
        'cd_w_in': nrm(ks[13], (D_MODEL, CD_IN), D_MODEL ** -0.5),
        'cd_b_in': nrm(ks[14], (CD_IN,), 0.01),
        'c_conv_w': nrm(ks[15], (C_CONV, C_WIDTH), C_CONV ** -0.5),
        'c_conv_b': nrm(ks[16], (C_WIDTH,), 0.01),
        'c_ln_g': 1.0 + nrm(ks[17], (C_WIDTH,), 0.01),
        'c_ln_b': nrm(ks[18], (C_WIDTH,), 0.01),
        'd_lower_bounds': nrm(ks[19], (DEPTH, D_HEADS * D_KEY), 0.1),
        'd_norm_g': 1.0 + nrm(ks[20], (D_WIDTH,), 0.01),
        'cd_w_o': nrm(ks[21], (CD_OUT, D_MODEL), BETA * CD_OUT ** -0.5),
        'ln1_g': 1.0 + nrm(ks[22], (DEPTH, D_MODEL), 0.01),
        'ln1_b': nrm(ks[23], (DEPTH, D_MODEL), 0.01),
        'ln2_g': 1.0 + nrm(ks[24], (DEPTH, D_MODEL), 0.01),
        'ln2_b': nrm(ks[25], (DEPTH, D_MODEL), 0.01),
        'ffn_w_gu': nrm(ks[26], (DEPTH, D_MODEL, 2 * FFN_HIDDEN), D_MODEL ** -0.5),
        'ffn_w_down': nrm(ks[27], (DEPTH, FFN_HIDDEN, D_MODEL), BETA * FFN_HIDDEN ** -0.5),
    }


def reference(x_prompt, x_sample, cache_k_a, cache_v_a, state_conv_b, state_conv_c, state_hgrn,
              meta_tokens, ab_w_in, ab_b_in, a_sinks, b_conv_w, ab_w_o, cd_w_in, cd_b_in,
              c_conv_w, c_conv_b, c_ln_g, c_ln_b, d_lower_bounds, d_norm_g, cd_w_o,
              ln1_g, ln1_b, ln2_g, ln2_b, ffn_w_gu, ffn_w_down):
    win = cache_k_a.shape[1]
    bp = x_prompt.shape[0]
    meta = jnp.broadcast_to(meta_tokens.astype(x_prompt.dtype)[None], (bp, N_META, D_MODEL))
    hp = jnp.concatenate([meta, x_prompt], axis=1)
    hs = x_sample
    for l in range(DEPTH):
        if l % 2 == 0:
            mp, k_a_p, v_a_p, conv_b_p = mixer_ab_prompt(hp, ab_w_in, ab_b_in, a_sinks, b_conv_w, ab_w_o, win)
            ms, k_a_s, v_a_s, conv_b_s = mixer_ab_sample(hs, cache_k_a, cache_v_a, state_conv_b,
                                                         ab_w_in, ab_b_in, a_sinks, b_conv_w, ab_w_o)
        else:
            lb = hgrn_lower_bound(d_lower_bounds, l)
            mp, conv_c_p, hgrn_p = mixer_cd_prompt(hp, cd_w_in, cd_b_in, c_conv_w, c_conv_b, c_ln_g, c_ln_b,
                                                   lb, d_norm_g, cd_w_o)
            ms, conv_c_s, hgrn_s = mixer_cd_sample(hs, state_conv_c, state_hgrn, cd_w_in, cd_b_in, c_conv_w,
                                                   c_conv_b, c_ln_g, c_ln_b, lb, d_norm_g, cd_w_o)
        hp = residual_block(hp, mp, ln1_g[l], ln1_b[l], ln2_g[l], ln2_b[l], ffn_w_gu[l], ffn_w_down[l])
        hs = residual_block(hs, ms, ln1_g[l], ln1_b[l], ln2_g[l], ln2_b[l], ffn_w_gu[l], ffn_w_down[l])
    y_prompt = hp[:, N_META:]
    return (y_prompt, hs, k_a_p, v_a_p, conv_b_p, conv_c_p, hgrn_p, k_a_s, v_a_s, conv_b_s, conv_c_s, hgrn_s)
```

```python
import functools

import jax
import jax.numpy as jnp
import numpy as np
from jax import lax
from jax.experimental import pallas as pl
from jax.experimental.pallas import tpu as pltpu

F32 = jnp.float32
BF16 = jnp.bfloat16

D_MODEL = 1024
CHUNK = 64
N_META = 16
PADF = CHUNK - N_META
A_HEADS = 8
A_KV_HEADS = 2
HEAD_DIM = 64
WINDOW = 128
KV_W = A_KV_HEADS * HEAD_DIM
Q_W = A_HEADS * HEAD_DIM
BAND = WINDOW + CHUNK
B_WIDTH = 512
B_CONV = 3
C_WIDTH = 512
C_CONV = 31
D_HEADS = 4
D_KEY = 128
D_VAL = 128
D_WIDTH = D_HEADS * D_VAL
SUB = 16
FFN_HIDDEN = 2816
DEPTH = 2
ALPHA = (2 * DEPTH) ** 0.25
LN_EPS = 1e-5
RMS_EPS = 1e-6
NEG = -1e30

AB_IN = Q_W + 2 * KV_W + 3 * B_WIDTH
CD_IN = 2 * C_WIDTH + 4 * D_WIDTH

TILE = 256
FFN_BLOCK = 256
CONV_ROWS = 32
C_HIST = C_CONV - 1
C_OFF = 32
B_OFF = 8
VMEM_LIMIT = 60 * 1024 * 1024


def _dot(a, b):
    return jnp.dot(a.astype(BF16), b.astype(BF16), preferred_element_type=F32)


def _dot_nt(a, b):
    return lax.dot_general(a.astype(BF16), b.astype(BF16), (((1,), (1,)), ((), ())),
                           preferred_element_type=F32)


def _dot_tn(a, b):
    return lax.dot_general(a.astype(BF16), b.astype(BF16), (((0,), (0,)), ((), ())),
                           preferred_element_type=F32)


def _sigmoid(x):
    return 1.0 / (1.0 + jnp.exp(-x))


def _silu(x):
    return x * _sigmoid(x)


def _layer_norm(x, g, b):
    mu = jnp.mean(x, axis=-1, keepdims=True)
    xc = x - mu
    var = jnp.mean(xc * xc, axis=-1, keepdims=True)
    return xc * lax.rsqrt(var + LN_EPS) * g + b


def _post_block(x, mix, w_o_ref, ln_ref, wgu_ref, wd_ref):
    h = _layer_norm(ALPHA * x + _dot(mix, w_o_ref[...]), ln_ref[0:1], ln_ref[1:2])
    hb = h.astype(BF16)
    acc = None
    for j in range(FFN_HIDDEN // FFN_BLOCK):
        lo = j * FFN_BLOCK
        gate = jnp.dot(hb, wgu_ref[:, lo:lo + FFN_BLOCK], preferred_element_type=F32)
        up = jnp.dot(hb, wgu_ref[:, FFN_HIDDEN + lo:FFN_HIDDEN + lo + FFN_BLOCK],
                     preferred_element_type=F32)
        part = _dot(_silu(gate) * up, wd_ref[lo:lo + FFN_BLOCK, :])
        acc = part if acc is None else acc + part
    return _layer_norm(ALPHA * h + acc, ln_ref[2:3], ln_ref[3:4])


def _attn_chunk(q, kband, vband, kvbias, alibi_ref, sink_ref):
    lane = lax.broadcasted_iota(jnp.int32, (BAND, KV_W), 1)
    low = lane < HEAD_DIM
    kswap = pltpu.roll(kband, HEAD_DIM, axis=1)
    vswap = pltpu.roll(vband, HEAD_DIM, axis=1)
    outs = []
    for g in range(A_KV_HEADS):
        qg = jnp.concatenate([q[:, 256 * g:256 * g + 128], q[:, 256 * g + 128:256 * g + 256]], axis=0)
        acc = None
        for e in range(2):
            src_k, src_v = (kband, vband) if g == e else (kswap, vswap)
            keep = low if e == 0 else jnp.logical_not(low)
            kext = jnp.where(keep, src_k, 0.0)
            vext = jnp.where(keep, src_v, 0.0)
            s = _dot_nt(qg, kext) + alibi_ref[2 * g + e] + kvbias
            sink = sink_ref[2 * g + e]
            m = jnp.maximum(jnp.max(s, axis=-1, keepdims=True), sink)
            p = jnp.exp(s - m)
            den = jnp.sum(p, axis=-1, keepdims=True) + jnp.exp(sink - m)
            part = _dot(p, vext) * (1.0 / den)
            acc = part if acc is None else acc + part
        outs += [acc[0:CHUNK], acc[CHUNK:2 * CHUNK]]
    return jnp.concatenate(outs, axis=1)


def _ab_project(x, w_in_ref, b_in_ref):
    proj = _dot(x, w_in_ref[...]) + b_in_ref[...]
    q = proj[:, 0:Q_W] * (HEAD_DIM ** -0.5)
    k = proj[:, Q_W:Q_W + KV_W]
    v = proj[:, Q_W + KV_W:Q_W + 2 * KV_W]
    o = Q_W + 2 * KV_W
    bg = proj[:, o:o + B_WIDTH]
    u = proj[:, o + B_WIDTH:o + 2 * B_WIDTH] * proj[:, o + 2 * B_WIDTH:o + 3 * B_WIDTH]
    return q, k, v, bg, u


def _short_conv(u_ref, base, rows, w_ref):
    out = None
    for j in range(B_CONV):
        term = u_ref[pl.ds(base - (B_CONV - 1) + j, rows), :] * w_ref[j:j + 1, :]
        out = term if out is None else out + term
    return out


def _l0_short_kernel(x_ref, rowmask_ref, k0_ref, v0_ref, u0_ref, kvb_ref, alibi_ref, sink_ref,
                     convw_ref, w_in_ref, b_in_ref, w_o_ref, ln_ref, wgu_ref, wd_ref,
                     h_ref, kout_ref, vout_ref, uout_ref,
                     q_s, k_s, v_s, bg_s, u_s, mix_s):
    ns = k0_ref.shape[0]
    x = x_ref[...]
    q, k, v, bg, u = _ab_project(x, w_in_ref, b_in_ref)
    q_s[...] = q
    k_s[...] = k
    v_s[...] = v
    bg_s[...] = bg
    u = u * rowmask_ref[...]

    def stream(s, carry):
        r0 = pl.multiple_of(s * CHUNK, CHUNK)
        kband = jnp.concatenate([k0_ref[s], k_s[pl.ds(r0, CHUNK), :]], axis=0)
        vband = jnp.concatenate([v0_ref[s], v_s[pl.ds(r0, CHUNK), :]], axis=0)
        kout_ref[s] = kband[CHUNK:]
        vout_ref[s] = vband[CHUNK:]
        att = _attn_chunk(q_s[pl.ds(r0, CHUNK), :], kband, vband, kvb_ref[s], alibi_ref, sink_ref)
        mix_s[pl.ds(r0, CHUNK), 0:Q_W] = att
        return carry

    lax.fori_loop(0, ns, stream, 0)

    for s in range(ns):
        r0 = s * CHUNK
        u_s[B_OFF - 2:B_OFF, :] = u0_ref[s]
        u_s[B_OFF:B_OFF + CHUNK, :] = u[r0:r0 + CHUNK]
        cb = _short_conv(u_s, B_OFF, CHUNK, convw_ref)
        mix_s[r0:r0 + CHUNK, Q_W:] = bg_s[r0:r0 + CHUNK, :] * cb
        uout_ref[s] = u_s[B_OFF + CHUNK - 2:B_OFF + CHUNK, :]

    h_ref[...] = _post_block(x, mix_s[...], w_o_ref, ln_ref, wgu_ref, wd_ref)


def _l0_main_kernel(x_ref, k0_ref, v0_ref, u0_ref, kvb_ref, alibi_ref, sink_ref,
                    convw_ref, w_in_ref, b_in_ref, w_o_ref, ln_ref, wgu_ref, wd_ref,
                    h_ref, kout_ref, vout_ref, uout_ref,
                    q_s, k_s, v_s, u_s, mix_s):
    t = pl.program_id(1)
    tm = x_ref.shape[1]

    @pl.when(t == 0)
    def _():
        k_s[0:WINDOW, :] = k0_ref[0]
        v_s[0:WINDOW, :] = v0_ref[0]
        u_s[B_OFF - 2:B_OFF, :] = u0_ref[0]

    x = x_ref[0]
    q, k, v, bg, u = _ab_project(x, w_in_ref, b_in_ref)
    q_s[...] = q
    k_s[WINDOW:, :] = k
    v_s[WINDOW:, :] = v
    u_s[B_OFF:, :] = u
    mix_s[:, Q_W:] = bg * _short_conv(u_s, B_OFF, tm, convw_ref)

    first = t == 0

    def chunk(c, carry):
        r0 = pl.multiple_of(c * CHUNK, CHUNK)
        kvbias = jnp.where(jnp.logical_and(first, c < 2), kvb_ref[jnp.minimum(c, 1)], 0.0)
        att = _attn_chunk(q_s[pl.ds(r0, CHUNK), :], k_s[pl.ds(r0, BAND), :], v_s[pl.ds(r0, BAND), :],
                          kvbias, alibi_ref, sink_ref)
        mix_s[pl.ds(r0, CHUNK), 0:Q_W] = att
        return carry

    lax.fori_loop(0, tm // CHUNK, chunk, 0)

    k_s[0:WINDOW, :] = k_s[tm:tm + WINDOW, :]
    v_s[0:WINDOW, :] = v_s[tm:tm + WINDOW, :]
    u_s[B_OFF - 2:B_OFF, :] = u_s[B_OFF + tm - 2:B_OFF + tm, :]

    @pl.when(t == pl.num_programs(1) - 1)
    def _():
        kout_ref[0] = k_s[0:WINDOW, :]
        vout_ref[0] = v_s[0:WINDOW, :]
        uout_ref[0] = u_s[B_OFF - 2:B_OFF, :]

    h_ref[0] = _post_block(x, mix_s[...], w_o_ref, ln_ref, wgu_ref, wd_ref)


def _cd_project(x, rowmask, w_in_ref, b_in_ref, lb_ref):
    proj = _dot(x, w_in_ref[...]) + b_in_ref[...]
    w = C_WIDTH
    u = proj[:, 0:w] * _sigmoid(proj[:, w:2 * w])
    q = proj[:, 2 * w:3 * w]
    d0 = lb_ref[0:1]
    d1 = lb_ref[1:2]
    mx = jnp.maximum(d0, d1)
    e0 = jnp.exp(d0 - mx)
    e1 = jnp.exp(d1 - mx)
    p0 = e0 / (e0 + e1)
    p1 = e1 / (e0 + e1)
    lb = (p0 + p1) - p0
    forget = lb + (1.0 - lb) * _sigmoid(proj[:, 3 * w:4 * w])
    kk = 1.0 - forget
    lf = jnp.log(forget)
    v = proj[:, 4 * w:5 * w]
    g = proj[:, 5 * w:6 * w]
    if rowmask is not None:
        u = u * rowmask
        q = q * rowmask
        kk = kk * rowmask
        v = v * rowmask
        lf = lf * rowmask
    return u, q, kk, v, lf, g


def _conformer_conv(c_ref, base, rows, convw_ref, cvec_ref, out_ref, out_base):
    nblk = rows // CONV_ROWS

    def block(i, carry):
        r0 = pl.multiple_of(i * CONV_ROWS, CONV_ROWS)
        win = c_ref[pl.ds(base - C_OFF + r0, CONV_ROWS + C_OFF), :]
        acc = None
        for j in range(C_CONV):
            off = C_OFF - C_HIST + j
            term = win[off:off + CONV_ROWS] * convw_ref[j:j + 1, :]
            acc = term if acc is None else acc + term
        y = _layer_norm(acc + cvec_ref[0:1], cvec_ref[1:2], cvec_ref[2:3])
        out_ref[pl.ds(out_base + r0, CONV_ROWS), 0:C_WIDTH] = _silu(y)
        return carry

    lax.fori_loop(0, nblk, block, 0)


def _hgrn_chunk(q, kk, v, lf, st_ref, sidx):
    row = lax.broadcasted_iota(jnp.int32, (CHUNK, CHUNK), 0)
    col = lax.broadcasted_iota(jnp.int32, (CHUNK, CHUNK), 1)
    tril = col <= row
    ltri = jnp.where(tril, 1.0, 0.0).astype(BF16)
    hi = lf.astype(BF16)
    r1 = lf - hi.astype(F32)
    mid = r1.astype(BF16)
    lo = (r1 - mid.astype(F32)).astype(BF16)
    cum = (jnp.dot(ltri, hi, preferred_element_type=F32) + jnp.dot(ltri, mid, preferred_element_type=F32)
           + jnp.dot(ltri, lo, preferred_element_type=F32))
    tot = cum[CHUNK - 1:CHUNK]
    nsub = CHUNK // SUB
    bases = [jnp.zeros((1, D_WIDTH), F32)] + [cum[SUB * i - 1:SUB * i] for i in range(1, nsub)]
    bsel = jnp.concatenate([jnp.broadcast_to(b, (SUB, D_WIDTH)) for b in bases], axis=0)
    qe = q * jnp.exp(cum)
    qd = q * jnp.exp(cum - bsel)
    kdec = kk * jnp.exp(tot - cum)
    etot = jnp.exp(tot)
    rsub = jnp.right_shift(lax.broadcasted_iota(jnp.int32, (CHUNK, D_KEY), 0), SUB.bit_length() - 1)
    outs = []
    for h in range(D_HEADS):
        sl = slice(D_KEY * h, D_KEY * (h + 1))
        st = st_ref[sidx, h]
        o_inter = _dot_nt(qe[:, sl], st)
        qd_h = qd[:, sl]
        qblk = jnp.concatenate([jnp.where(rsub == i, qd_h, 0.0) for i in range(nsub)], axis=1)
        kds = []
        for i in range(nsub):
            n = SUB * (i + 1)
            kd = kk[0:n, sl] * jnp.exp(bases[i][:, sl] - cum[0:n, sl])
            if n < CHUNK:
                kd = jnp.concatenate([kd, jnp.zeros((CHUNK - n, D_KEY), F32)], axis=0)
            kds.append(kd)
        att = _dot_nt(qblk, jnp.concatenate(kds, axis=1))
        att = jnp.where(tril, att, 0.0)
        outs.append(o_inter + _dot(att, v[:, sl]))
        st_ref[sidx, h] = etot[:, sl] * st + _dot_tn(v[:, sl], kdec[:, sl])
    return jnp.concatenate(outs, axis=1)


def _hgrn_readout(o, g, norm_g):
    parts = []
    for h in range(D_HEADS):
        oh = o[:, D_VAL * h:D_VAL * (h + 1)]
        parts.append(oh * lax.rsqrt(jnp.mean(oh * oh, axis=-1, keepdims=True) + RMS_EPS))
    return jnp.concatenate(parts, axis=1) * norm_g * _silu(g)


def _l1_short_kernel(x_ref, rowmask_ref, c0_ref, s0_ref, convw_ref, cvec_ref, lb_ref, normg_ref,
                     w_in_ref, b_in_ref, w_o_ref, ln_ref, wgu_ref, wd_ref,
                     h_ref, cout_ref, sout_ref,
                     c_s, q_s, kk_s, v_s, lf_s, g_s, st_s, mix_s):
    ns = c0_ref.shape[0]
    x = x_ref[...]
    u, q, kk, v, lf, g = _cd_project(x, rowmask_ref[...], w_in_ref, b_in_ref, lb_ref)
    q_s[...] = q
    kk_s[...] = kk
    v_s[...] = v
    lf_s[...] = lf
    g_s[...] = g

    for s in range(ns):
        r0 = s * CHUNK
        c_s[C_OFF - C_HIST:C_OFF, :] = c0_ref[s]
        c_s[C_OFF:C_OFF + CHUNK, :] = u[r0:r0 + CHUNK]
        _conformer_conv(c_s, C_OFF, CHUNK, convw_ref, cvec_ref, mix_s, r0)
        cout_ref[s] = c_s[C_OFF + CHUNK - C_HIST:C_OFF + CHUNK, :]

    def stream(s, carry):
        r0 = pl.multiple_of(s * CHUNK, CHUNK)
        for h in range(D_HEADS):
            st_s[0, h] = s0_ref[s, h].T
        rows = pl.ds(r0, CHUNK)
        o = _hgrn_chunk(q_s[rows, :], kk_s[rows, :], v_s[rows, :], lf_s[rows, :], st_s, 0)
        mix_s[rows, C_WIDTH:] = _hgrn_readout(o, g_s[rows, :], normg_ref[...])
        for h in range(D_HEADS):
            sout_ref[s, h] = st_s[0, h].T
        return carry

    lax.fori_loop(0, ns, stream, 0)

    h_ref[...] = _post_block(x, mix_s[...], w_o_ref, ln_ref, wgu_ref, wd_ref)


def _l1_main_kernel(x_ref, c0_ref, s0_ref, convw_ref, cvec_ref, lb_ref, normg_ref,
                    w_in_ref, b_in_ref, w_o_ref, ln_ref, wgu_ref, wd_ref,
                    h_ref, cout_ref, sout_ref,
                    c_s, q_s, kk_s, v_s, lf_s, g_s, st_s, mix_s):
    t = pl.program_id(1)
    tm = x_ref.shape[1]

    @pl.when(t == 0)
    def _():
        c_s[C_OFF - C_HIST:C_OFF, :] = c0_ref[0]
        for h in range(D_HEADS):
            st_s[0, h] = s0_ref[0, h].T

    x = x_ref[0]
    u, q, kk, v, lf, g = _cd_project(x, None, w_in_ref, b_in_ref, lb_ref)
    q_s[...] = q
    kk_s[...] = kk
    v_s[...] = v
    lf_s[...] = lf
    g_s[...] = g
    c_s[C_OFF:, :] = u
    _conformer_conv(c_s, C_OFF, tm, convw_ref, cvec_ref, mix_s, 0)
    c_s[C_OFF - C_HIST:C_OFF, :] = c_s[C_OFF + tm - C_HIST:C_OFF + tm, :]

    def chunk(c, carry):
        rows = pl.ds(pl.multiple_of(c * CHUNK, CHUNK), CHUNK)
        o = _hgrn_chunk(q_s[rows, :], kk_s[rows, :], v_s[rows, :], lf_s[rows, :], st_s, 0)
        mix_s[rows, C_WIDTH:] = _hgrn_readout(o, g_s[rows, :], normg_ref[...])
        return carry

    lax.fori_loop(0, tm // CHUNK, chunk, 0)

    @pl.when(t == pl.num_programs(1) - 1)
    def _():
        cout_ref[0] = c_s[C_OFF - C_HIST:C_OFF, :]
        for h in range(D_HEADS):
            sout_ref[0, h] = st_s[0, h].T

    h_ref[0] = _post_block(x, mix_s[...], w_o_ref, ln_ref, wgu_ref, wd_ref)


def _whole(shape):
    zeros = (0,) * len(shape)
    return pl.BlockSpec(shape, lambda *_: zeros, pipeline_mode=pl.Buffered(1))


def _whole_out(shape):
    zeros = (0,) * len(shape)
    return pl.BlockSpec(shape, lambda *_: zeros)


def _params(n_axes):
    return pltpu.CompilerParams(dimension_semantics=("arbitrary",) * n_axes,
                                vmem_limit_bytes=VMEM_LIMIT)


def _l0_short(x, rowmask, k0, v0, u0, kvb, consts, weights):
    rows, ns = x.shape[0], k0.shape[0]
    ins = (x, rowmask, k0, v0, u0, kvb) + consts + weights
    return pl.pallas_call(
        _l0_short_kernel,
        grid=(1,),
        in_specs=[_whole(a.shape) for a in ins],
        out_specs=[_whole_out((rows, D_MODEL)), _whole_out((ns, WINDOW, KV_W)),
                   _whole_out((ns, WINDOW, KV_W)), _whole_out((ns, B_CONV - 1, B_WIDTH))],
        out_shape=[jax.ShapeDtypeStruct((rows, D_MODEL), F32),
                   jax.ShapeDtypeStruct((ns, WINDOW, KV_W), F32),
                   jax.ShapeDtypeStruct((ns, WINDOW, KV_W), F32),
                   jax.ShapeDtypeStruct((ns, B_CONV - 1, B_WIDTH), F32)],
        scratch_shapes=[pltpu.VMEM((rows, Q_W), F32), pltpu.VMEM((rows, KV_W), F32),
                        pltpu.VMEM((rows, KV_W), F32), pltpu.VMEM((rows, B_WIDTH), F32),
                        pltpu.VMEM((B_OFF + CHUNK, B_WIDTH), F32), pltpu.VMEM((rows, D_MODEL), F32)],
        compiler_params=_params(1),
        name="l0_short",
    )(*ins)


def _l0_main(x, k0, v0, u0, kvb, consts, weights, meta_idx):
    nb, seq, _ = x.shape
    nt = seq // TILE
    pick = lambda shape: pl.BlockSpec((1,) + shape, lambda b, t: (meta_idx, 0, 0))
    per_b = lambda shape: pl.BlockSpec((1,) + shape, lambda b, t: (b, 0, 0))
    ins = (x, k0, v0, u0, kvb) + consts + weights
    in_specs = [pl.BlockSpec((1, TILE, D_MODEL), lambda b, t: (b, t, 0)),
                pick((WINDOW, KV_W)), pick((WINDOW, KV_W)), pick((B_CONV - 1, B_WIDTH)),
                _whole(kvb.shape)] + [_whole(a.shape) for a in consts + weights]
    return pl.pallas_call(
        _l0_main_kernel,
        grid=(nb, nt),
        in_specs=in_specs,
        out_specs=[pl.BlockSpec((1, TILE, D_MODEL), lambda b, t: (b, t, 0)),
                   per_b((WINDOW, KV_W)), per_b((WINDOW, KV_W)), per_b((B_CONV - 1, B_WIDTH))],
        out_shape=[jax.ShapeDtypeStruct((nb, seq, D_MODEL), F32),
                   jax.ShapeDtypeStruct((nb, WINDOW, KV_W), F32),
                   jax.ShapeDtypeStruct((nb, WINDOW, KV_W), F32),
                   jax.ShapeDtypeStruct((nb, B_CONV - 1, B_WIDTH), F32)],
        scratch_shapes=[pltpu.VMEM((TILE, Q_W), F32), pltpu.VMEM((WINDOW + TILE, KV_W), F32),
                        pltpu.VMEM((WINDOW + TILE, KV_W), F32), pltpu.VMEM((B_OFF + TILE, B_WIDTH), F32),
                        pltpu.VMEM((TILE, D_MODEL), F32)],
        compiler_params=_params(2),
        name="l0_main",
    )(*ins)


def _l1_scratch(rows, conv_rows):
    wide = lambda: pltpu.VMEM((rows, D_WIDTH), F32)
    return [pltpu.VMEM((C_OFF + conv_rows, C_WIDTH), F32), wide(), wide(), wide(), wide(), wide(),
            pltpu.VMEM((1, D_HEADS, D_VAL, D_KEY), F32), pltpu.VMEM((rows, D_MODEL), F32)]


def _l1_short(x, rowmask, c0, s0, consts, weights):
    rows, ns = x.shape[0], c0.shape[0]
    ins = (x, rowmask, c0, s0) + consts + weights
    return pl.pallas_call(
        _l1_short_kernel,
        grid=(1,),
        in_specs=[_whole(a.shape) for a in ins],
        out_specs=[_whole_out((rows, D_MODEL)), _whole_out((ns, C_HIST, C_WIDTH)),
                   _whole_out((ns, D_HEADS, D_KEY, D_VAL))],
        out_shape=[jax.ShapeDtypeStruct((rows, D_MODEL), F32),
                   jax.ShapeDtypeStruct((ns, C_HIST, C_WIDTH), F32),
                   jax.ShapeDtypeStruct((ns, D_HEADS, D_KEY, D_VAL), F32)],
        scratch_shapes=_l1_scratch(rows, CHUNK),
        compiler_params=_params(1),
        name="l1_short",
    )(*ins)


def _l1_main(x, c0, s0, consts, weights, meta_idx):
    nb, seq, _ = x.shape
    nt = seq // TILE
    ins = (x, c0, s0) + consts + weights
    in_specs = [pl.BlockSpec((1, TILE, D_MODEL), lambda b, t: (b, t, 0)),
                pl.BlockSpec((1, C_HIST, C_WIDTH), lambda b, t: (meta_idx, 0, 0)),
                pl.BlockSpec((1, D_HEADS, D_KEY, D_VAL), lambda b, t: (meta_idx, 0, 0, 0)),
                ] + [_whole(a.shape) for a in consts + weights]
    return pl.pallas_call(
        _l1_main_kernel,
        grid=(nb, nt),
        in_specs=in_specs,
        out_specs=[pl.BlockSpec((1, TILE, D_MODEL), lambda b, t: (b, t, 0)),
                   pl.BlockSpec((1, C_HIST, C_WIDTH), lambda b, t: (b, 0, 0)),
                   pl.BlockSpec((1, D_HEADS, D_KEY, D_VAL), lambda b, t: (b, 0, 0, 0))],
        out_shape=[jax.ShapeDtypeStruct((nb, seq, D_MODEL), F32),
                   jax.ShapeDtypeStruct((nb, C_HIST, C_WIDTH), F32),
                   jax.ShapeDtypeStruct((nb, D_HEADS, D_KEY, D_VAL), F32)],
        scratch_shapes=_l1_scratch(TILE, TILE),
        compiler_params=_params(2),
        name="l1_main",
    )(*ins)


def _alibi_tables():
    slopes = np.exp2(-8.0 * np.arange(1, A_HEADS + 1, dtype=np.float32) / A_HEADS).astype(np.float32)
    i = np.arange(CHUNK, dtype=np.float32)[:, None]
    j = np.arange(BAND, dtype=np.float32)[None, :]
    dist = np.abs(WINDOW + i - j).astype(np.float32)
    bias = np.zeros((2 * A_KV_HEADS, 2 * CHUNK, BAND), np.float32)
    heads = np.zeros((2 * A_KV_HEADS, 2 * CHUNK), np.int32)
    for g in range(A_KV_HEADS):
        for e in range(2):
            for half, head in enumerate((4 * g + e, 4 * g + 2 + e)):
                rows = slice(CHUNK * half, CHUNK * (half + 1))
                bias[2 * g + e, rows] = -slopes[head] * dist
                heads[2 * g + e, rows] = head
    return bias, heads


def kernel(x_prompt, x_sample, cache_k_a, cache_v_a, state_conv_b, state_conv_c, state_hgrn, meta_tokens,
           ab_w_in, ab_b_in, a_sinks, b_conv_w, ab_w_o, cd_w_in, cd_b_in, c_conv_w, c_conv_b, c_ln_g,
           c_ln_b, d_lower_bounds, d_norm_g, cd_w_o, ln1_g, ln1_b, ln2_g, ln2_b, ffn_w_gu, ffn_w_down):
    nb = x_prompt.shape[0]
    ns = x_sample.shape[0]
    assert x_sample.shape[1] == CHUNK and cache_k_a.shape[1] == WINDOW
    assert x_prompt.shape[1] % TILE == 0
    meta_idx = ns
    n_short = ns + 1
    rows = n_short * CHUNK

    xs = jnp.concatenate([x_sample.reshape(ns * CHUNK, D_MODEL),
                          jnp.zeros((PADF, D_MODEL), F32), meta_tokens.astype(F32)], axis=0)
    rowmask = np.ones((rows, 1), np.float32)
    rowmask[ns * CHUNK:ns * CHUNK + PADF] = 0.0
    rowmask = jnp.asarray(rowmask)

    kvb_short = np.zeros((n_short, 1, BAND), np.float32)
    kvb_short[meta_idx, 0, :WINDOW + PADF] = NEG
    kvb_main = np.zeros((2, 1, BAND), np.float32)
    kvb_main[0, 0, :CHUNK + PADF] = NEG
    kvb_main[1, 0, :PADF] = NEG

    alibi, sink_heads = _alibi_tables()
    sinks = a_sinks.astype(F32)[jnp.asarray(sink_heads)][..., None]
    a_consts = (jnp.asarray(alibi), sinks, b_conv_w.astype(F32))

    zpad = lambda a: jnp.concatenate([a, jnp.zeros((1,) + a.shape[1:], a.dtype)], axis=0)
    k0 = zpad(cache_k_a.reshape(ns, WINDOW, KV_W))
    v0 = zpad(cache_v_a.reshape(ns, WINDOW, KV_W))
    u0 = zpad(state_conv_b)
    c0 = zpad(state_conv_c)
    s0 = zpad(state_hgrn)

    def layer_weights(l, w_in, b_in, w_o):
        ln = jnp.stack([ln1_g[l], ln1_b[l], ln2_g[l], ln2_b[l]]).astype(F32)
        return (w_in.astype(BF16), b_in.astype(F32)[None, :], w_o.astype(BF16), ln,
                ffn_w_gu[l].astype(BF16), ffn_w_down[l].astype(BF16))

    w0 = layer_weights(0, ab_w_in, ab_b_in, ab_w_o)
    w1 = layer_weights(1, cd_w_in, cd_b_in, cd_w_o)

    hs0, k_s, v_s, u_s = _l0_short(xs, rowmask, k0, v0, u0, jnp.asarray(kvb_short), a_consts, w0)
    hp0, k_p, v_p, u_p = _l0_main(x_prompt, k_s, v_s, u_s, jnp.asarray(kvb_main), a_consts, w0, meta_idx)

    cvec = jnp.stack([c_conv_b, c_ln_g, c_ln_b]).astype(F32)
    c_consts = (c_conv_w.astype(F32), cvec, d_lower_bounds.astype(F32), d_norm_g.astype(F32)[None, :])
    hs1, c_s, s_s = _l1_short(hs0, rowmask, c0, s0, c_consts, w1)
    hp1, c_p, s_p = _l1_main(hp0, c_s, s_s, c_consts, w1, meta_idx)

    kv = lambda a: a.reshape(a.shape[0], WINDOW, A_KV_HEADS, HEAD_DIM)
    y_sample = hs1[:ns * CHUNK].reshape(ns, CHUNK, D_MODEL)
    return (hp1, y_sample, kv(k_p), kv(v_p), u_p, c_p, s_p,
            kv(k_s[:ns]), kv(v_s[:ns]), u_s[:ns], c_s[:ns], s_s[:ns])
```

```python
import functools

import jax
import jax.numpy as jnp
import numpy as np
from jax import lax
from jax.experimental import pallas as pl
from jax.experimental.pallas import tpu as pltpu

F32 = jnp.float32
BF16 = jnp.bfloat16

D_MODEL = 1024
CHUNK = 64
N_META = 16
PADF = CHUNK - N_META
A_HEADS = 8
A_KV_HEADS = 2
HEAD_DIM = 64
WINDOW = 128
KV_W = A_KV_HEADS * HEAD_DIM
Q_W = A_HEADS * HEAD_DIM
BAND = WINDOW + CHUNK
B_WIDTH = 512
B_CONV = 3
C_WIDTH = 512
C_CONV = 31
D_HEADS = 4
D_KEY = 128
D_VAL = 128
D_WIDTH = D_HEADS * D_VAL
SUB = 16
FFN_HIDDEN = 2816
DEPTH = 2
ALPHA = (2 * DEPTH) ** 0.25
LN_EPS = 1e-5
RMS_EPS = 1e-6
NEG = -1e30

AB_IN = Q_W + 2 * KV_W + 3 * B_WIDTH
CD_IN = 2 * C_WIDTH + 4 * D_WIDTH

TILE = 256
FFN_BLOCK = 256
SUBLANES = 8
CONV_ROWS = 32
C_HIST = C_CONV - 1
C_OFF = 32
B_OFF = 8
VMEM_LIMIT = 60 * 1024 * 1024


def _dot(a, b):
    return jnp.dot(a.astype(BF16), b.astype(BF16), preferred_element_type=F32)


def _dot_nt(a, b):
    return lax.dot_general(a.astype(BF16), b.astype(BF16), (((1,), (1,)), ((), ())),
                           preferred_element_type=F32)


def _dot_tn(a, b):
    return lax.dot_general(a.astype(BF16), b.astype(BF16), (((0,), (0,)), ((), ())),
                           preferred_element_type=F32)


def _sigmoid(x):
    return 1.0 / (1.0 + jnp.exp(-x))


def _silu(x):
    return x * _sigmoid(x)


def _layer_norm(x, g, b):
    mu = jnp.mean(x, axis=-1, keepdims=True)
    xc = x - mu
    var = jnp.mean(xc * xc, axis=-1, keepdims=True)
    return xc * lax.rsqrt(var + LN_EPS) * g + b


def _post_block(x, mix, w_o_ref, ln_ref, wgu_ref, wd_ref):
    h = _layer_norm(ALPHA * x + _dot(mix, w_o_ref[...]), ln_ref[0:1], ln_ref[1:2])
    hb = h.astype(BF16)
    acc = None
    for j in range(FFN_HIDDEN // FFN_BLOCK):
        lo = j * FFN_BLOCK
        gate = jnp.dot(hb, wgu_ref[:, lo:lo + FFN_BLOCK], preferred_element_type=F32)
        up = jnp.dot(hb, wgu_ref[:, FFN_HIDDEN + lo:FFN_HIDDEN + lo + FFN_BLOCK],
                     preferred_element_type=F32)
        part = _dot(_silu(gate) * up, wd_ref[lo:lo + FFN_BLOCK, :])
        acc = part if acc is None else acc + part
    return _layer_norm(ALPHA * h + acc, ln_ref[2:3], ln_ref[3:4])


def _attn_chunks(chunks, alibi_ref, sink_ref):
    lane = lax.broadcasted_iota(jnp.int32, (BAND, KV_W), 1)
    low = lane < HEAD_DIM
    high = jnp.logical_not(low)

    def extend(band, swapped, g, e):
        src = band if g == e else swapped
        return jnp.where(low if e == 0 else high, src, 0.0).astype(BF16)

    scores = []
    for q, kband, _, kvbias in chunks:
        kswap = pltpu.roll(kband, HEAD_DIM, axis=1)
        for g in range(A_KV_HEADS):
            qg = jnp.concatenate([q[:, 256 * g:256 * g + 128], q[:, 256 * g + 128:256 * g + 256]],
                                 axis=0).astype(BF16)
            k2 = jnp.concatenate([extend(kband, kswap, g, e) for e in range(2)], axis=0)
            s = _dot_nt(k2, qg) + alibi_ref[g]
            for e in range(2):
                se = s[BAND * e:BAND * (e + 1)]
                scores.append(se if kvbias is None else se + kvbias)

    probs = []
    for idx, s in enumerate(scores):
        sink = sink_ref[idx % (2 * A_KV_HEADS)]
        m = jnp.maximum(jnp.max(s, axis=0, keepdims=True), sink)
        p = jnp.exp(s - m)
        den = jnp.sum(p, axis=0, keepdims=True) + jnp.exp(sink - m)
        probs.append((p * (1.0 / den)).astype(BF16))

    outs = []
    for ci, (_, _, vband, _) in enumerate(chunks):
        vswap = pltpu.roll(vband, HEAD_DIM, axis=1)
        parts = []
        for g in range(A_KV_HEADS):
            base = ci * 2 * A_KV_HEADS + 2 * g
            p2 = jnp.concatenate([probs[base], probs[base + 1]], axis=0)
            v2 = jnp.concatenate([extend(vband, vswap, g, e) for e in range(2)], axis=0)
            o = _dot_tn(p2, v2)
            parts += [o[0:CHUNK], o[CHUNK:2 * CHUNK]]
        outs.append(jnp.concatenate(parts, axis=1))
    return outs


def _ab_project(x, w_in_ref, b_in_ref):
    proj = _dot(x, w_in_ref[...]) + b_in_ref[...]
    q = proj[:, 0:Q_W] * (HEAD_DIM ** -0.5)
    k = proj[:, Q_W:Q_W + KV_W]
    v = proj[:, Q_W + KV_W:Q_W + 2 * KV_W]
    o = Q_W + 2 * KV_W
    bg = proj[:, o:o + B_WIDTH]
    u = proj[:, o + B_WIDTH:o + 2 * B_WIDTH] * proj[:, o + 2 * B_WIDTH:o + 3 * B_WIDTH]
    return q, k, v, bg, u


def _short_conv(u_ref, base, rows, w_ref):
    out = None
    for j in range(B_CONV):
        term = u_ref[pl.ds(base - (B_CONV - 1) + j, rows), :] * w_ref[j:j + 1, :]
        out = term if out is None else out + term
    return out


def _l0_short_kernel(x_ref, rowmask_ref, k0_ref, v0_ref, u0_ref, kvb_ref, alibi_ref, sink_ref,
                     convw_ref, w_in_ref, b_in_ref, w_o_ref, ln_ref, wgu_ref, wd_ref,
                     h_ref, kout_ref, vout_ref, uout_ref,
                     u_s, mix_s):
    ns = k0_ref.shape[0]
    x = x_ref[...]
    q, k, v, bg, u = _ab_project(x, w_in_ref, b_in_ref)
    u = u * rowmask_ref[...]

    chunks = []
    for s in range(ns):
        r0 = s * CHUNK
        kband = jnp.concatenate([k0_ref[s], k[r0:r0 + CHUNK]], axis=0)
        vband = jnp.concatenate([v0_ref[s], v[r0:r0 + CHUNK]], axis=0)
        kout_ref[s] = kband[CHUNK:]
        vout_ref[s] = vband[CHUNK:]
        chunks.append((q[r0:r0 + CHUNK], kband, vband, kvb_ref[...] if s == ns - 1 else None))
    for s, att in enumerate(_attn_chunks(chunks, alibi_ref, sink_ref)):
        mix_s[s * CHUNK:(s + 1) * CHUNK, 0:Q_W] = att

    for s in range(ns):
        r0 = s * CHUNK
        us = u_s.at[s]
        us[B_OFF - 2:B_OFF, :] = u0_ref[s]
        us[B_OFF:B_OFF + CHUNK, :] = u[r0:r0 + CHUNK]
        cb = _short_conv(us, B_OFF, CHUNK, convw_ref)
        mix_s[r0:r0 + CHUNK, Q_W:] = bg[r0:r0 + CHUNK] * cb
        uout_ref[s] = u[r0 + CHUNK - 2:r0 + CHUNK]

    h_ref[...] = _post_block(x, mix_s[...], w_o_ref, ln_ref, wgu_ref, wd_ref)


def _l0_main_kernel(x_ref, k0_ref, v0_ref, u0_ref, kvb_ref, alibi_ref, sink_ref,
                    convw_ref, w_in_ref, b_in_ref, w_o_ref, ln_ref, wgu_ref, wd_ref,
                    h_ref, kout_ref, vout_ref, uout_ref,
                    k_s, v_s, u_s, mix_s):
    t = pl.program_id(1)
    tm = x_ref.shape[1]

    @pl.when(t == 0)
    def _():
        k_s[0:WINDOW, :] = k0_ref[0]
        v_s[0:WINDOW, :] = v0_ref[0]
        u_s[B_OFF - 2:B_OFF, :] = u0_ref[0]

    x = x_ref[0]
    q, k, v, bg, u = _ab_project(x, w_in_ref, b_in_ref)
    k_s[WINDOW:, :] = k
    v_s[WINDOW:, :] = v
    u_s[B_OFF:, :] = u
    mix_s[:, Q_W:] = bg * _short_conv(u_s, B_OFF, tm, convw_ref)

    first = t == 0

    chunks = []
    for c in range(tm // CHUNK):
        r0 = c * CHUNK
        kvbias = jnp.where(first, kvb_ref[c], 0.0) if c < 2 else None
        chunks.append((q[r0:r0 + CHUNK], k_s[r0:r0 + BAND, :], v_s[r0:r0 + BAND, :], kvbias))
    for c, att in enumerate(_attn_chunks(chunks, alibi_ref, sink_ref)):
        mix_s[c * CHUNK:(c + 1) * CHUNK, 0:Q_W] = att

    k_s[0:WINDOW, :] = k_s[tm:tm + WINDOW, :]
    v_s[0:WINDOW, :] = v_s[tm:tm + WINDOW, :]
    u_s[B_OFF - 2:B_OFF, :] = u_s[B_OFF + tm - 2:B_OFF + tm, :]

    @pl.when(t == pl.num_programs(1) - 1)
    def _():
        kout_ref[0] = k_s[0:WINDOW, :]
        vout_ref[0] = v_s[0:WINDOW, :]
        uout_ref[0] = u_s[B_OFF - 2:B_OFF, :]

    h_ref[0] = _post_block(x, mix_s[...], w_o_ref, ln_ref, wgu_ref, wd_ref)


def _cd_project(x, rowmask, w_in_ref, b_in_ref, lb_ref):
    proj = _dot(x, w_in_ref[...]) + b_in_ref[...]
    w = C_WIDTH
    u = proj[:, 0:w] * _sigmoid(proj[:, w:2 * w])
    q = proj[:, 2 * w:3 * w]
    d0 = lb_ref[0:1]
    d1 = lb_ref[1:2]
    mx = jnp.maximum(d0, d1)
    e0 = jnp.exp(d0 - mx)
    e1 = jnp.exp(d1 - mx)
    p0 = e0 / (e0 + e1)
    p1 = e1 / (e0 + e1)
    lb = (p0 + p1) - p0
    forget = lb + (1.0 - lb) * _sigmoid(proj[:, 3 * w:4 * w])
    kk = 1.0 - forget
    lf = jnp.log(forget)
    v = proj[:, 4 * w:5 * w]
    g = proj[:, 5 * w:6 * w]
    if rowmask is not None:
        u = u * rowmask
        q = q * rowmask
        kk = kk * rowmask
        v = v * rowmask
        lf = lf * rowmask
    return u, q, kk, v, lf, g


def _conformer_conv(c_ref, base, rows, convw_ref, cvec_ref, out_ref, out_base):
    nblk = rows // CONV_ROWS

    def block(i, carry):
        r0 = pl.multiple_of(i * CONV_ROWS, CONV_ROWS)
        win = c_ref[pl.ds(base - C_OFF + r0, CONV_ROWS + C_OFF), :]
        acc = None
        for phase in range(SUBLANES):
            shifted = win if phase == 0 else pltpu.roll(win, CONV_ROWS + C_OFF - phase, axis=0)
            for j in range(C_CONV):
                off = C_OFF - C_HIST + j
                if off % SUBLANES != phase:
                    continue
                lo = off - phase
                term = shifted[lo:lo + CONV_ROWS] * convw_ref[j:j + 1, :]
                acc = term if acc is None else acc + term
        y = _layer_norm(acc + cvec_ref[0:1], cvec_ref[1:2], cvec_ref[2:3])
        out_ref[pl.ds(out_base + r0, CONV_ROWS), 0:C_WIDTH] = _silu(y)
        return carry

    lax.fori_loop(0, nblk, block, 0)


def _hgrn_chunks(chunks, init_states):
    row = lax.broadcasted_iota(jnp.int32, (CHUNK, CHUNK), 0)
    col = lax.broadcasted_iota(jnp.int32, (CHUNK, CHUNK), 1)
    tril = col <= row
    ltri = jnp.where(tril, 1.0, 0.0).astype(BF16)
    nsub = CHUNK // SUB
    rsub = jnp.right_shift(lax.broadcasted_iota(jnp.int32, (CHUNK, D_KEY), 0), SUB.bit_length() - 1)
    heads = [slice(D_KEY * h, D_KEY * (h + 1)) for h in range(D_HEADS)]

    cums = []
    for _, _, _, lf in chunks:
        hi = lf.astype(BF16)
        r1 = lf - hi.astype(F32)
        mid = r1.astype(BF16)
        lo = (r1 - mid.astype(F32)).astype(BF16)
        cums.append(jnp.dot(ltri, hi, preferred_element_type=F32)
                    + jnp.dot(ltri, mid, preferred_element_type=F32)
                    + jnp.dot(ltri, lo, preferred_element_type=F32))

    pre = []
    for (q, kk, v, _), cum in zip(chunks, cums):
        tot = cum[CHUNK - 1:CHUNK]
        bases = [jnp.zeros((1, D_WIDTH), F32)] + [cum[SUB * i - 1:SUB * i] for i in range(1, nsub)]
        bsel = jnp.concatenate([jnp.broadcast_to(b, (SUB, D_WIDTH)) for b in bases], axis=0)
        qe = (q * jnp.exp(cum)).astype(BF16)
        qd = q * jnp.exp(cum - bsel)
        kdec = (kk * jnp.exp(tot - cum)).astype(BF16)
        qblks, kdcs = [], []
        for sl in heads:
            qd_h = qd[:, sl]
            qblks.append(jnp.concatenate([jnp.where(rsub == i, qd_h, 0.0) for i in range(nsub)],
                                         axis=1).astype(BF16))
            kds = []
            for i in range(nsub):
                n = SUB * (i + 1)
                kd = kk[0:n, sl] * jnp.exp(bases[i][:, sl] - cum[0:n, sl])
                if n < CHUNK:
                    kd = jnp.concatenate([kd, jnp.zeros((CHUNK - n, D_KEY), F32)], axis=0)
                kds.append(kd)
            kdcs.append(jnp.concatenate(kds, axis=1).astype(BF16))
        pre.append((qe, kdec, v.astype(BF16), jnp.exp(tot), qblks, kdcs))

    atts, incs = [], []
    for qe, kdec, vb, etot, qblks, kdcs in pre:
        atts.append([jnp.where(tril, _dot_nt(qblks[h], kdcs[h]), 0.0).astype(BF16) for h in range(D_HEADS)])
        incs.append([_dot_tn(vb[:, sl], kdec[:, sl]) for sl in heads])

    intra = [[_dot(atts[c][h], pre[c][2][:, heads[h]]) for h in range(D_HEADS)] for c in range(len(chunks))]

    outs, ends = [], []
    state = None
    for c, (qe, kdec, vb, etot, _, _) in enumerate(pre):
        if init_states[c] is not None:
            state = init_states[c]
        o = [_dot_nt(qe[:, heads[h]], state[h]) + intra[c][h] for h in range(D_HEADS)]
        state = [etot[:, heads[h]] * state[h] + incs[c][h] for h in range(D_HEADS)]
        outs.append(jnp.concatenate(o, axis=1))
        ends.append(state)
    return outs, ends


def _hgrn_readout(o, g, norm_g):
    parts = []
    for h in range(D_HEADS):
        oh = o[:, D_VAL * h:D_VAL * (h + 1)]
        parts.append(oh * lax.rsqrt(jnp.mean(oh * oh, axis=-1, keepdims=True) + RMS_EPS))
    return jnp.concatenate(parts, axis=1) * norm_g * _silu(g)


def _l1_short_kernel(x_ref, rowmask_ref, c0_ref, s0_ref, convw_ref, cvec_ref, lb_ref, normg_ref,
                     w_in_ref, b_in_ref, w_o_ref, ln_ref, wgu_ref, wd_ref,
                     h_ref, cout_ref, sout_ref,
                     c_s, mix_s):
    ns = c0_ref.shape[0]
    x = x_ref[...]
    u, q, kk, v, lf, g = _cd_project(x, rowmask_ref[...], w_in_ref, b_in_ref, lb_ref)

    for s in range(ns):
        r0 = s * CHUNK
        cs = c_s.at[s]
        cs[C_OFF - C_HIST:C_OFF, :] = c0_ref[s]
        cs[C_OFF:C_OFF + CHUNK, :] = u[r0:r0 + CHUNK]
        _conformer_conv(cs, C_OFF, CHUNK, convw_ref, cvec_ref, mix_s, r0)
        cout_ref[s] = cs[C_OFF + CHUNK - C_HIST:C_OFF + CHUNK, :]

    rows = [slice(s * CHUNK, (s + 1) * CHUNK) for s in range(ns)]
    outs, ends = _hgrn_chunks([(q[r], kk[r], v[r], lf[r]) for r in rows],
                              [[s0_ref[s, h].T for h in range(D_HEADS)] for s in range(ns)])
    for s in range(ns):
        mix_s[rows[s], C_WIDTH:] = _hgrn_readout(outs[s], g[rows[s]], normg_ref[...])
        for h in range(D_HEADS):
            sout_ref[s, h] = ends[s][h].T

    h_ref[...] = _post_block(x, mix_s[...], w_o_ref, ln_ref, wgu_ref, wd_ref)


def _l1_main_kernel(x_ref, c0_ref, s0_ref, convw_ref, cvec_ref, lb_ref, normg_ref,
                    w_in_ref, b_in_ref, w_o_ref, ln_ref, wgu_ref, wd_ref,
                    h_ref, cout_ref, sout_ref,
                    c_s, st_s, mix_s):
    t = pl.program_id(1)
    tm = x_ref.shape[1]

    @pl.when(t == 0)
    def _():
        c_s[C_OFF - C_HIST:C_OFF, :] = c0_ref[0]
        for h in range(D_HEADS):
            st_s[h] = s0_ref[0, h].T

    x = x_ref[0]
    u, q, kk, v, lf, g = _cd_project(x, None, w_in_ref, b_in_ref, lb_ref)
    c_s[C_OFF:, :] = u
    _conformer_conv(c_s, C_OFF, tm, convw_ref, cvec_ref, mix_s, 0)
    c_s[C_OFF - C_HIST:C_OFF, :] = c_s[C_OFF + tm - C_HIST:C_OFF + tm, :]

    nc = tm // CHUNK
    rows = [slice(c * CHUNK, (c + 1) * CHUNK) for c in range(nc)]
    outs, ends = _hgrn_chunks([(q[r], kk[r], v[r], lf[r]) for r in rows],
                              [[st_s[h] for h in range(D_HEADS)]] + [None] * (nc - 1))
    for c in range(nc):
        mix_s[rows[c], C_WIDTH:] = _hgrn_readout(outs[c], g[rows[c]], normg_ref[...])
    for h in range(D_HEADS):
        st_s[h] = ends[-1][h]

    @pl.when(t == pl.num_programs(1) - 1)
    def _():
        cout_ref[0] = c_s[C_OFF - C_HIST:C_OFF, :]
        for h in range(D_HEADS):
            sout_ref[0, h] = st_s[h].T

    h_ref[0] = _post_block(x, mix_s[...], w_o_ref, ln_ref, wgu_ref, wd_ref)


def _whole(shape):
    zeros = (0,) * len(shape)
    return pl.BlockSpec(shape, lambda *_: zeros, pipeline_mode=pl.Buffered(1))


def _whole_out(shape):
    zeros = (0,) * len(shape)
    return pl.BlockSpec(shape, lambda *_: zeros)


def _params(n_axes):
    return pltpu.CompilerParams(dimension_semantics=("arbitrary",) * n_axes,
                                vmem_limit_bytes=VMEM_LIMIT)


def _l0_short(x, rowmask, k0, v0, u0, kvb, consts, weights):
    rows, ns = x.shape[0], k0.shape[0]
    ins = (x, rowmask, k0, v0, u0, kvb) + consts + weights
    return pl.pallas_call(
        _l0_short_kernel,
        grid=(1,),
        in_specs=[_whole(a.shape) for a in ins],
        out_specs=[_whole_out((rows, D_MODEL)), _whole_out((ns, WINDOW, KV_W)),
                   _whole_out((ns, WINDOW, KV_W)), _whole_out((ns, B_CONV - 1, B_WIDTH))],
        out_shape=[jax.ShapeDtypeStruct((rows, D_MODEL), F32),
                   jax.ShapeDtypeStruct((ns, WINDOW, KV_W), F32),
                   jax.ShapeDtypeStruct((ns, WINDOW, KV_W), F32),
                   jax.ShapeDtypeStruct((ns, B_CONV - 1, B_WIDTH), F32)],
        scratch_shapes=[pltpu.VMEM((ns, B_OFF + CHUNK, B_WIDTH), F32), pltpu.VMEM((rows, D_MODEL), F32)],
        compiler_params=_params(1),
        name="l0_short",
    )(*ins)


def _l0_main(x, k0, v0, u0, kvb, consts, weights, meta_idx):
    nb, seq, _ = x.shape
    nt = seq // TILE
    pick = lambda shape: pl.BlockSpec((1,) + shape, lambda b, t: (meta_idx, 0, 0))
    per_b = lambda shape: pl.BlockSpec((1,) + shape, lambda b, t: (b, 0, 0))
    ins = (x, k0, v0, u0, kvb) + consts + weights
    in_specs = [pl.BlockSpec((1, TILE, D_MODEL), lambda b, t: (b, t, 0)),
                pick((WINDOW, KV_W)), pick((WINDOW, KV_W)), pick((B_CONV - 1, B_WIDTH)),
                _whole(kvb.shape)] + [_whole(a.shape) for a in consts + weights]
    return pl.pallas_call(
        _l0_main_kernel,
        grid=(nb, nt),
        in_specs=in_specs,
        out_specs=[pl.BlockSpec((1, TILE, D_MODEL), lambda b, t: (b, t, 0)),
                   per_b((WINDOW, KV_W)), per_b((WINDOW, KV_W)), per_b((B_CONV - 1, B_WIDTH))],
        out_shape=[jax.ShapeDtypeStruct((nb, seq, D_MODEL), F32),
                   jax.ShapeDtypeStruct((nb, WINDOW, KV_W), F32),
                   jax.ShapeDtypeStruct((nb, WINDOW, KV_W), F32),
                   jax.ShapeDtypeStruct((nb, B_CONV - 1, B_WIDTH), F32)],
        scratch_shapes=[pltpu.VMEM((WINDOW + TILE, KV_W), F32),
                        pltpu.VMEM((WINDOW + TILE, KV_W), F32), pltpu.VMEM((B_OFF + TILE, B_WIDTH), F32),
                        pltpu.VMEM((TILE, D_MODEL), F32)],
        compiler_params=_params(2),
        name="l0_main",
    )(*ins)


def _l1_short(x, rowmask, c0, s0, consts, weights):
    rows, ns = x.shape[0], c0.shape[0]
    ins = (x, rowmask, c0, s0) + consts + weights
    return pl.pallas_call(
        _l1_short_kernel,
        grid=(1,),
        in_specs=[_whole(a.shape) for a in ins],
        out_specs=[_whole_out((rows, D_MODEL)), _whole_out((ns, C_HIST, C_WIDTH)),
                   _whole_out((ns, D_HEADS, D_KEY, D_VAL))],
        out_shape=[jax.ShapeDtypeStruct((rows, D_MODEL), F32),
                   jax.ShapeDtypeStruct((ns, C_HIST, C_WIDTH), F32),
                   jax.ShapeDtypeStruct((ns, D_HEADS, D_KEY, D_VAL), F32)],
        scratch_shapes=[pltpu.VMEM((ns, C_OFF + CHUNK, C_WIDTH), F32), pltpu.VMEM((rows, D_MODEL), F32)],
        compiler_params=_params(1),
        name="l1_short",
    )(*ins)


def _l1_main(x, c0, s0, consts, weights, meta_idx):
    nb, seq, _ = x.shape
    nt = seq // TILE
    ins = (x, c0, s0) + consts + weights
    in_specs = [pl.BlockSpec((1, TILE, D_MODEL), lambda b, t: (b, t, 0)),
                pl.BlockSpec((1, C_HIST, C_WIDTH), lambda b, t: (meta_idx, 0, 0)),
                pl.BlockSpec((1, D_HEADS, D_KEY, D_VAL), lambda b, t: (meta_idx, 0, 0, 0)),
                ] + [_whole(a.shape) for a in consts + weights]
    return pl.pallas_call(
        _l1_main_kernel,
        grid=(nb, nt),
        in_specs=in_specs,
        out_specs=[pl.BlockSpec((1, TILE, D_MODEL), lambda b, t: (b, t, 0)),
                   pl.BlockSpec((1, C_HIST, C_WIDTH), lambda b, t: (b, 0, 0)),
                   pl.BlockSpec((1, D_HEADS, D_KEY, D_VAL), lambda b, t: (b, 0, 0, 0))],
        out_shape=[jax.ShapeDtypeStruct((nb, seq, D_MODEL), F32),
                   jax.ShapeDtypeStruct((nb, C_HIST, C_WIDTH), F32),
                   jax.ShapeDtypeStruct((nb, D_HEADS, D_KEY, D_VAL), F32)],
        scratch_shapes=[pltpu.VMEM((C_OFF + TILE, C_WIDTH), F32), pltpu.VMEM((D_HEADS, D_VAL, D_KEY), F32),
                        pltpu.VMEM((TILE, D_MODEL), F32)],
        compiler_params=_params(2),
        name="l1_main",
    )(*ins)


def _alibi_tables():
    slopes = np.exp2(-8.0 * np.arange(1, A_HEADS + 1, dtype=np.float32) / A_HEADS).astype(np.float32)
    i = np.arange(CHUNK, dtype=np.float32)[None, :]
    j = np.arange(BAND, dtype=np.float32)[:, None]
    dist = np.abs(WINDOW + i - j).astype(np.float32)
    bias = np.zeros((A_KV_HEADS, 2 * BAND, 2 * CHUNK), np.float32)
    heads = np.zeros((2 * A_KV_HEADS, 2 * CHUNK), np.int32)
    for g in range(A_KV_HEADS):
        for e in range(2):
            for half, head in enumerate((4 * g + e, 4 * g + 2 + e)):
                cols = slice(CHUNK * half, CHUNK * (half + 1))
                bias[g, BAND * e:BAND * (e + 1), cols] = -slopes[head] * dist
                heads[2 * g + e, cols] = head
    return bias, heads


def kernel(x_prompt, x_sample, cache_k_a, cache_v_a, state_conv_b, state_conv_c, state_hgrn, meta_tokens,
           ab_w_in, ab_b_in, a_sinks, b_conv_w, ab_w_o, cd_w_in, cd_b_in, c_conv_w, c_conv_b, c_ln_g,
           c_ln_b, d_lower_bounds, d_norm_g, cd_w_o, ln1_g, ln1_b, ln2_g, ln2_b, ffn_w_gu, ffn_w_down):
    nb = x_prompt.shape[0]
    ns = x_sample.shape[0]
    assert x_sample.shape[1] == CHUNK and cache_k_a.shape[1] == WINDOW
    assert x_prompt.shape[1] % TILE == 0
    meta_idx = ns
    n_short = ns + 1
    rows = n_short * CHUNK

    xs = jnp.concatenate([x_sample.reshape(ns * CHUNK, D_MODEL),
                          jnp.zeros((PADF, D_MODEL), F32), meta_tokens.astype(F32)], axis=0)
    rowmask = np.ones((rows, 1), np.float32)
    rowmask[ns * CHUNK:ns * CHUNK + PADF] = 0.0
    rowmask = jnp.asarray(rowmask)

    kvb_short = np.zeros((BAND, 2 * CHUNK), np.float32)
    kvb_short[:WINDOW + PADF] = NEG
    kvb_main = np.zeros((2, BAND, 2 * CHUNK), np.float32)
    kvb_main[0, :CHUNK + PADF] = NEG
    kvb_main[1, :PADF] = NEG

    alibi, sink_heads = _alibi_tables()
    sinks = a_sinks.astype(F32)[jnp.asarray(sink_heads)][:, None, :]
    a_consts = (jnp.asarray(alibi), sinks, b_conv_w.astype(F32))

    zpad = lambda a: jnp.concatenate([a, jnp.zeros((1,) + a.shape[1:], a.dtype)], axis=0)
    k0 = zpad(cache_k_a.reshape(ns, WINDOW, KV_W))
    v0 = zpad(cache_v_a.reshape(ns, WINDOW, KV_W))
    u0 = zpad(state_conv_b)
    c0 = zpad(state_conv_c)
    s0 = zpad(state_hgrn)

    def layer_weights(l, w_in, b_in, w_o):
        ln = jnp.stack([ln1_g[l], ln1_b[l], ln2_g[l], ln2_b[l]]).astype(F32)
        return (w_in.astype(BF16), b_in.astype(F32)[None, :], w_o.astype(BF16), ln,
                ffn_w_gu[l].astype(BF16), ffn_w_down[l].astype(BF16))

    w0 = layer_weights(0, ab_w_in, ab_b_in, ab_w_o)
    w1 = layer_weights(1, cd_w_in, cd_b_in, cd_w_o)

    hs0, k_s, v_s, u_s = _l0_short(xs, rowmask, k0, v0, u0, jnp.asarray(kvb_short), a_consts, w0)
    hp0, k_p, v_p, u_p = _l0_main(x_prompt, k_s, v_s, u_s, jnp.asarray(kvb_main), a_consts, w0, meta_idx)

    cvec = jnp.stack([c_conv_b, c_ln_g, c_ln_b]).astype(F32)
    c_consts = (c_conv_w.astype(F32), cvec, d_lower_bounds.astype(F32), d_norm_g.astype(F32)[None, :])
    hs1, c_s, s_s = _l1_short(hs0, rowmask, c0, s0, c_consts, w1)
    hp1, c_p, s_p = _l1_main(hp0, c_s, s_s, c_consts, w1, meta_idx)

    kv = lambda a: a.reshape(a.shape[0], WINDOW, A_KV_HEADS, HEAD_DIM)
    y_sample = hs1[:ns * CHUNK].reshape(ns, CHUNK, D_MODEL)
    return (hp1, y_sample, kv(k_p), kv(v_p), u_p, c_p, s_p,
            kv(k_s[:ns]), kv(v_s[:ns]), u_s[:ns], c_s[:ns], s_s[:ns])
```

```python
import functools

import jax
import jax.numpy as jnp
import numpy as np
from jax import lax
from jax.experimental import pallas as pl
from jax.experimental.pallas import tpu as pltpu

F32 = jnp.float32
BF16 = jnp.bfloat16

D_MODEL = 1024
CHUNK = 64
N_META = 16
PADF = CHUNK - N_META
A_HEADS = 8
A_KV_HEADS = 2
HEAD_DIM = 64
WINDOW = 128
KV_W = A_KV_HEADS * HEAD_DIM
Q_W = A_HEADS * HEAD_DIM
BAND = WINDOW + CHUNK
B_WIDTH = 512
B_CONV = 3
C_WIDTH = 512
C_CONV = 31
D_HEADS = 4
D_KEY = 128
D_VAL = 128
D_WIDTH = D_HEADS * D_VAL
SUB = 16
FFN_HIDDEN = 2816
DEPTH = 2
ALPHA = (2 * DEPTH) ** 0.25
LN_EPS = 1e-5
RMS_EPS = 1e-6
NEG = -1e30

AB_IN = Q_W + 2 * KV_W + 3 * B_WIDTH
CD_IN = 2 * C_WIDTH + 4 * D_WIDTH

TILE = 256
FFN_BLOCK = 256
L0_FFN_STEPS_BETWEEN = 4
L1_FFN_STEPS_BETWEEN = 2
SUBLANES = 8
CONV_ROWS = 32
C_HIST = C_CONV - 1
C_OFF = 32
B_OFF = 8
VMEM_LIMIT = 60 * 1024 * 1024


def _dot(a, b):
    return jnp.dot(a.astype(BF16), b.astype(BF16), preferred_element_type=F32)


def _dot_nt(a, b):
    return lax.dot_general(a.astype(BF16), b.astype(BF16), (((1,), (1,)), ((), ())),
                           preferred_element_type=F32)


def _dot_tn(a, b):
    return lax.dot_general(a.astype(BF16), b.astype(BF16), (((0,), (0,)), ((), ())),
                           preferred_element_type=F32)


def _sigmoid(x):
    return 1.0 / (1.0 + jnp.exp(-x))


def _silu(x):
    return x * _sigmoid(x)


def _layer_norm(x, g, b):
    mu = jnp.mean(x, axis=-1, keepdims=True)
    xc = x - mu
    var = jnp.mean(xc * xc, axis=-1, keepdims=True)
    return xc * lax.rsqrt(var + LN_EPS) * g + b


class _PostBlock:
    N_BLOCKS = FFN_HIDDEN // FFN_BLOCK
    N_STEPS = 2 + N_BLOCKS

    def __init__(self, x, mix, w_o_ref, ln_ref, wgu_ref, wd_ref):
        self.x, self.mix = x, mix
        self.w_o_ref, self.ln_ref, self.wgu_ref, self.wd_ref = w_o_ref, ln_ref, wgu_ref, wd_ref
        self.done = 0
        self.h = self.hb = self.acc = self.act = None

    def _gate_up(self, j):
        lo = j * FFN_BLOCK
        gate = jnp.dot(self.hb, self.wgu_ref[:, lo:lo + FFN_BLOCK], preferred_element_type=F32)
        up = jnp.dot(self.hb, self.wgu_ref[:, FFN_HIDDEN + lo:FFN_HIDDEN + lo + FFN_BLOCK],
                     preferred_element_type=F32)
        return (_silu(gate) * up).astype(BF16)

    def _step(self, j):
        if j == 0:
            ln = self.ln_ref
            self.h = _layer_norm(ALPHA * self.x + _dot(self.mix, self.w_o_ref[...]), ln[0:1], ln[1:2])
            self.hb = self.h.astype(BF16)
            return
        prev = self.act
        self.act = self._gate_up(j - 1) if j - 1 < self.N_BLOCKS else None
        if prev is not None:
            lo = (j - 2) * FFN_BLOCK
            part = jnp.dot(prev, self.wd_ref[lo:lo + FFN_BLOCK, :], preferred_element_type=F32)
            self.acc = part if self.acc is None else self.acc + part

    def run(self, n):
        for _ in range(n):
            if self.done < self.N_STEPS:
                self._step(self.done)
                self.done += 1

    def result(self):
        self.run(self.N_STEPS)
        return _layer_norm(ALPHA * self.h + self.acc, self.ln_ref[2:3], self.ln_ref[3:4])


def _no_op():
    pass


def _attn_chunks(chunks, alibi_ref, sink_ref, between=_no_op):
    lane = lax.broadcasted_iota(jnp.int32, (BAND, KV_W), 1)
    low = lane < HEAD_DIM
    high = jnp.logical_not(low)

    def extend(band, swapped, g, e):
        src = band if g == e else swapped
        return jnp.where(low if e == 0 else high, src, 0.0).astype(BF16)

    scores = []
    for q, kband, _, kvbias in chunks:
        kswap = pltpu.roll(kband, HEAD_DIM, axis=1)
        for g in range(A_KV_HEADS):
            qg = jnp.concatenate([q[:, 256 * g:256 * g + 128], q[:, 256 * g + 128:256 * g + 256]],
                                 axis=0).astype(BF16)
            k2 = jnp.concatenate([extend(kband, kswap, g, e) for e in range(2)], axis=0)
            s = _dot_nt(k2, qg) + alibi_ref[g]
            for e in range(2):
                se = s[BAND * e:BAND * (e + 1)]
                scores.append(se if kvbias is None else se + kvbias)

    between()
    probs = []
    for idx, s in enumerate(scores):
        sink = sink_ref[idx % (2 * A_KV_HEADS)]
        m = jnp.maximum(jnp.max(s, axis=0, keepdims=True), sink)
        p = jnp.exp(s - m)
        den = jnp.sum(p, axis=0, keepdims=True) + jnp.exp(sink - m)
        probs.append((p * (1.0 / den)).astype(BF16))

    outs = []
    for ci, (_, _, vband, _) in enumerate(chunks):
        vswap = pltpu.roll(vband, HEAD_DIM, axis=1)
        parts = []
        for g in range(A_KV_HEADS):
            base = ci * 2 * A_KV_HEADS + 2 * g
            p2 = jnp.concatenate([probs[base], probs[base + 1]], axis=0)
            v2 = jnp.concatenate([extend(vband, vswap, g, e) for e in range(2)], axis=0)
            o = _dot_tn(p2, v2)
            parts += [o[0:CHUNK], o[CHUNK:2 * CHUNK]]
        outs.append(jnp.concatenate(parts, axis=1))
    return outs


def _ab_project(x, w_in_ref, b_in_ref):
    proj = _dot(x, w_in_ref[...]) + b_in_ref[...]
    q = proj[:, 0:Q_W] * (HEAD_DIM ** -0.5)
    k = proj[:, Q_W:Q_W + KV_W]
    v = proj[:, Q_W + KV_W:Q_W + 2 * KV_W]
    o = Q_W + 2 * KV_W
    bg = proj[:, o:o + B_WIDTH]
    u = proj[:, o + B_WIDTH:o + 2 * B_WIDTH] * proj[:, o + 2 * B_WIDTH:o + 3 * B_WIDTH]
    return q, k, v, bg, u


def _short_conv(u_ref, base, rows, w_ref):
    out = None
    for j in range(B_CONV):
        term = u_ref[pl.ds(base - (B_CONV - 1) + j, rows), :] * w_ref[j:j + 1, :]
        out = term if out is None else out + term
    return out


def _l0_short_kernel(x_ref, rowmask_ref, k0_ref, v0_ref, u0_ref, kvb_ref, alibi_ref, sink_ref,
                     convw_ref, w_in_ref, b_in_ref, w_o_ref, ln_ref, wgu_ref, wd_ref,
                     h_ref, kout_ref, vout_ref, uout_ref,
                     u_s, mix_s):
    ns = k0_ref.shape[0]
    x = x_ref[...]
    q, k, v, bg, u = _ab_project(x, w_in_ref, b_in_ref)
    u = u * rowmask_ref[...]

    chunks = []
    for s in range(ns):
        r0 = s * CHUNK
        kband = jnp.concatenate([k0_ref[s], k[r0:r0 + CHUNK]], axis=0)
        vband = jnp.concatenate([v0_ref[s], v[r0:r0 + CHUNK]], axis=0)
        kout_ref[s] = kband[CHUNK:]
        vout_ref[s] = vband[CHUNK:]
        chunks.append((q[r0:r0 + CHUNK], kband, vband, kvb_ref[...] if s == ns - 1 else None))
    for s, att in enumerate(_attn_chunks(chunks, alibi_ref, sink_ref)):
        mix_s[s * CHUNK:(s + 1) * CHUNK, 0:Q_W] = att

    for s in range(ns):
        r0 = s * CHUNK
        us = u_s.at[s]
        us[B_OFF - 2:B_OFF, :] = u0_ref[s]
        us[B_OFF:B_OFF + CHUNK, :] = u[r0:r0 + CHUNK]
        cb = _short_conv(us, B_OFF, CHUNK, convw_ref)
        mix_s[r0:r0 + CHUNK, Q_W:] = bg[r0:r0 + CHUNK] * cb
        uout_ref[s] = u[r0 + CHUNK - 2:r0 + CHUNK]

    h_ref[...] = _PostBlock(x, mix_s[...], w_o_ref, ln_ref, wgu_ref, wd_ref).result()


def _tile_position(nt):
    i = pl.program_id(0)
    n = pl.num_programs(0) - 1
    return i, n, lax.rem(jnp.minimum(i, n - 1), nt)


def _l0_main_kernel(nt, x_ref, xprev_ref, k0_ref, v0_ref, u0_ref, kvb_ref, alibi_ref, sink_ref,
                    convw_ref, w_in_ref, b_in_ref, w_o_ref, ln_ref, wgu_ref, wd_ref,
                    h_ref, kout_ref, vout_ref, uout_ref,
                    k_s, v_s, u_s, mix_s):
    i, n, t = _tile_position(nt)
    tm = x_ref.shape[1]

    @pl.when(i == 0)
    def _():
        mix_s[...] = jnp.zeros_like(mix_s)

    @pl.when(t == 0)
    def _():
        k_s[0:WINDOW, :] = k0_ref[0]
        v_s[0:WINDOW, :] = v0_ref[0]
        u_s[B_OFF - 2:B_OFF, :] = u0_ref[0]

    post = _PostBlock(xprev_ref[0], mix_s[...], w_o_ref, ln_ref, wgu_ref, wd_ref)
    post.run(1)

    q, k, v, bg, u = _ab_project(x_ref[0], w_in_ref, b_in_ref)
    k_s[WINDOW:, :] = k
    v_s[WINDOW:, :] = v
    u_s[B_OFF:, :] = u
    mix_s[:, Q_W:] = bg * _short_conv(u_s, B_OFF, tm, convw_ref)

    first = t == 0
    chunks = []
    for c in range(tm // CHUNK):
        r0 = c * CHUNK
        kvbias = jnp.where(first, kvb_ref[c], 0.0) if c < 2 else None
        chunks.append((q[r0:r0 + CHUNK], k_s[r0:r0 + BAND, :], v_s[r0:r0 + BAND, :], kvbias))
    outs = _attn_chunks(chunks, alibi_ref, sink_ref, between=lambda: post.run(L0_FFN_STEPS_BETWEEN))
    for c, att in enumerate(outs):
        mix_s[c * CHUNK:(c + 1) * CHUNK, 0:Q_W] = att

    h_ref[0] = post.result()

    k_s[0:WINDOW, :] = k_s[tm:tm + WINDOW, :]
    v_s[0:WINDOW, :] = v_s[tm:tm + WINDOW, :]
    u_s[B_OFF - 2:B_OFF, :] = u_s[B_OFF + tm - 2:B_OFF + tm, :]

    @pl.when(jnp.logical_and(t == nt - 1, i < n))
    def _():
        kout_ref[0] = k_s[0:WINDOW, :]
        vout_ref[0] = v_s[0:WINDOW, :]
        uout_ref[0] = u_s[B_OFF - 2:B_OFF, :]


def _cd_project(x, rowmask, w_in_ref, b_in_ref, lb_ref):
    proj = _dot(x, w_in_ref[...]) + b_in_ref[...]
    w = C_WIDTH
    u = proj[:, 0:w] * _sigmoid(proj[:, w:2 * w])
    q = proj[:, 2 * w:3 * w]
    d0 = lb_ref[0:1]
    d1 = lb_ref[1:2]
    mx = jnp.maximum(d0, d1)
    e0 = jnp.exp(d0 - mx)
    e1 = jnp.exp(d1 - mx)
    p0 = e0 / (e0 + e1)
    p1 = e1 / (e0 + e1)
    lb = (p0 + p1) - p0
    forget = lb + (1.0 - lb) * _sigmoid(proj[:, 3 * w:4 * w])
    kk = 1.0 - forget
    lf = jnp.log(forget)
    v = proj[:, 4 * w:5 * w]
    g = proj[:, 5 * w:6 * w]
    if rowmask is not None:
        u = u * rowmask
        q = q * rowmask
        kk = kk * rowmask
        v = v * rowmask
        lf = lf * rowmask
    return u, q, kk, v, lf, g


def _conformer_conv(c_ref, base, rows, convw_ref, cvec_ref, out_ref, out_base):
    for r0 in range(0, rows, CONV_ROWS):
        win = c_ref[base - C_OFF + r0:base + r0 + CONV_ROWS, :]
        acc = None
        for phase in range(SUBLANES):
            shifted = win if phase == 0 else pltpu.roll(win, CONV_ROWS + C_OFF - phase, axis=0)
            for j in range(C_CONV):
                off = C_OFF - C_HIST + j
                if off % SUBLANES != phase:
                    continue
                lo = off - phase
                term = shifted[lo:lo + CONV_ROWS] * convw_ref[j:j + 1, :]
                acc = term if acc is None else acc + term
        y = _layer_norm(acc + cvec_ref[0:1], cvec_ref[1:2], cvec_ref[2:3])
        out_ref[out_base + r0:out_base + r0 + CONV_ROWS, 0:C_WIDTH] = _silu(y)


def _hgrn_chunks(chunks, init_states, between=_no_op):
    row = lax.broadcasted_iota(jnp.int32, (CHUNK, CHUNK), 0)
    col = lax.broadcasted_iota(jnp.int32, (CHUNK, CHUNK), 1)
    tril = col <= row
    ltri = jnp.where(tril, 1.0, 0.0).astype(BF16)
    nsub = CHUNK // SUB
    rsub = jnp.right_shift(lax.broadcasted_iota(jnp.int32, (CHUNK, D_KEY), 0), SUB.bit_length() - 1)
    heads = [slice(D_KEY * h, D_KEY * (h + 1)) for h in range(D_HEADS)]

    cums = []
    for _, _, _, lf in chunks:
        hi = lf.astype(BF16)
        r1 = lf - hi.astype(F32)
        mid = r1.astype(BF16)
        lo = (r1 - mid.astype(F32)).astype(BF16)
        cums.append(jnp.dot(ltri, hi, preferred_element_type=F32)
                    + jnp.dot(ltri, mid, preferred_element_type=F32)
                    + jnp.dot(ltri, lo, preferred_element_type=F32))

    between()
    pre = []
    for (q, kk, v, _), cum in zip(chunks, cums):
        tot = cum[CHUNK - 1:CHUNK]
        bases = [jnp.zeros((1, D_WIDTH), F32)] + [cum[SUB * i - 1:SUB * i] for i in range(1, nsub)]
        bsel = jnp.concatenate([jnp.broadcast_to(b, (SUB, D_WIDTH)) for b in bases], axis=0)
        qe = (q * jnp.exp(cum)).astype(BF16)
        qd = q * jnp.exp(cum - bsel)
        kdec = (kk * jnp.exp(tot - cum)).astype(BF16)
        qblks, kdcs = [], []
        for sl in heads:
            qd_h = qd[:, sl]
            qblks.append(jnp.concatenate([jnp.where(rsub == i, qd_h, 0.0) for i in range(nsub)],
                                         axis=1).astype(BF16))
            kds = []
            for i in range(nsub):
                n = SUB * (i + 1)
                kd = kk[0:n, sl] * jnp.exp(bases[i][:, sl] - cum[0:n, sl])
                if n < CHUNK:
                    kd = jnp.concatenate([kd, jnp.zeros((CHUNK - n, D_KEY), F32)], axis=0)
                kds.append(kd)
            kdcs.append(jnp.concatenate(kds, axis=1).astype(BF16))
        pre.append((qe, kdec, v.astype(BF16), jnp.exp(tot), qblks, kdcs))

    atts, incs = [], []
    for qe, kdec, vb, etot, qblks, kdcs in pre:
        atts.append([jnp.where(tril, _dot_nt(qblks[h], kdcs[h]), 0.0).astype(BF16) for h in range(D_HEADS)])
        incs.append([_dot_tn(vb[:, sl], kdec[:, sl]) for sl in heads])

    between()
    intra = [[_dot(atts[c][h], pre[c][2][:, heads[h]]) for h in range(D_HEADS)] for c in range(len(chunks))]

    between()
    outs, ends = [], []
    state = None
    for c, (qe, kdec, vb, etot, _, _) in enumerate(pre):
        if init_states[c] is not None:
            state = init_states[c]
        o = [_dot_nt(qe[:, heads[h]], state[h]) + intra[c][h] for h in range(D_HEADS)]
        state = [etot[:, heads[h]] * state[h] + incs[c][h] for h in range(D_HEADS)]
        outs.append(jnp.concatenate(o, axis=1))
        ends.append(state)
    return outs, ends


def _hgrn_readout(o, g, norm_g):
    parts = []
    for h in range(D_HEADS):
        oh = o[:, D_VAL * h:D_VAL * (h + 1)]
        parts.append(oh * lax.rsqrt(jnp.mean(oh * oh, axis=-1, keepdims=True) + RMS_EPS))
    return jnp.concatenate(parts, axis=1) * norm_g * _silu(g)


def _l1_short_kernel(x_ref, rowmask_ref, c0_ref, s0_ref, convw_ref, cvec_ref, lb_ref, normg_ref,
                     w_in_ref, b_in_ref, w_o_ref, ln_ref, wgu_ref, wd_ref,
                     h_ref, cout_ref, sout_ref,
                     c_s, mix_s):
    ns = c0_ref.shape[0]
    x = x_ref[...]
    u, q, kk, v, lf, g = _cd_project(x, rowmask_ref[...], w_in_ref, b_in_ref, lb_ref)

    for s in range(ns):
        r0 = s * CHUNK
        cs = c_s.at[s]
        cs[C_OFF - C_HIST:C_OFF, :] = c0_ref[s]
        cs[C_OFF:C_OFF + CHUNK, :] = u[r0:r0 + CHUNK]
        _conformer_conv(cs, C_OFF, CHUNK, convw_ref, cvec_ref, mix_s, r0)
        cout_ref[s] = cs[C_OFF + CHUNK - C_HIST:C_OFF + CHUNK, :]

    rows = [slice(s * CHUNK, (s + 1) * CHUNK) for s in range(ns)]
    outs, ends = _hgrn_chunks([(q[r], kk[r], v[r], lf[r]) for r in rows],
                              [[s0_ref[s, h].T for h in range(D_HEADS)] for s in range(ns)])
    for s in range(ns):
        mix_s[rows[s], C_WIDTH:] = _hgrn_readout(outs[s], g[rows[s]], normg_ref[...])
        for h in range(D_HEADS):
            sout_ref[s, h] = ends[s][h].T

    h_ref[...] = _PostBlock(x, mix_s[...], w_o_ref, ln_ref, wgu_ref, wd_ref).result()


def _l1_main_kernel(nt, x_ref, xprev_ref, c0_ref, s0_ref, convw_ref, cvec_ref, lb_ref, normg_ref,
                    w_in_ref, b_in_ref, w_o_ref, ln_ref, wgu_ref, wd_ref,
                    h_ref, cout_ref, sout_ref,
                    c_s, st_s, mix_s):
    i, n, t = _tile_position(nt)
    tm = x_ref.shape[1]

    @pl.when(i == 0)
    def _():
        mix_s[...] = jnp.zeros_like(mix_s)

    @pl.when(t == 0)
    def _():
        c_s[C_OFF - C_HIST:C_OFF, :] = c0_ref[0]
        for h in range(D_HEADS):
            st_s[h] = s0_ref[0, h].T

    post = _PostBlock(xprev_ref[0], mix_s[...], w_o_ref, ln_ref, wgu_ref, wd_ref)
    post.run(1)

    u, q, kk, v, lf, g = _cd_project(x_ref[0], None, w_in_ref, b_in_ref, lb_ref)

    nc = tm // CHUNK
    rows = [slice(c * CHUNK, (c + 1) * CHUNK) for c in range(nc)]
    outs, ends = _hgrn_chunks([(q[r], kk[r], v[r], lf[r]) for r in rows],
                              [[st_s[h] for h in range(D_HEADS)]] + [None] * (nc - 1),
                              between=lambda: post.run(L1_FFN_STEPS_BETWEEN))
    for c in range(nc):
        mix_s[rows[c], C_WIDTH:] = _hgrn_readout(outs[c], g[rows[c]], normg_ref[...])
    for h in range(D_HEADS):
        st_s[h] = ends[-1][h]

    c_s[C_OFF:, :] = u
    _conformer_conv(c_s, C_OFF, tm, convw_ref, cvec_ref, mix_s, 0)
    c_s[C_OFF - C_HIST:C_OFF, :] = c_s[C_OFF + tm - C_HIST:C_OFF + tm, :]

    h_ref[0] = post.result()

    @pl.when(jnp.logical_and(t == nt - 1, i < n))
    def _():
        cout_ref[0] = c_s[C_OFF - C_HIST:C_OFF, :]
        for h in range(D_HEADS):
            sout_ref[0, h] = st_s[h].T


def _whole(shape):
    zeros = (0,) * len(shape)
    return pl.BlockSpec(shape, lambda *_: zeros, pipeline_mode=pl.Buffered(1))


def _whole_out(shape):
    zeros = (0,) * len(shape)
    return pl.BlockSpec(shape, lambda *_: zeros)


def _params(n_axes):
    return pltpu.CompilerParams(dimension_semantics=("arbitrary",) * n_axes,
                                vmem_limit_bytes=VMEM_LIMIT)


def _l0_short(x, rowmask, k0, v0, u0, kvb, consts, weights):
    rows, ns = x.shape[0], k0.shape[0]
    ins = (x, rowmask, k0, v0, u0, kvb) + consts + weights
    return pl.pallas_call(
        _l0_short_kernel,
        grid=(1,),
        in_specs=[_whole(a.shape) for a in ins],
        out_specs=[_whole_out((rows, D_MODEL)), _whole_out((ns, WINDOW, KV_W)),
                   _whole_out((ns, WINDOW, KV_W)), _whole_out((ns, B_CONV - 1, B_WIDTH))],
        out_shape=[jax.ShapeDtypeStruct((rows, D_MODEL), F32),
                   jax.ShapeDtypeStruct((ns, WINDOW, KV_W), F32),
                   jax.ShapeDtypeStruct((ns, WINDOW, KV_W), F32),
                   jax.ShapeDtypeStruct((ns, B_CONV - 1, B_WIDTH), F32)],
        scratch_shapes=[pltpu.VMEM((ns, B_OFF + CHUNK, B_WIDTH), F32), pltpu.VMEM((rows, D_MODEL), F32)],
        compiler_params=_params(1),
        name="l0_short",
    )(*ins)


class _SkewedTiles:
    def __init__(self, nb, nt):
        self.nt, self.n = nt, nb * nt

    def _cur(self, i):
        return jnp.minimum(i, self.n - 1)

    def tile(self):
        return pl.BlockSpec((1, TILE, D_MODEL), lambda i: (self._cur(i) // self.nt, self._cur(i) % self.nt, 0))

    def prev_tile(self):
        prev = lambda i: jnp.maximum(i - 1, 0)
        return pl.BlockSpec((1, TILE, D_MODEL), lambda i: (prev(i) // self.nt, prev(i) % self.nt, 0))

    def fixed(self, shape, idx):
        zeros = (0,) * len(shape)
        return pl.BlockSpec((1,) + shape, lambda i: (idx,) + zeros)

    def per_prompt(self, shape):
        zeros = (0,) * len(shape)
        return pl.BlockSpec((1,) + shape, lambda i: (self._cur(i) // self.nt,) + zeros)


def _l0_main(x, k0, v0, u0, kvb, consts, weights, meta_idx):
    nb, seq, _ = x.shape
    nt = seq // TILE
    sk = _SkewedTiles(nb, nt)
    ins = (x, x, k0, v0, u0, kvb) + consts + weights
    in_specs = [sk.tile(), sk.prev_tile(),
                sk.fixed((WINDOW, KV_W), meta_idx), sk.fixed((WINDOW, KV_W), meta_idx),
                sk.fixed((B_CONV - 1, B_WIDTH), meta_idx),
                _whole(kvb.shape)] + [_whole(a.shape) for a in consts + weights]
    return pl.pallas_call(
        functools.partial(_l0_main_kernel, nt),
        grid=(nb * nt + 1,),
        in_specs=in_specs,
        out_specs=[sk.prev_tile(), sk.per_prompt((WINDOW, KV_W)), sk.per_prompt((WINDOW, KV_W)),
                   sk.per_prompt((B_CONV - 1, B_WIDTH))],
        out_shape=[jax.ShapeDtypeStruct((nb, seq, D_MODEL), F32),
                   jax.ShapeDtypeStruct((nb, WINDOW, KV_W), F32),
                   jax.ShapeDtypeStruct((nb, WINDOW, KV_W), F32),
                   jax.ShapeDtypeStruct((nb, B_CONV - 1, B_WIDTH), F32)],
        scratch_shapes=[pltpu.VMEM((WINDOW + TILE, KV_W), F32),
                        pltpu.VMEM((WINDOW + TILE, KV_W), F32), pltpu.VMEM((B_OFF + TILE, B_WIDTH), F32),
                        pltpu.VMEM((TILE, D_MODEL), F32)],
        compiler_params=_params(1),
        name="l0_main",
    )(*ins)


def _l1_short(x, rowmask, c0, s0, consts, weights):
    rows, ns = x.shape[0], c0.shape[0]
    ins = (x, rowmask, c0, s0) + consts + weights
    return pl.pallas_call(
        _l1_short_kernel,
        grid=(1,),
        in_specs=[_whole(a.shape) for a in ins],
        out_specs=[_whole_out((rows, D_MODEL)), _whole_out((ns, C_HIST, C_WIDTH)),
                   _whole_out((ns, D_HEADS, D_KEY, D_VAL))],
        out_shape=[jax.ShapeDtypeStruct((rows, D_MODEL), F32),
                   jax.ShapeDtypeStruct((ns, C_HIST, C_WIDTH), F32),
                   jax.ShapeDtypeStruct((ns, D_HEADS, D_KEY, D_VAL), F32)],
        scratch_shapes=[pltpu.VMEM((ns, C_OFF + CHUNK, C_WIDTH), F32), pltpu.VMEM((rows, D_MODEL), F32)],
        compiler_params=_params(1),
        name="l1_short",
    )(*ins)


def _l1_main(x, c0, s0, consts, weights, meta_idx):
    nb, seq, _ = x.shape
    nt = seq // TILE
    sk = _SkewedTiles(nb, nt)
    ins = (x, x, c0, s0) + consts + weights
    in_specs = [sk.tile(), sk.prev_tile(), sk.fixed((C_HIST, C_WIDTH), meta_idx),
                sk.fixed((D_HEADS, D_KEY, D_VAL), meta_idx),
                ] + [_whole(a.shape) for a in consts + weights]
    return pl.pallas_call(
        functools.partial(_l1_main_kernel, nt),
        grid=(nb * nt + 1,),
        in_specs=in_specs,
        out_specs=[sk.prev_tile(), sk.per_prompt((C_HIST, C_WIDTH)),
                   sk.per_prompt((D_HEADS, D_KEY, D_VAL))],
        out_shape=[jax.ShapeDtypeStruct((nb, seq, D_MODEL), F32),
                   jax.ShapeDtypeStruct((nb, C_HIST, C_WIDTH), F32),
                   jax.ShapeDtypeStruct((nb, D_HEADS, D_KEY, D_VAL), F32)],
        scratch_shapes=[pltpu.VMEM((C_OFF + TILE, C_WIDTH), F32), pltpu.VMEM((D_HEADS, D_VAL, D_KEY), F32),
                        pltpu.VMEM((TILE, D_MODEL), F32)],
        compiler_params=_params(1),
        name="l1_main",
    )(*ins)


def _alibi_tables():
    slopes = np.exp2(-8.0 * np.arange(1, A_HEADS + 1, dtype=np.float32) / A_HEADS).astype(np.float32)
    i = np.arange(CHUNK, dtype=np.float32)[None, :]
    j = np.arange(BAND, dtype=np.float32)[:, None]
    dist = np.abs(WINDOW + i - j).astype(np.float32)
    bias = np.zeros((A_KV_HEADS, 2 * BAND, 2 * CHUNK), np.float32)
    heads = np.zeros((2 * A_KV_HEADS, 2 * CHUNK), np.int32)
    for g in range(A_KV_HEADS):
        for e in range(2):
            for half, head in enumerate((4 * g + e, 4 * g + 2 + e)):
                cols = slice(CHUNK * half, CHUNK * (half + 1))
                bias[g, BAND * e:BAND * (e + 1), cols] = -slopes[head] * dist
                heads[2 * g + e, cols] = head
    return bias, heads


def kernel(x_prompt, x_sample, cache_k_a, cache_v_a, state_conv_b, state_conv_c, state_hgrn, meta_tokens,
           ab_w_in, ab_b_in, a_sinks, b_conv_w, ab_w_o, cd_w_in, cd_b_in, c_conv_w, c_conv_b, c_ln_g,
           c_ln_b, d_lower_bounds, d_norm_g, cd_w_o, ln1_g, ln1_b, ln2_g, ln2_b, ffn_w_gu, ffn_w_down):
    nb = x_prompt.shape[0]
    ns = x_sample.shape[0]
    assert x_sample.shape[1] == CHUNK and cache_k_a.shape[1] == WINDOW
    assert x_prompt.shape[1] % TILE == 0
    meta_idx = ns
    n_short = ns + 1
    rows = n_short * CHUNK

    xs = jnp.concatenate([x_sample.reshape(ns * CHUNK, D_MODEL),
                          jnp.zeros((PADF, D_MODEL), F32), meta_tokens.astype(F32)], axis=0)
    rowmask = np.ones((rows, 1), np.float32)
    rowmask[ns * CHUNK:ns * CHUNK + PADF] = 0.0
    rowmask = jnp.asarray(rowmask)

    kvb_short = np.zeros((BAND, 2 * CHUNK), np.float32)
    kvb_short[:WINDOW + PADF] = NEG
    kvb_main = np.zeros((2, BAND, 2 * CHUNK), np.float32)
    kvb_main[0, :CHUNK + PADF] = NEG
    kvb_main[1, :PADF] = NEG

    alibi, sink_heads = _alibi_tables()
    sinks = a_sinks.astype(F32)[jnp.asarray(sink_heads)][:, None, :]
    a_consts = (jnp.asarray(alibi), sinks, b_conv_w.astype(F32))

    zpad = lambda a: jnp.concatenate([a, jnp.zeros((1,) + a.shape[1:], a.dtype)], axis=0)
    k0 = zpad(cache_k_a.reshape(ns, WINDOW, KV_W))
    v0 = zpad(cache_v_a.reshape(ns, WINDOW, KV_W))
    u0 = zpad(state_conv_b)
    c0 = zpad(state_conv_c)
    s0 = zpad(state_hgrn)

    def layer_weights(l, w_in, b_in, w_o):
        ln = jnp.stack([ln1_g[l], ln1_b[l], ln2_g[l], ln2_b[l]]).astype(F32)
        return (w_in.astype(BF16), b_in.astype(F32)[None, :], w_o.astype(BF16), ln,
                ffn_w_gu[l].astype(BF16), ffn_w_down[l].astype(BF16))

    w0 = layer_weights(0, ab_w_in, ab_b_in, ab_w_o)
    w1 = layer_weights(1, cd_w_in, cd_b_in, cd_w_o)

    hs0, k_s, v_s, u_s = _l0_short(xs, rowmask, k0, v0, u0, jnp.asarray(kvb_short), a_consts, w0)
    hp0, k_p, v_p, u_p = _l0_main(x_prompt, k_s, v_s, u_s, jnp.asarray(kvb_main), a_consts, w0, meta_idx)

    cvec = jnp.stack([c_conv_b, c_ln_g, c_ln_b]).astype(F32)
    c_consts = (c_conv_w.astype(F32), cvec, d_lower_bounds.astype(F32), d_norm_g.astype(F32)[None, :])
    hs1, c_s, s_s = _l1_short(hs0, rowmask, c0, s0, c_consts, w1)
    hp1, c_p, s_p = _l1_main(hp0, c_s, s_s, c_consts, w1, meta_idx)

    kv = lambda a: a.reshape(a.shape[0], WINDOW, A_KV_HEADS, HEAD_DIM)
    y_sample = hs1[:ns * CHUNK].reshape(ns, CHUNK, D_MODEL)
    return (hp1, y_sample, kv(k_p), kv(v_p), u_p, c_p, s_p,
            kv(k_s[:ns]), kv(v_s[:ns]), u_s[:ns], c_s[:ns], s_s[:ns])
```

```python
import functools

import jax
import jax.numpy as jnp
import numpy as np
from jax import lax
from jax.experimental import pallas as pl
from jax.experimental.pallas import tpu as pltpu

F32 = jnp.float32
BF16 = jnp.bfloat16

D_MODEL = 1024
CHUNK = 64
N_META = 16
PADF = CHUNK - N_META
A_HEADS = 8
A_KV_HEADS = 2
HEAD_DIM = 64
WINDOW = 128
KV_W = A_KV_HEADS * HEAD_DIM
Q_W = A_HEADS * HEAD_DIM
BAND = WINDOW + CHUNK
B_WIDTH = 512
B_CONV = 3
C_WIDTH = 512
C_CONV = 31
D_HEADS = 4
D_KEY = 128
D_VAL = 128
D_WIDTH = D_HEADS * D_VAL
SUB = 16
FFN_HIDDEN = 2816
DEPTH = 2
ALPHA = (2 * DEPTH) ** 0.25
LN_EPS = 1e-5
RMS_EPS = 1e-6
NEG = -1e30

AB_IN = Q_W + 2 * KV_W + 3 * B_WIDTH
CD_IN = 2 * C_WIDTH + 4 * D_WIDTH

TILE = 256
FFN_BLOCK = 256
L0_FFN_STEPS_FIRST = 0
L0_FFN_STEPS_BETWEEN = (1,) * 8
L1_FFN_STEPS = (6, 2, 2, 2)
SUBLANES = 8
LANES = 128
CONV_ROWS = 64
C_HIST = C_CONV - 1
C_OFF = 32
B_OFF = 8
VMEM_LIMIT = 60 * 1024 * 1024


def _dot(a, b):
    return jnp.dot(a.astype(BF16), b.astype(BF16), preferred_element_type=F32)


def _dot_nt(a, b):
    return lax.dot_general(a.astype(BF16), b.astype(BF16), (((1,), (1,)), ((), ())),
                           preferred_element_type=F32)


def _dot_tn(a, b):
    return lax.dot_general(a.astype(BF16), b.astype(BF16), (((0,), (0,)), ((), ())),
                           preferred_element_type=F32)


def _sigmoid(x):
    return 1.0 / (1.0 + jnp.exp(-x))


def _silu(x):
    return x * _sigmoid(x)


def _layer_norm(x, g, b):
    mu = jnp.mean(x, axis=-1, keepdims=True)
    xc = x - mu
    var = jnp.mean(xc * xc, axis=-1, keepdims=True)
    return xc * lax.rsqrt(var + LN_EPS) * g + b


class _PostBlock:
    N_BLOCKS = FFN_HIDDEN // FFN_BLOCK
    N_STEPS = 2 + N_BLOCKS

    def __init__(self, x, mix, w_o_ref, ln_ref, wgu_ref, wd_ref):
        self.x, self.mix = x, mix
        self.w_o_ref, self.ln_ref, self.wgu_ref, self.wd_ref = w_o_ref, ln_ref, wgu_ref, wd_ref
        self.done = 0
        self.h = self.hb = self.acc = self.act = None

    def _gate_up(self, j):
        lo = j * FFN_BLOCK
        gate = jnp.dot(self.hb, self.wgu_ref[:, lo:lo + FFN_BLOCK], preferred_element_type=F32)
        up = jnp.dot(self.hb, self.wgu_ref[:, FFN_HIDDEN + lo:FFN_HIDDEN + lo + FFN_BLOCK],
                     preferred_element_type=F32)
        return (_silu(gate) * up).astype(BF16)

    def _step(self, j):
        if j == 0:
            ln = self.ln_ref
            self.h = _layer_norm(ALPHA * self.x + _dot(self.mix, self.w_o_ref[...]), ln[0:1], ln[1:2])
            self.hb = self.h.astype(BF16)
            return
        prev = self.act
        self.act = self._gate_up(j - 1) if j - 1 < self.N_BLOCKS else None
        if prev is not None:
            lo = (j - 2) * FFN_BLOCK
            part = jnp.dot(prev, self.wd_ref[lo:lo + FFN_BLOCK, :], preferred_element_type=F32)
            self.acc = part if self.acc is None else self.acc + part

    def run(self, n):
        for _ in range(n):
            if self.done < self.N_STEPS:
                self._step(self.done)
                self.done += 1

    def result(self):
        self.run(self.N_STEPS)
        return _layer_norm(ALPHA * self.h + self.acc, self.ln_ref[2:3], self.ln_ref[3:4])


def _no_op():
    pass


def _attn_chunks(chunks, alibi_ref, sink_ref, between=_no_op):
    lane = lax.broadcasted_iota(jnp.int32, (BAND, KV_W), 1)
    low = lane < HEAD_DIM
    high = jnp.logical_not(low)

    def extend(band, swapped, g, e):
        src = band if g == e else swapped
        return jnp.where(low if e == 0 else high, src, 0.0).astype(BF16)

    scores = []
    for q, kband, _, kvbias in chunks:
        kswap = pltpu.roll(kband, HEAD_DIM, axis=1)
        for g in range(A_KV_HEADS):
            qg = jnp.concatenate([q[:, 256 * g:256 * g + 128], q[:, 256 * g + 128:256 * g + 256]],
                                 axis=0).astype(BF16)
            k2 = jnp.concatenate([extend(kband, kswap, g, e) for e in range(2)], axis=0)
            s = _dot_nt(k2, qg) + alibi_ref[g]
            for e in range(2):
                se = s[BAND * e:BAND * (e + 1)]
                scores.append(se if kvbias is None else se + kvbias)
        between()

    probs = []
    for idx, s in enumerate(scores):
        sink = sink_ref[idx % (2 * A_KV_HEADS)]
        m = jnp.maximum(jnp.max(s, axis=0, keepdims=True), sink)
        p = jnp.exp(s - m)
        den = jnp.sum(p, axis=0, keepdims=True) + jnp.exp(sink - m)
        probs.append((p * (1.0 / den)).astype(BF16))
        if idx % (2 * A_KV_HEADS) == 2 * A_KV_HEADS - 1:
            between()

    outs = []
    for ci, (_, _, vband, _) in enumerate(chunks):
        vswap = pltpu.roll(vband, HEAD_DIM, axis=1)
        parts = []
        for g in range(A_KV_HEADS):
            base = ci * 2 * A_KV_HEADS + 2 * g
            p2 = jnp.concatenate([probs[base], probs[base + 1]], axis=0)
            v2 = jnp.concatenate([extend(vband, vswap, g, e) for e in range(2)], axis=0)
            o = _dot_tn(p2, v2)
            parts += [o[0:CHUNK], o[CHUNK:2 * CHUNK]]
        outs.append(jnp.concatenate(parts, axis=1))
    return outs


def _ab_project(x, w_in_ref, b_in_ref):
    proj = _dot(x, w_in_ref[...]) + b_in_ref[...]
    q = proj[:, 0:Q_W] * (HEAD_DIM ** -0.5)
    k = proj[:, Q_W:Q_W + KV_W]
    v = proj[:, Q_W + KV_W:Q_W + 2 * KV_W]
    o = Q_W + 2 * KV_W
    bg = proj[:, o:o + B_WIDTH]
    u = proj[:, o + B_WIDTH:o + 2 * B_WIDTH] * proj[:, o + 2 * B_WIDTH:o + 3 * B_WIDTH]
    return q, k, v, bg, u


def _short_conv(u_ref, base, rows, w_ref):
    out = None
    for j in range(B_CONV):
        term = u_ref[pl.ds(base - (B_CONV - 1) + j, rows), :] * w_ref[j:j + 1, :]
        out = term if out is None else out + term
    return out


def _l0_short_kernel(x_ref, rowmask_ref, k0_ref, v0_ref, u0_ref, kvb_ref, alibi_ref, sink_ref,
                     convw_ref, w_in_ref, b_in_ref, w_o_ref, ln_ref, wgu_ref, wd_ref,
                     h_ref, kout_ref, vout_ref, uout_ref,
                     u_s, mix_s):
    ns = k0_ref.shape[0]
    x = x_ref[...]
    q, k, v, bg, u = _ab_project(x, w_in_ref, b_in_ref)
    u = u * rowmask_ref[...]

    chunks = []
    for s in range(ns):
        r0 = s * CHUNK
        kband = jnp.concatenate([k0_ref[s], k[r0:r0 + CHUNK]], axis=0)
        vband = jnp.concatenate([v0_ref[s], v[r0:r0 + CHUNK]], axis=0)
        kout_ref[s] = kband[CHUNK:]
        vout_ref[s] = vband[CHUNK:]
        chunks.append((q[r0:r0 + CHUNK], kband, vband, kvb_ref[...] if s == ns - 1 else None))
    for s, att in enumerate(_attn_chunks(chunks, alibi_ref, sink_ref)):
        mix_s[s * CHUNK:(s + 1) * CHUNK, 0:Q_W] = att

    for s in range(ns):
        r0 = s * CHUNK
        us = u_s.at[s]
        us[B_OFF - 2:B_OFF, :] = u0_ref[s]
        us[B_OFF:B_OFF + CHUNK, :] = u[r0:r0 + CHUNK]
        cb = _short_conv(us, B_OFF, CHUNK, convw_ref)
        mix_s[r0:r0 + CHUNK, Q_W:] = bg[r0:r0 + CHUNK] * cb
        uout_ref[s] = u[r0 + CHUNK - 2:r0 + CHUNK]

    h_ref[...] = _PostBlock(x, mix_s[...], w_o_ref, ln_ref, wgu_ref, wd_ref).result()


def _tile_position(nt):
    i = pl.program_id(0)
    n = pl.num_programs(0) - 1
    return i, n, lax.rem(jnp.minimum(i, n - 1), nt)


def _l0_main_kernel(nt, x_ref, xprev_ref, k0_ref, v0_ref, u0_ref, kvb_ref, alibi_ref, sink_ref,
                    convw_ref, w_in_ref, b_in_ref, w_o_ref, ln_ref, wgu_ref, wd_ref,
                    h_ref, kout_ref, vout_ref, uout_ref,
                    k_s, v_s, u_s, mix_s):
    i, n, t = _tile_position(nt)
    tm = x_ref.shape[1]

    @pl.when(i == 0)
    def _():
        mix_s[...] = jnp.zeros_like(mix_s)

    @pl.when(t == 0)
    def _():
        k_s[0:WINDOW, :] = k0_ref[0]
        v_s[0:WINDOW, :] = v0_ref[0]
        u_s[B_OFF - 2:B_OFF, :] = u0_ref[0]

    post = _PostBlock(xprev_ref[0], mix_s[...], w_o_ref, ln_ref, wgu_ref, wd_ref)
    post.run(1)

    q, k, v, bg, u = _ab_project(x_ref[0], w_in_ref, b_in_ref)
    post.run(L0_FFN_STEPS_FIRST)
    k_s[WINDOW:, :] = k
    v_s[WINDOW:, :] = v
    u_s[B_OFF:, :] = u
    mix_s[:, Q_W:] = bg * _short_conv(u_s, B_OFF, tm, convw_ref)

    first = t == 0
    chunks = []
    for c in range(tm // CHUNK):
        r0 = c * CHUNK
        kvbias = jnp.where(first, kvb_ref[c], 0.0) if c < 2 else None
        chunks.append((q[r0:r0 + CHUNK], k_s[r0:r0 + BAND, :], v_s[r0:r0 + BAND, :], kvbias))
    plan = iter(L0_FFN_STEPS_BETWEEN)
    outs = _attn_chunks(chunks, alibi_ref, sink_ref, between=lambda: post.run(next(plan, 0)))
    for c, att in enumerate(outs):
        mix_s[c * CHUNK:(c + 1) * CHUNK, 0:Q_W] = att

    h_ref[0] = post.result()

    k_s[0:WINDOW, :] = k_s[tm:tm + WINDOW, :]
    v_s[0:WINDOW, :] = v_s[tm:tm + WINDOW, :]
    u_s[B_OFF - 2:B_OFF, :] = u_s[B_OFF + tm - 2:B_OFF + tm, :]

    @pl.when(jnp.logical_and(t == nt - 1, i < n))
    def _():
        kout_ref[0] = k_s[0:WINDOW, :]
        vout_ref[0] = v_s[0:WINDOW, :]
        uout_ref[0] = u_s[B_OFF - 2:B_OFF, :]


def _c_project(x, rowmask, w_in_ref, b_in_ref):
    w = C_WIDTH
    proj = _dot(x, w_in_ref[:, 0:2 * w]) + b_in_ref[:, 0:2 * w]
    u = proj[:, 0:w] * _sigmoid(proj[:, w:2 * w])
    return u if rowmask is None else u * rowmask


def _d_project(x, rowmask, w_in_ref, b_in_ref, lb_ref):
    w = D_WIDTH
    proj = _dot(x, w_in_ref[:, 2 * C_WIDTH:]) + b_in_ref[:, 2 * C_WIDTH:]
    q = proj[:, 0:w]
    d0 = lb_ref[0:1]
    d1 = lb_ref[1:2]
    mx = jnp.maximum(d0, d1)
    e0 = jnp.exp(d0 - mx)
    e1 = jnp.exp(d1 - mx)
    p0 = e0 / (e0 + e1)
    p1 = e1 / (e0 + e1)
    lb = (p0 + p1) - p0
    forget = lb + (1.0 - lb) * _sigmoid(proj[:, w:2 * w])
    kk = 1.0 - forget
    lf = jnp.log(forget)
    v = proj[:, 2 * w:3 * w]
    g = proj[:, 3 * w:4 * w]
    if rowmask is not None:
        q = q * rowmask
        kk = kk * rowmask
        v = v * rowmask
        lf = lf * rowmask
    return q, kk, v, lf, g


def _conformer_conv(c_ref, base, rows, convw_ref, cvec_ref, out_ref, out_base, between=_no_op):
    for r0 in range(0, rows, CONV_ROWS):
        cols = []
        for c0 in range(0, C_WIDTH, LANES):
            lanes = slice(c0, c0 + LANES)
            win = c_ref[base - C_OFF + r0:base + r0 + CONV_ROWS, lanes]
            acc = None
            for phase in range(SUBLANES):
                shifted = win if phase == 0 else pltpu.roll(win, CONV_ROWS + C_OFF - phase, axis=0)
                for j in range(C_CONV):
                    off = C_OFF - C_HIST + j
                    if off % SUBLANES != phase:
                        continue
                    lo = off - phase
                    term = shifted[lo:lo + CONV_ROWS] * convw_ref[j:j + 1, lanes]
                    acc = term if acc is None else acc + term
            cols.append(acc)
            between()
        acc = jnp.concatenate(cols, axis=1)
        y = _layer_norm(acc + cvec_ref[0:1], cvec_ref[1:2], cvec_ref[2:3])
        out_ref[out_base + r0:out_base + r0 + CONV_ROWS, 0:C_WIDTH] = _silu(y)


def _hgrn_chunks(chunks, init_states, between=_no_op):
    row = lax.broadcasted_iota(jnp.int32, (CHUNK, CHUNK), 0)
    col = lax.broadcasted_iota(jnp.int32, (CHUNK, CHUNK), 1)
    tril = col <= row
    ltri = jnp.where(tril, 1.0, 0.0).astype(BF16)
    nsub = CHUNK // SUB
    rsub = jnp.right_shift(lax.broadcasted_iota(jnp.int32, (CHUNK, D_KEY), 0), SUB.bit_length() - 1)
    heads = [slice(D_KEY * h, D_KEY * (h + 1)) for h in range(D_HEADS)]

    cums = []
    for _, _, _, lf in chunks:
        hi = lf.astype(BF16)
        r1 = lf - hi.astype(F32)
        mid = r1.astype(BF16)
        lo = (r1 - mid.astype(F32)).astype(BF16)
        cums.append(jnp.dot(ltri, hi, preferred_element_type=F32)
                    + jnp.dot(ltri, mid, preferred_element_type=F32)
                    + jnp.dot(ltri, lo, preferred_element_type=F32))

    between()
    pre = []
    for (q, kk, v, _), cum in zip(chunks, cums):
        tot = cum[CHUNK - 1:CHUNK]
        bases = [jnp.zeros((1, D_WIDTH), F32)] + [cum[SUB * i - 1:SUB * i] for i in range(1, nsub)]
        bsel = jnp.concatenate([jnp.broadcast_to(b, (SUB, D_WIDTH)) for b in bases], axis=0)
        qe = (q * jnp.exp(cum)).astype(BF16)
        qd = q * jnp.exp(cum - bsel)
        kdec = (kk * jnp.exp(tot - cum)).astype(BF16)
        qblks, kdcs = [], []
        for sl in heads:
            qd_h = qd[:, sl]
            qblks.append(jnp.concatenate([jnp.where(rsub == i, qd_h, 0.0) for i in range(nsub)],
                                         axis=1).astype(BF16))
            kds = []
            for i in range(nsub):
                n = SUB * (i + 1)
                kd = kk[0:n, sl] * jnp.exp(bases[i][:, sl] - cum[0:n, sl])
                if n < CHUNK:
                    kd = jnp.concatenate([kd, jnp.zeros((CHUNK - n, D_KEY), F32)], axis=0)
                kds.append(kd)
            kdcs.append(jnp.concatenate(kds, axis=1).astype(BF16))
        pre.append((qe, kdec, v.astype(BF16), jnp.exp(tot), qblks, kdcs))

    atts, incs = [], []
    for qe, kdec, vb, etot, qblks, kdcs in pre:
        atts.append([jnp.where(tril, _dot_nt(qblks[h], kdcs[h]), 0.0).astype(BF16) for h in range(D_HEADS)])
        incs.append([_dot_tn(vb[:, sl], kdec[:, sl]) for sl in heads])

    between()
    intra = [[_dot(atts[c][h], pre[c][2][:, heads[h]]) for h in range(D_HEADS)] for c in range(len(chunks))]

    between()
    outs, ends = [], []
    state = None
    for c, (qe, kdec, vb, etot, _, _) in enumerate(pre):
        if init_states[c] is not None:
            state = init_states[c]
        o = [_dot_nt(qe[:, heads[h]], state[h]) + intra[c][h] for h in range(D_HEADS)]
        state = [etot[:, heads[h]] * state[h] + incs[c][h] for h in range(D_HEADS)]
        outs.append(jnp.concatenate(o, axis=1))
        ends.append(state)
    return outs, ends


def _hgrn_readout(o, g, norm_g):
    parts = []
    for h in range(D_HEADS):
        oh = o[:, D_VAL * h:D_VAL * (h + 1)]
        parts.append(oh * lax.rsqrt(jnp.mean(oh * oh, axis=-1, keepdims=True) + RMS_EPS))
    return jnp.concatenate(parts, axis=1) * norm_g * _silu(g)


def _l1_short_kernel(x_ref, rowmask_ref, c0_ref, s0_ref, convw_ref, cvec_ref, lb_ref, normg_ref,
                     w_in_ref, b_in_ref, w_o_ref, ln_ref, wgu_ref, wd_ref,
                     h_ref, cout_ref, sout_ref,
                     c_s, mix_s):
    ns = c0_ref.shape[0]
    x = x_ref[...]
    rowmask = rowmask_ref[...]
    u = _c_project(x, rowmask, w_in_ref, b_in_ref)
    q, kk, v, lf, g = _d_project(x, rowmask, w_in_ref, b_in_ref, lb_ref)

    for s in range(ns):
        r0 = s * CHUNK
        cs = c_s.at[s]
        cs[C_OFF - C_HIST:C_OFF, :] = c0_ref[s]
        cs[C_OFF:C_OFF + CHUNK, :] = u[r0:r0 + CHUNK]
        _conformer_conv(cs, C_OFF, CHUNK, convw_ref, cvec_ref, mix_s, r0)
        cout_ref[s] = cs[C_OFF + CHUNK - C_HIST:C_OFF + CHUNK, :]

    rows = [slice(s * CHUNK, (s + 1) * CHUNK) for s in range(ns)]
    outs, ends = _hgrn_chunks([(q[r], kk[r], v[r], lf[r]) for r in rows],
                              [[s0_ref[s, h].T for h in range(D_HEADS)] for s in range(ns)])
    for s in range(ns):
        mix_s[rows[s], C_WIDTH:] = _hgrn_readout(outs[s], g[rows[s]], normg_ref[...])
        for h in range(D_HEADS):
            sout_ref[s, h] = ends[s][h].T

    h_ref[...] = _PostBlock(x, mix_s[...], w_o_ref, ln_ref, wgu_ref, wd_ref).result()


def _l1_main_kernel(nt, x_ref, xprev_ref, c0_ref, s0_ref, convw_ref, cvec_ref, lb_ref, normg_ref,
                    w_in_ref, b_in_ref, w_o_ref, ln_ref, wgu_ref, wd_ref,
                    h_ref, cout_ref, sout_ref,
                    c_s, st_s, mix_s):
    i, n, t = _tile_position(nt)
    tm = x_ref.shape[1]

    @pl.when(i == 0)
    def _():
        mix_s[...] = jnp.zeros_like(mix_s)

    @pl.when(t == 0)
    def _():
        c_s[C_OFF - C_HIST:C_OFF, :] = c0_ref[0]
        for h in range(D_HEADS):
            st_s[h] = s0_ref[0, h].T

    post = _PostBlock(xprev_ref[0], mix_s[...], w_o_ref, ln_ref, wgu_ref, wd_ref)
    post.run(1)

    x = x_ref[0]
    plan = iter(L1_FFN_STEPS)
    between = lambda: post.run(next(plan, 0))
    c_s[C_OFF:, :] = _c_project(x, None, w_in_ref, b_in_ref)
    q, kk, v, lf, g = _d_project(x, None, w_in_ref, b_in_ref, lb_ref)
    between()
    _conformer_conv(c_s, C_OFF, tm, convw_ref, cvec_ref, mix_s, 0)
    c_s[C_OFF - C_HIST:C_OFF, :] = c_s[C_OFF + tm - C_HIST:C_OFF + tm, :]

    nc = tm // CHUNK
    rows = [slice(c * CHUNK, (c + 1) * CHUNK) for c in range(nc)]
    outs, ends = _hgrn_chunks([(q[r], kk[r], v[r], lf[r]) for r in rows],
                              [[st_s[h] for h in range(D_HEADS)]] + [None] * (nc - 1), between)
    for c in range(nc):
        mix_s[rows[c], C_WIDTH:] = _hgrn_readout(outs[c], g[rows[c]], normg_ref[...])
    for h in range(D_HEADS):
        st_s[h] = ends[-1][h]

    h_ref[0] = post.result()

    @pl.when(jnp.logical_and(t == nt - 1, i < n))
    def _():
        cout_ref[0] = c_s[C_OFF - C_HIST:C_OFF, :]
        for h in range(D_HEADS):
            sout_ref[0, h] = st_s[h].T


def _whole(shape):
    zeros = (0,) * len(shape)
    return pl.BlockSpec(shape, lambda *_: zeros, pipeline_mode=pl.Buffered(1))


def _whole_out(shape):
    zeros = (0,) * len(shape)
    return pl.BlockSpec(shape, lambda *_: zeros)


def _params(n_axes):
    return pltpu.CompilerParams(dimension_semantics=("arbitrary",) * n_axes,
                                vmem_limit_bytes=VMEM_LIMIT)


def _l0_short(x, rowmask, k0, v0, u0, kvb, consts, weights):
    rows, ns = x.shape[0], k0.shape[0]
    ins = (x, rowmask, k0, v0, u0, kvb) + consts + weights
    return pl.pallas_call(
        _l0_short_kernel,
        grid=(1,),
        in_specs=[_whole(a.shape) for a in ins],
        out_specs=[_whole_out((rows, D_MODEL)), _whole_out((ns, WINDOW, KV_W)),
                   _whole_out((ns, WINDOW, KV_W)), _whole_out((ns, B_CONV - 1, B_WIDTH))],
        out_shape=[jax.ShapeDtypeStruct((rows, D_MODEL), F32),
                   jax.ShapeDtypeStruct((ns, WINDOW, KV_W), F32),
                   jax.ShapeDtypeStruct((ns, WINDOW, KV_W), F32),
                   jax.ShapeDtypeStruct((ns, B_CONV - 1, B_WIDTH), F32)],
        scratch_shapes=[pltpu.VMEM((ns, B_OFF + CHUNK, B_WIDTH), F32), pltpu.VMEM((rows, D_MODEL), F32)],
        compiler_params=_params(1),
        name="l0_short",
    )(*ins)


class _SkewedTiles:
    def __init__(self, nb, nt):
        self.nt, self.n = nt, nb * nt

    def _cur(self, i):
        return jnp.minimum(i, self.n - 1)

    def tile(self):
        return pl.BlockSpec((1, TILE, D_MODEL), lambda i: (self._cur(i) // self.nt, self._cur(i) % self.nt, 0))

    def prev_tile(self):
        prev = lambda i: jnp.maximum(i - 1, 0)
        return pl.BlockSpec((1, TILE, D_MODEL), lambda i: (prev(i) // self.nt, prev(i) % self.nt, 0))

    def fixed(self, shape, idx):
        zeros = (0,) * len(shape)
        return pl.BlockSpec((1,) + shape, lambda i: (idx,) + zeros)

    def per_prompt(self, shape):
        zeros = (0,) * len(shape)
        return pl.BlockSpec((1,) + shape, lambda i: (self._cur(i) // self.nt,) + zeros)


def _l0_main(x, k0, v0, u0, kvb, consts, weights, meta_idx):
    nb, seq, _ = x.shape
    nt = seq // TILE
    sk = _SkewedTiles(nb, nt)
    ins = (x, x, k0, v0, u0, kvb) + consts + weights
    in_specs = [sk.tile(), sk.prev_tile(),
                sk.fixed((WINDOW, KV_W), meta_idx), sk.fixed((WINDOW, KV_W), meta_idx),
                sk.fixed((B_CONV - 1, B_WIDTH), meta_idx),
                _whole(kvb.shape)] + [_whole(a.shape) for a in consts + weights]
    return pl.pallas_call(
        functools.partial(_l0_main_kernel, nt),
        grid=(nb * nt + 1,),
        in_specs=in_specs,
        out_specs=[sk.prev_tile(), sk.per_prompt((WINDOW, KV_W)), sk.per_prompt((WINDOW, KV_W)),
                   sk.per_prompt((B_CONV - 1, B_WIDTH))],
        out_shape=[jax.ShapeDtypeStruct((nb, seq, D_MODEL), F32),
                   jax.ShapeDtypeStruct((nb, WINDOW, KV_W), F32),
                   jax.ShapeDtypeStruct((nb, WINDOW, KV_W), F32),
                   jax.ShapeDtypeStruct((nb, B_CONV - 1, B_WIDTH), F32)],
        scratch_shapes=[pltpu.VMEM((WINDOW + TILE, KV_W), F32),
                        pltpu.VMEM((WINDOW + TILE, KV_W), F32), pltpu.VMEM((B_OFF + TILE, B_WIDTH), F32),
                        pltpu.VMEM((TILE, D_MODEL), F32)],
        compiler_params=_params(1),
        name="l0_main",
    )(*ins)


def _l1_short(x, rowmask, c0, s0, consts, weights):
    rows, ns = x.shape[0], c0.shape[0]
    ins = (x, rowmask, c0, s0) + consts + weights
    return pl.pallas_call(
        _l1_short_kernel,
        grid=(1,),
        in_specs=[_whole(a.shape) for a in ins],
        out_specs=[_whole_out((rows, D_MODEL)), _whole_out((ns, C_HIST, C_WIDTH)),
                   _whole_out((ns, D_HEADS, D_KEY, D_VAL))],
        out_shape=[jax.ShapeDtypeStruct((rows, D_MODEL), F32),
                   jax.ShapeDtypeStruct((ns, C_HIST, C_WIDTH), F32),
                   jax.ShapeDtypeStruct((ns, D_HEADS, D_KEY, D_VAL), F32)],
        scratch_shapes=[pltpu.VMEM((ns, C_OFF + CHUNK, C_WIDTH), F32), pltpu.VMEM((rows, D_MODEL), F32)],
        compiler_params=_params(1),
        name="l1_short",
    )(*ins)


def _l1_main(x, c0, s0, consts, weights, meta_idx):
    nb, seq, _ = x.shape
    nt = seq // TILE
    sk = _SkewedTiles(nb, nt)
    ins = (x, x, c0, s0) + consts + weights
    in_specs = [sk.tile(), sk.prev_tile(), sk.fixed((C_HIST, C_WIDTH), meta_idx),
                sk.fixed((D_HEADS, D_KEY, D_VAL), meta_idx),
                ] + [_whole(a.shape) for a in consts + weights]
    return pl.pallas_call(
        functools.partial(_l1_main_kernel, nt),
        grid=(nb * nt + 1,),
        in_specs=in_specs,
        out_specs=[sk.prev_tile(), sk.per_prompt((C_HIST, C_WIDTH)),
                   sk.per_prompt((D_HEADS, D_KEY, D_VAL))],
        out_shape=[jax.ShapeDtypeStruct((nb, seq, D_MODEL), F32),
                   jax.ShapeDtypeStruct((nb, C_HIST, C_WIDTH), F32),
                   jax.ShapeDtypeStruct((nb, D_HEADS, D_KEY, D_VAL), F32)],
        scratch_shapes=[pltpu.VMEM((C_OFF + TILE, C_WIDTH), F32), pltpu.VMEM((D_HEADS, D_VAL, D_KEY), F32),
                        pltpu.VMEM((TILE, D_MODEL), F32)],
        compiler_params=_params(1),
        name="l1_main",
    )(*ins)


def _alibi_tables():
    slopes = np.exp2(-8.0 * np.arange(1, A_HEADS + 1, dtype=np.float32) / A_HEADS).astype(np.float32)
    i = np.arange(CHUNK, dtype=np.float32)[None, :]
    j = np.arange(BAND, dtype=np.float32)[:, None]
    dist = np.abs(WINDOW + i - j).astype(np.float32)
    bias = np.zeros((A_KV_HEADS, 2 * BAND, 2 * CHUNK), np.float32)
    heads = np.zeros((2 * A_KV_HEADS, 2 * CHUNK), np.int32)
    for g in range(A_KV_HEADS):
        for e in range(2):
            for half, head in enumerate((4 * g + e, 4 * g + 2 + e)):
                cols = slice(CHUNK * half, CHUNK * (half + 1))
                bias[g, BAND * e:BAND * (e + 1), cols] = -slopes[head] * dist
                heads[2 * g + e, cols] = head
    return bias, heads


def kernel(x_prompt, x_sample, cache_k_a, cache_v_a, state_conv_b, state_conv_c, state_hgrn, meta_tokens,
           ab_w_in, ab_b_in, a_sinks, b_conv_w, ab_w_o, cd_w_in, cd_b_in, c_conv_w, c_conv_b, c_ln_g,
           c_ln_b, d_lower_bounds, d_norm_g, cd_w_o, ln1_g, ln1_b, ln2_g, ln2_b, ffn_w_gu, ffn_w_down):
    nb = x_prompt.shape[0]
    ns = x_sample.shape[0]
    assert x_sample.shape[1] == CHUNK and cache_k_a.shape[1] == WINDOW
    assert x_prompt.shape[1] % TILE == 0
    meta_idx = ns
    n_short = ns + 1
    rows = n_short * CHUNK

    xs = jnp.concatenate([x_sample.reshape(ns * CHUNK, D_MODEL),
                          jnp.zeros((PADF, D_MODEL), F32), meta_tokens.astype(F32)], axis=0)
    rowmask = np.ones((rows, 1), np.float32)
    rowmask[ns * CHUNK:ns * CHUNK + PADF] = 0.0
    rowmask = jnp.asarray(rowmask)

    kvb_short = np.zeros((BAND, 2 * CHUNK), np.float32)
    kvb_short[:WINDOW + PADF] = NEG
    kvb_main = np.zeros((2, BAND, 2 * CHUNK), np.float32)
    kvb_main[0, :CHUNK + PADF] = NEG
    kvb_main[1, :PADF] = NEG

    alibi, sink_heads = _alibi_tables()
    sinks = a_sinks.astype(F32)[jnp.asarray(sink_heads)][:, None, :]
    a_consts = (jnp.asarray(alibi), sinks, b_conv_w.astype(F32))

    zpad = lambda a: jnp.concatenate([a, jnp.zeros((1,) + a.shape[1:], a.dtype)], axis=0)
    k0 = zpad(cache_k_a.reshape(ns, WINDOW, KV_W))
    v0 = zpad(cache_v_a.reshape(ns, WINDOW, KV_W))
    u0 = zpad(state_conv_b)
    c0 = zpad(state_conv_c)
    s0 = zpad(state_hgrn)

    def layer_weights(l, w_in, b_in, w_o):
        ln = jnp.stack([ln1_g[l], ln1_b[l], ln2_g[l], ln2_b[l]]).astype(F32)
        return (w_in.astype(BF16), b_in.astype(F32)[None, :], w_o.astype(BF16), ln,
                ffn_w_gu[l].astype(BF16), ffn_w_down[l].astype(BF16))

    w0 = layer_weights(0, ab_w_in, ab_b_in, ab_w_o)
    w1 = layer_weights(1, cd_w_in, cd_b_in, cd_w_o)

    hs0, k_s, v_s, u_s = _l0_short(xs, rowmask, k0, v0, u0, jnp.asarray(kvb_short), a_consts, w0)
    hp0, k_p, v_p, u_p = _l0_main(x_prompt, k_s, v_s, u_s, jnp.asarray(kvb_main), a_consts, w0, meta_idx)

    cvec = jnp.stack([c_conv_b, c_ln_g, c_ln_b]).astype(F32)
    c_consts = (c_conv_w.astype(F32), cvec, d_lower_bounds.astype(F32), d_norm_g.astype(F32)[None, :])
    hs1, c_s, s_s = _l1_short(hs0, rowmask, c0, s0, c_consts, w1)
    hp1, c_p, s_p = _l1_main(hp0, c_s, s_s, c_consts, w1, meta_idx)

    kv = lambda a: a.reshape(a.shape[0], WINDOW, A_KV_HEADS, HEAD_DIM)
    y_sample = hs1[:ns * CHUNK].reshape(ns, CHUNK, D_MODEL)
    return (hp1, y_sample, kv(k_p), kv(v_p), u_p, c_p, s_p,
            kv(k_s[:ns]), kv(v_s[:ns]), u_s[:ns], c_s[:ns], s_s[:ns])
```

```python
import functools

import jax
import jax.numpy as jnp
import numpy as np
from jax import lax
from jax.experimental import pallas as pl
from jax.experimental.pallas import tpu as pltpu

F32 = jnp.float32
BF16 = jnp.bfloat16

D_MODEL = 1024
CHUNK = 64
N_META = 16
PADF = CHUNK - N_META
A_HEADS = 8
A_KV_HEADS = 2
HEAD_DIM = 64
WINDOW = 128
KV_W = A_KV_HEADS * HEAD_DIM
Q_W = A_HEADS * HEAD_DIM
BAND = WINDOW + CHUNK
B_WIDTH = 512
B_CONV = 3
C_WIDTH = 512
C_CONV = 31
D_HEADS = 4
D_KEY = 128
D_VAL = 128
D_WIDTH = D_HEADS * D_VAL
SUB = 16
FFN_HIDDEN = 2816
DEPTH = 2
ALPHA = (2 * DEPTH) ** 0.25
LN_EPS = 1e-5
RMS_EPS = 1e-6
NEG = -1e30

AB_IN = Q_W + 2 * KV_W + 3 * B_WIDTH
CD_IN = 2 * C_WIDTH + 4 * D_WIDTH

TILE = 512
FFN_BLOCK = 256
L0_FFN_STEPS_BETWEEN = (1,) * (2 * TILE // CHUNK)
L1_FFN_STEPS = (6, 2, 2, 2)
SUBLANES = 8
LANES = 128
CONV_ROWS = 64
C_HIST = C_CONV - 1
C_OFF = 32
B_OFF = 8
VMEM_LIMIT = 60 * 1024 * 1024


def _dot(a, b):
    return jnp.dot(a.astype(BF16), b.astype(BF16), preferred_element_type=F32)


def _dot_nt(a, b):
    return lax.dot_general(a.astype(BF16), b.astype(BF16), (((1,), (1,)), ((), ())),
                           preferred_element_type=F32)


def _dot_tn(a, b):
    return lax.dot_general(a.astype(BF16), b.astype(BF16), (((0,), (0,)), ((), ())),
                           preferred_element_type=F32)


def _sigmoid(x):
    return 1.0 / (1.0 + jnp.exp(-x))


def _silu(x):
    return x * _sigmoid(x)


def _layer_norm(x, g, b):
    mu = jnp.mean(x, axis=-1, keepdims=True)
    xc = x - mu
    var = jnp.mean(xc * xc, axis=-1, keepdims=True)
    return xc * lax.rsqrt(var + LN_EPS) * g + b


class _PostBlock:
    N_BLOCKS = FFN_HIDDEN // FFN_BLOCK
    N_STEPS = 2 + N_BLOCKS

    def __init__(self, x, mix, w_o_ref, ln_ref, wgu_ref, wd_ref):
        self.x, self.mix = x, mix
        self.w_o_ref, self.ln_ref, self.wgu_ref, self.wd_ref = w_o_ref, ln_ref, wgu_ref.at[0], wd_ref.at[0]
        self.done = 0
        self.h = self.hb = self.acc = self.act = None

    def _gate_up(self, j):
        lo = j * FFN_BLOCK
        gate = jnp.dot(self.hb, self.wgu_ref[:, lo:lo + FFN_BLOCK], preferred_element_type=F32)
        up = jnp.dot(self.hb, self.wgu_ref[:, FFN_HIDDEN + lo:FFN_HIDDEN + lo + FFN_BLOCK],
                     preferred_element_type=F32)
        return (_silu(gate) * up).astype(BF16)

    def _step(self, j):
        if j == 0:
            ln = self.ln_ref
            self.h = _layer_norm(ALPHA * self.x + _dot(self.mix, self.w_o_ref[...]), ln[0:1], ln[1:2])
            self.hb = self.h.astype(BF16)
            return
        prev = self.act
        self.act = self._gate_up(j - 1) if j - 1 < self.N_BLOCKS else None
        if prev is not None:
            lo = (j - 2) * FFN_BLOCK
            part = jnp.dot(prev, self.wd_ref[lo:lo + FFN_BLOCK, :], preferred_element_type=F32)
            self.acc = part if self.acc is None else self.acc + part

    def run(self, n):
        for _ in range(n):
            if self.done < self.N_STEPS:
                self._step(self.done)
                self.done += 1

    def result(self):
        self.run(self.N_STEPS)
        return _layer_norm(ALPHA * self.h + self.acc, self.ln_ref[2:3], self.ln_ref[3:4])


def _no_op():
    pass


def _attn_chunks(chunks, alibi_ref, sink_ref, between=_no_op):
    lane = lax.broadcasted_iota(jnp.int32, (BAND, KV_W), 1)
    low = lane < HEAD_DIM
    high = jnp.logical_not(low)

    def extend(band, swapped, g, e):
        src = band if g == e else swapped
        return jnp.where(low if e == 0 else high, src, 0.0).astype(BF16)

    scores = []
    for q, kband, _, kvbias in chunks:
        kswap = pltpu.roll(kband, HEAD_DIM, axis=1)
        for g in range(A_KV_HEADS):
            qg = jnp.concatenate([q[:, 256 * g:256 * g + 128], q[:, 256 * g + 128:256 * g + 256]],
                                 axis=0).astype(BF16)
            k2 = jnp.concatenate([extend(kband, kswap, g, e) for e in range(2)], axis=0)
            s = _dot_nt(k2, qg) + alibi_ref[g]
            for e in range(2):
                se = s[BAND * e:BAND * (e + 1)]
                scores.append(se if kvbias is None else se + kvbias)
        between()

    probs = []
    for idx, s in enumerate(scores):
        sink = sink_ref[idx % (2 * A_KV_HEADS)]
        m = jnp.maximum(jnp.max(s, axis=0, keepdims=True), sink)
        p = jnp.exp(s - m)
        den = jnp.sum(p, axis=0, keepdims=True) + jnp.exp(sink - m)
        probs.append((p * (1.0 / den)).astype(BF16))
        if idx % (2 * A_KV_HEADS) == 2 * A_KV_HEADS - 1:
            between()

    outs = []
    for ci, (_, _, vband, _) in enumerate(chunks):
        vswap = pltpu.roll(vband, HEAD_DIM, axis=1)
        parts = []
        for g in range(A_KV_HEADS):
            base = ci * 2 * A_KV_HEADS + 2 * g
            p2 = jnp.concatenate([probs[base], probs[base + 1]], axis=0)
            v2 = jnp.concatenate([extend(vband, vswap, g, e) for e in range(2)], axis=0)
            o = _dot_tn(p2, v2)
            parts += [o[0:CHUNK], o[CHUNK:2 * CHUNK]]
        outs.append(jnp.concatenate(parts, axis=1))
    return outs


def _ab_project(x, w_in_ref, b_in_ref):
    proj = _dot(x, w_in_ref[...]) + b_in_ref[...]
    q = proj[:, 0:Q_W] * (HEAD_DIM ** -0.5)
    k = proj[:, Q_W:Q_W + KV_W]
    v = proj[:, Q_W + KV_W:Q_W + 2 * KV_W]
    o = Q_W + 2 * KV_W
    bg = proj[:, o:o + B_WIDTH]
    u = proj[:, o + B_WIDTH:o + 2 * B_WIDTH] * proj[:, o + 2 * B_WIDTH:o + 3 * B_WIDTH]
    return q, k, v, bg, u


def _short_conv(u_ref, base, rows, w_ref):
    out = None
    for j in range(B_CONV):
        term = u_ref[pl.ds(base - (B_CONV - 1) + j, rows), :] * w_ref[j:j + 1, :]
        out = term if out is None else out + term
    return out


def _l0_short_kernel(x_ref, rowmask_ref, k0_ref, v0_ref, u0_ref, kvb_ref, alibi_ref, sink_ref,
                     convw_ref, w_in_ref, b_in_ref, w_o_ref, ln_ref, wgu_ref, wd_ref,
                     h_ref, kout_ref, vout_ref, uout_ref,
                     u_s, mix_s):
    ns = k0_ref.shape[0]
    x = x_ref[...]
    q, k, v, bg, u = _ab_project(x, w_in_ref, b_in_ref)
    u = u * rowmask_ref[...]

    chunks = []
    for s in range(ns):
        r0 = s * CHUNK
        kband = jnp.concatenate([k0_ref[s], k[r0:r0 + CHUNK]], axis=0)
        vband = jnp.concatenate([v0_ref[s], v[r0:r0 + CHUNK]], axis=0)
        kout_ref[s] = kband[CHUNK:]
        vout_ref[s] = vband[CHUNK:]
        chunks.append((q[r0:r0 + CHUNK], kband, vband, kvb_ref[...] if s == ns - 1 else None))
    for s, att in enumerate(_attn_chunks(chunks, alibi_ref, sink_ref)):
        mix_s[s * CHUNK:(s + 1) * CHUNK, 0:Q_W] = att

    for s in range(ns):
        r0 = s * CHUNK
        us = u_s.at[s]
        us[B_OFF - 2:B_OFF, :] = u0_ref[s]
        us[B_OFF:B_OFF + CHUNK, :] = u[r0:r0 + CHUNK]
        cb = _short_conv(us, B_OFF, CHUNK, convw_ref)
        mix_s[r0:r0 + CHUNK, Q_W:] = bg[r0:r0 + CHUNK] * cb
        uout_ref[s] = u[r0 + CHUNK - 2:r0 + CHUNK]

    h_ref[...] = _PostBlock(x, mix_s[...], w_o_ref, ln_ref, wgu_ref, wd_ref).result()


def _tile_position(nt):
    i = pl.program_id(0)
    n = pl.num_programs(0) - 1
    return i, n, lax.rem(jnp.minimum(i, n - 1), nt)


def _l0_main_kernel(nt, x_ref, xprev_ref, k0_ref, v0_ref, u0_ref, kvb_ref, alibi_ref, sink_ref,
                    convw_ref, w_in_ref, b_in_ref, w_o_ref, ln_ref, wgu_ref, wd_ref,
                    h_ref, kout_ref, vout_ref, uout_ref,
                    k_s, v_s, u_s, mix_s):
    i, n, t = _tile_position(nt)
    tm = x_ref.shape[1]

    @pl.when(i == 0)
    def _():
        mix_s[...] = jnp.zeros_like(mix_s)

    @pl.when(t == 0)
    def _():
        k_s[0:WINDOW, :] = k0_ref[0]
        v_s[0:WINDOW, :] = v0_ref[0]
        u_s[B_OFF - 2:B_OFF, :] = u0_ref[0]

    post = _PostBlock(xprev_ref[0], mix_s[...], w_o_ref, ln_ref, wgu_ref, wd_ref)
    post.run(1)

    q, k, v, bg, u = _ab_project(x_ref[0], w_in_ref, b_in_ref)
    k_s[WINDOW:, :] = k
    v_s[WINDOW:, :] = v
    u_s[B_OFF:, :] = u
    mix_s[:, Q_W:] = bg * _short_conv(u_s, B_OFF, tm, convw_ref)

    first = t == 0
    chunks = []
    for c in range(tm // CHUNK):
        r0 = c * CHUNK
        kvbias = jnp.where(first, kvb_ref[c], 0.0) if c < 2 else None
        chunks.append((q[r0:r0 + CHUNK], k_s[r0:r0 + BAND, :], v_s[r0:r0 + BAND, :], kvbias))
    plan = iter(L0_FFN_STEPS_BETWEEN)
    outs = _attn_chunks(chunks, alibi_ref, sink_ref, between=lambda: post.run(next(plan, 0)))
    for c, att in enumerate(outs):
        mix_s[c * CHUNK:(c + 1) * CHUNK, 0:Q_W] = att

    h_ref[0] = post.result()

    k_s[0:WINDOW, :] = k_s[tm:tm + WINDOW, :]
    v_s[0:WINDOW, :] = v_s[tm:tm + WINDOW, :]
    u_s[B_OFF - 2:B_OFF, :] = u_s[B_OFF + tm - 2:B_OFF + tm, :]

    @pl.when(jnp.logical_and(t == nt - 1, i < n))
    def _():
        kout_ref[0] = k_s[0:WINDOW, :]
        vout_ref[0] = v_s[0:WINDOW, :]
        uout_ref[0] = u_s[B_OFF - 2:B_OFF, :]


def _c_project(x, rowmask, w_in_ref, b_in_ref):
    w = C_WIDTH
    proj = _dot(x, w_in_ref[:, 0:2 * w]) + b_in_ref[:, 0:2 * w]
    u = proj[:, 0:w] * _sigmoid(proj[:, w:2 * w])
    return u if rowmask is None else u * rowmask


def _d_project(x, rowmask, w_in_ref, b_in_ref, lb_ref):
    w = D_WIDTH
    proj = _dot(x, w_in_ref[:, 2 * C_WIDTH:]) + b_in_ref[:, 2 * C_WIDTH:]
    q = proj[:, 0:w]
    d0 = lb_ref[0:1]
    d1 = lb_ref[1:2]
    mx = jnp.maximum(d0, d1)
    e0 = jnp.exp(d0 - mx)
    e1 = jnp.exp(d1 - mx)
    p0 = e0 / (e0 + e1)
    p1 = e1 / (e0 + e1)
    lb = (p0 + p1) - p0
    forget = lb + (1.0 - lb) * _sigmoid(proj[:, w:2 * w])
    kk = 1.0 - forget
    lf = jnp.log(forget)
    v = proj[:, 2 * w:3 * w]
    g = proj[:, 3 * w:4 * w]
    if rowmask is not None:
        q = q * rowmask
        kk = kk * rowmask
        v = v * rowmask
        lf = lf * rowmask
    return q, kk, v, lf, g


def _conformer_conv(c_ref, base, rows, convw_ref, cvec_ref, out_ref, out_base, between=_no_op):
    for r0 in range(0, rows, CONV_ROWS):
        cols = []
        for c0 in range(0, C_WIDTH, LANES):
            lanes = slice(c0, c0 + LANES)
            win = c_ref[base - C_OFF + r0:base + r0 + CONV_ROWS, lanes]
            acc = None
            for phase in range(SUBLANES):
                shifted = win if phase == 0 else pltpu.roll(win, CONV_ROWS + C_OFF - phase, axis=0)
                for j in range(C_CONV):
                    off = C_OFF - C_HIST + j
                    if off % SUBLANES != phase:
                        continue
                    lo = off - phase
                    term = shifted[lo:lo + CONV_ROWS] * convw_ref[j:j + 1, lanes]
                    acc = term if acc is None else acc + term
            cols.append(acc)
            between()
        acc = jnp.concatenate(cols, axis=1)
        y = _layer_norm(acc + cvec_ref[0:1], cvec_ref[1:2], cvec_ref[2:3])
        out_ref[out_base + r0:out_base + r0 + CONV_ROWS, 0:C_WIDTH] = _silu(y)


def _hgrn_chunks(chunks, init_states, between=_no_op):
    row = lax.broadcasted_iota(jnp.int32, (CHUNK, CHUNK), 0)
    col = lax.broadcasted_iota(jnp.int32, (CHUNK, CHUNK), 1)
    tril = col <= row
    ltri = jnp.where(tril, 1.0, 0.0).astype(BF16)
    nsub = CHUNK // SUB
    rsub = jnp.right_shift(lax.broadcasted_iota(jnp.int32, (CHUNK, D_KEY), 0), SUB.bit_length() - 1)
    heads = [slice(D_KEY * h, D_KEY * (h + 1)) for h in range(D_HEADS)]

    cums = []
    for _, _, _, lf in chunks:
        hi = lf.astype(BF16)
        r1 = lf - hi.astype(F32)
        mid = r1.astype(BF16)
        lo = (r1 - mid.astype(F32)).astype(BF16)
        cums.append(jnp.dot(ltri, hi, preferred_element_type=F32)
                    + jnp.dot(ltri, mid, preferred_element_type=F32)
                    + jnp.dot(ltri, lo, preferred_element_type=F32))

    between()
    pre = []
    for (q, kk, v, _), cum in zip(chunks, cums):
        tot = cum[CHUNK - 1:CHUNK]
        bases = [jnp.zeros((1, D_WIDTH), F32)] + [cum[SUB * i - 1:SUB * i] for i in range(1, nsub)]
        bsel = jnp.concatenate([jnp.broadcast_to(b, (SUB, D_WIDTH)) for b in bases], axis=0)
        qe = (q * jnp.exp(cum)).astype(BF16)
        qd = q * jnp.exp(cum - bsel)
        kdec = (kk * jnp.exp(tot - cum)).astype(BF16)
        qblks, kdcs = [], []
        for sl in heads:
            qd_h = qd[:, sl]
            qblks.append(jnp.concatenate([jnp.where(rsub == i, qd_h, 0.0) for i in range(nsub)],
                                         axis=1).astype(BF16))
            kds = []
            for i in range(nsub):
                n = SUB * (i + 1)
                kd = kk[0:n, sl] * jnp.exp(bases[i][:, sl] - cum[0:n, sl])
                if n < CHUNK:
                    kd = jnp.concatenate([kd, jnp.zeros((CHUNK - n, D_KEY), F32)], axis=0)
                kds.append(kd)
            kdcs.append(jnp.concatenate(kds, axis=1).astype(BF16))
        pre.append((qe, kdec, v.astype(BF16), jnp.exp(tot), qblks, kdcs))

    atts, incs = [], []
    for qe, kdec, vb, etot, qblks, kdcs in pre:
        atts.append([jnp.where(tril, _dot_nt(qblks[h], kdcs[h]), 0.0).astype(BF16) for h in range(D_HEADS)])
        incs.append([_dot_tn(vb[:, sl], kdec[:, sl]) for sl in heads])

    between()
    intra = [[_dot(atts[c][h], pre[c][2][:, heads[h]]) for h in range(D_HEADS)] for c in range(len(chunks))]

    between()
    outs, ends = [], []
    state = None
    for c, (qe, kdec, vb, etot, _, _) in enumerate(pre):
        if init_states[c] is not None:
            state = init_states[c]
        o = [_dot_nt(qe[:, heads[h]], state[h]) + intra[c][h] for h in range(D_HEADS)]
        state = [etot[:, heads[h]] * state[h] + incs[c][h] for h in range(D_HEADS)]
        outs.append(jnp.concatenate(o, axis=1))
        ends.append(state)
    return outs, ends


def _hgrn_readout(o, g, norm_g):
    parts = []
    for h in range(D_HEADS):
        oh = o[:, D_VAL * h:D_VAL * (h + 1)]
        parts.append(oh * lax.rsqrt(jnp.mean(oh * oh, axis=-1, keepdims=True) + RMS_EPS))
    return jnp.concatenate(parts, axis=1) * norm_g * _silu(g)


def _l1_short_kernel(x_ref, rowmask_ref, c0_ref, s0_ref, convw_ref, cvec_ref, lb_ref, normg_ref,
                     w_in_ref, b_in_ref, w_o_ref, ln_ref, wgu_ref, wd_ref,
                     h_ref, cout_ref, sout_ref,
                     c_s, mix_s):
    ns = c0_ref.shape[0]
    x = x_ref[...]
    rowmask = rowmask_ref[...]
    u = _c_project(x, rowmask, w_in_ref, b_in_ref)
    q, kk, v, lf, g = _d_project(x, rowmask, w_in_ref, b_in_ref, lb_ref)

    for s in range(ns):
        r0 = s * CHUNK
        cs = c_s.at[s]
        cs[C_OFF - C_HIST:C_OFF, :] = c0_ref[s]
        cs[C_OFF:C_OFF + CHUNK, :] = u[r0:r0 + CHUNK]
        _conformer_conv(cs, C_OFF, CHUNK, convw_ref, cvec_ref, mix_s, r0)
        cout_ref[s] = cs[C_OFF + CHUNK - C_HIST:C_OFF + CHUNK, :]

    rows = [slice(s * CHUNK, (s + 1) * CHUNK) for s in range(ns)]
    outs, ends = _hgrn_chunks([(q[r], kk[r], v[r], lf[r]) for r in rows],
                              [[s0_ref[s, h].T for h in range(D_HEADS)] for s in range(ns)])
    for s in range(ns):
        mix_s[rows[s], C_WIDTH:] = _hgrn_readout(outs[s], g[rows[s]], normg_ref[...])
        for h in range(D_HEADS):
            sout_ref[s, h] = ends[s][h].T

    h_ref[...] = _PostBlock(x, mix_s[...], w_o_ref, ln_ref, wgu_ref, wd_ref).result()


def _l1_main_kernel(nt, x_ref, xprev_ref, c0_ref, s0_ref, convw_ref, cvec_ref, lb_ref, normg_ref,
                    w_in_ref, b_in_ref, w_o_ref, ln_ref, wgu_ref, wd_ref,
                    h_ref, cout_ref, sout_ref,
                    c_s, st_s, mix_s):
    i, n, t = _tile_position(nt)
    tm = x_ref.shape[1]

    @pl.when(i == 0)
    def _():
        mix_s[...] = jnp.zeros_like(mix_s)

    @pl.when(t == 0)
    def _():
        c_s[C_OFF - C_HIST:C_OFF, :] = c0_ref[0]
        for h in range(D_HEADS):
            st_s[h] = s0_ref[0, h].T

    post = _PostBlock(xprev_ref[0], mix_s[...], w_o_ref, ln_ref, wgu_ref, wd_ref)
    post.run(1)

    x = x_ref[0]
    plan = iter(L1_FFN_STEPS)
    between = lambda: post.run(next(plan, 0))
    c_s[C_OFF:, :] = _c_project(x, None, w_in_ref, b_in_ref)
    q, kk, v, lf, g = _d_project(x, None, w_in_ref, b_in_ref, lb_ref)
    between()
    _conformer_conv(c_s, C_OFF, tm, convw_ref, cvec_ref, mix_s, 0)
    c_s[C_OFF - C_HIST:C_OFF, :] = c_s[C_OFF + tm - C_HIST:C_OFF + tm, :]

    nc = tm // CHUNK
    rows = [slice(c * CHUNK, (c + 1) * CHUNK) for c in range(nc)]
    outs, ends = _hgrn_chunks([(q[r], kk[r], v[r], lf[r]) for r in rows],
                              [[st_s[h] for h in range(D_HEADS)]] + [None] * (nc - 1), between)
    for c in range(nc):
        mix_s[rows[c], C_WIDTH:] = _hgrn_readout(outs[c], g[rows[c]], normg_ref[...])
    for h in range(D_HEADS):
        st_s[h] = ends[-1][h]

    h_ref[0] = post.result()

    @pl.when(jnp.logical_and(t == nt - 1, i < n))
    def _():
        cout_ref[0] = c_s[C_OFF - C_HIST:C_OFF, :]
        for h in range(D_HEADS):
            sout_ref[0, h] = st_s[h].T


def _whole(shape):
    zeros = (0,) * len(shape)
    return pl.BlockSpec(shape, lambda *_: zeros, pipeline_mode=pl.Buffered(1))


def _weight_specs(weights, layer):
    def stacked(a):
        zeros = (0,) * (a.ndim - 1)
        return pl.BlockSpec((1,) + a.shape[1:], lambda *_: (layer,) + zeros, pipeline_mode=pl.Buffered(1))

    return [_whole(a.shape) for a in weights[:-2]] + [stacked(a) for a in weights[-2:]]


def _whole_out(shape):
    zeros = (0,) * len(shape)
    return pl.BlockSpec(shape, lambda *_: zeros)


def _params(n_axes):
    return pltpu.CompilerParams(dimension_semantics=("arbitrary",) * n_axes,
                                vmem_limit_bytes=VMEM_LIMIT)


def _l0_short(x, rowmask, k0, v0, u0, kvb, consts, weights, layer):
    rows, ns = x.shape[0], k0.shape[0]
    ins = (x, rowmask, k0, v0, u0, kvb) + consts
    return pl.pallas_call(
        _l0_short_kernel,
        grid=(1,),
        in_specs=[_whole(a.shape) for a in ins] + _weight_specs(weights, layer),
        out_specs=[_whole_out((rows, D_MODEL)), _whole_out((ns, WINDOW, KV_W)),
                   _whole_out((ns, WINDOW, KV_W)), _whole_out((ns, B_CONV - 1, B_WIDTH))],
        out_shape=[jax.ShapeDtypeStruct((rows, D_MODEL), F32),
                   jax.ShapeDtypeStruct((ns, WINDOW, KV_W), F32),
                   jax.ShapeDtypeStruct((ns, WINDOW, KV_W), F32),
                   jax.ShapeDtypeStruct((ns, B_CONV - 1, B_WIDTH), F32)],
        scratch_shapes=[pltpu.VMEM((ns, B_OFF + CHUNK, B_WIDTH), F32), pltpu.VMEM((rows, D_MODEL), F32)],
        compiler_params=_params(1),
        name="l0_short",
    )(*ins, *weights)


class _SkewedTiles:
    def __init__(self, nb, nt):
        self.nt, self.n = nt, nb * nt

    def _cur(self, i):
        return jnp.minimum(i, self.n - 1)

    def tile(self):
        return pl.BlockSpec((1, TILE, D_MODEL), lambda i: (self._cur(i) // self.nt, self._cur(i) % self.nt, 0))

    def prev_tile(self):
        prev = lambda i: jnp.maximum(i - 1, 0)
        return pl.BlockSpec((1, TILE, D_MODEL), lambda i: (prev(i) // self.nt, prev(i) % self.nt, 0))

    def fixed(self, shape, idx):
        zeros = (0,) * len(shape)
        return pl.BlockSpec((1,) + shape, lambda i: (idx,) + zeros)

    def per_prompt(self, shape):
        zeros = (0,) * len(shape)
        return pl.BlockSpec((1,) + shape, lambda i: (self._cur(i) // self.nt,) + zeros)


def _l0_main(x, k0, v0, u0, kvb, consts, weights, layer, meta_idx):
    nb, seq, _ = x.shape
    nt = seq // TILE
    sk = _SkewedTiles(nb, nt)
    ins = (x, x, k0, v0, u0, kvb) + consts + weights
    in_specs = [sk.tile(), sk.prev_tile(),
                sk.fixed((WINDOW, KV_W), meta_idx), sk.fixed((WINDOW, KV_W), meta_idx),
                sk.fixed((B_CONV - 1, B_WIDTH), meta_idx),
                _whole(kvb.shape)] + [_whole(a.shape) for a in consts] + _weight_specs(weights, layer)
    return pl.pallas_call(
        functools.partial(_l0_main_kernel, nt),
        grid=(nb * nt + 1,),
        in_specs=in_specs,
        out_specs=[sk.prev_tile(), sk.per_prompt((WINDOW, KV_W)), sk.per_prompt((WINDOW, KV_W)),
                   sk.per_prompt((B_CONV - 1, B_WIDTH))],
        out_shape=[jax.ShapeDtypeStruct((nb, seq, D_MODEL), F32),
                   jax.ShapeDtypeStruct((nb, WINDOW, KV_W), F32),
                   jax.ShapeDtypeStruct((nb, WINDOW, KV_W), F32),
                   jax.ShapeDtypeStruct((nb, B_CONV - 1, B_WIDTH), F32)],
        scratch_shapes=[pltpu.VMEM((WINDOW + TILE, KV_W), F32),
                        pltpu.VMEM((WINDOW + TILE, KV_W), F32), pltpu.VMEM((B_OFF + TILE, B_WIDTH), F32),
                        pltpu.VMEM((TILE, D_MODEL), F32)],
        compiler_params=_params(1),
        name="l0_main",
    )(*ins)


def _l1_short(x, rowmask, c0, s0, consts, weights, layer):
    rows, ns = x.shape[0], c0.shape[0]
    ins = (x, rowmask, c0, s0) + consts
    return pl.pallas_call(
        _l1_short_kernel,
        grid=(1,),
        in_specs=[_whole(a.shape) for a in ins] + _weight_specs(weights, layer),
        out_specs=[_whole_out((rows, D_MODEL)), _whole_out((ns, C_HIST, C_WIDTH)),
                   _whole_out((ns, D_HEADS, D_KEY, D_VAL))],
        out_shape=[jax.ShapeDtypeStruct((rows, D_MODEL), F32),
                   jax.ShapeDtypeStruct((ns, C_HIST, C_WIDTH), F32),
                   jax.ShapeDtypeStruct((ns, D_HEADS, D_KEY, D_VAL), F32)],
        scratch_shapes=[pltpu.VMEM((ns, C_OFF + CHUNK, C_WIDTH), F32), pltpu.VMEM((rows, D_MODEL), F32)],
        compiler_params=_params(1),
        name="l1_short",
    )(*ins, *weights)


def _l1_main(x, c0, s0, consts, weights, layer, meta_idx):
    nb, seq, _ = x.shape
    nt = seq // TILE
    sk = _SkewedTiles(nb, nt)
    ins = (x, x, c0, s0) + consts + weights
    in_specs = [sk.tile(), sk.prev_tile(), sk.fixed((C_HIST, C_WIDTH), meta_idx),
                sk.fixed((D_HEADS, D_KEY, D_VAL), meta_idx),
                ] + [_whole(a.shape) for a in consts] + _weight_specs(weights, layer)
    return pl.pallas_call(
        functools.partial(_l1_main_kernel, nt),
        grid=(nb * nt + 1,),
        in_specs=in_specs,
        out_specs=[sk.prev_tile(), sk.per_prompt((C_HIST, C_WIDTH)),
                   sk.per_prompt((D_HEADS, D_KEY, D_VAL))],
        out_shape=[jax.ShapeDtypeStruct((nb, seq, D_MODEL), F32),
                   jax.ShapeDtypeStruct((nb, C_HIST, C_WIDTH), F32),
                   jax.ShapeDtypeStruct((nb, D_HEADS, D_KEY, D_VAL), F32)],
        scratch_shapes=[pltpu.VMEM((C_OFF + TILE, C_WIDTH), F32), pltpu.VMEM((D_HEADS, D_VAL, D_KEY), F32),
                        pltpu.VMEM((TILE, D_MODEL), F32)],
        compiler_params=_params(1),
        name="l1_main",
    )(*ins)


def _alibi_tables():
    slopes = np.exp2(-8.0 * np.arange(1, A_HEADS + 1, dtype=np.float32) / A_HEADS).astype(np.float32)
    i = np.arange(CHUNK, dtype=np.float32)[None, :]
    j = np.arange(BAND, dtype=np.float32)[:, None]
    dist = np.abs(WINDOW + i - j).astype(np.float32)
    bias = np.zeros((A_KV_HEADS, 2 * BAND, 2 * CHUNK), np.float32)
    heads = np.zeros((2 * A_KV_HEADS, 2 * CHUNK), np.int32)
    for g in range(A_KV_HEADS):
        for e in range(2):
            for half, head in enumerate((4 * g + e, 4 * g + 2 + e)):
                cols = slice(CHUNK * half, CHUNK * (half + 1))
                bias[g, BAND * e:BAND * (e + 1), cols] = -slopes[head] * dist
                heads[2 * g + e, cols] = head
    return bias, heads


def kernel(x_prompt, x_sample, cache_k_a, cache_v_a, state_conv_b, state_conv_c, state_hgrn, meta_tokens,
           ab_w_in, ab_b_in, a_sinks, b_conv_w, ab_w_o, cd_w_in, cd_b_in, c_conv_w, c_conv_b, c_ln_g,
           c_ln_b, d_lower_bounds, d_norm_g, cd_w_o, ln1_g, ln1_b, ln2_g, ln2_b, ffn_w_gu, ffn_w_down):
    nb = x_prompt.shape[0]
    ns = x_sample.shape[0]
    assert x_sample.shape[1] == CHUNK and cache_k_a.shape[1] == WINDOW
    assert x_prompt.shape[1] % TILE == 0
    meta_idx = ns
    n_short = ns + 1
    rows = n_short * CHUNK

    xs = jnp.concatenate([x_sample.reshape(ns * CHUNK, D_MODEL),
                          jnp.zeros((PADF, D_MODEL), F32), meta_tokens.astype(F32)], axis=0)
    rowmask = np.ones((rows, 1), np.float32)
    rowmask[ns * CHUNK:ns * CHUNK + PADF] = 0.0
    rowmask = jnp.asarray(rowmask)

    kvb_short = np.zeros((BAND, 2 * CHUNK), np.float32)
    kvb_short[:WINDOW + PADF] = NEG
    kvb_main = np.zeros((2, BAND, 2 * CHUNK), np.float32)
    kvb_main[0, :CHUNK + PADF] = NEG
    kvb_main[1, :PADF] = NEG

    alibi, sink_heads = _alibi_tables()
    sinks = a_sinks.astype(F32)[jnp.asarray(sink_heads)][:, None, :]
    a_consts = (jnp.asarray(alibi), sinks, b_conv_w.astype(F32))

    zpad = lambda a: jnp.concatenate([a, jnp.zeros((1,) + a.shape[1:], a.dtype)], axis=0)
    k0 = zpad(cache_k_a.reshape(ns, WINDOW, KV_W))
    v0 = zpad(cache_v_a.reshape(ns, WINDOW, KV_W))
    u0 = zpad(state_conv_b)
    c0 = zpad(state_conv_c)
    s0 = zpad(state_hgrn)

    w_gu = ffn_w_gu.astype(BF16)
    w_down = ffn_w_down.astype(BF16)

    def layer_weights(l, w_in, b_in, w_o):
        ln = jnp.stack([ln1_g[l], ln1_b[l], ln2_g[l], ln2_b[l]]).astype(F32)
        return (w_in.astype(BF16), b_in.astype(F32)[None, :], w_o.astype(BF16), ln, w_gu, w_down)

    w0 = layer_weights(0, ab_w_in, ab_b_in, ab_w_o)
    w1 = layer_weights(1, cd_w_in, cd_b_in, cd_w_o)

    hs0, k_s, v_s, u_s = _l0_short(xs, rowmask, k0, v0, u0, jnp.asarray(kvb_short), a_consts, w0, 0)
    hp0, k_p, v_p, u_p = _l0_main(x_prompt, k_s, v_s, u_s, jnp.asarray(kvb_main), a_consts, w0, 0, meta_idx)

    cvec = jnp.stack([c_conv_b, c_ln_g, c_ln_b]).astype(F32)
    c_consts = (c_conv_w.astype(F32), cvec, d_lower_bounds.astype(F32), d_norm_g.astype(F32)[None, :])
    hs1, c_s, s_s = _l1_short(hs0, rowmask, c0, s0, c_consts, w1, 1)
    hp1, c_p, s_p = _l1_main(hp0, c_s, s_s, c_consts, w1, 1, meta_idx)

    kv = lambda a: a.reshape(a.shape[0], WINDOW, A_KV_HEADS, HEAD_DIM)
    y_sample = hs1[:ns * CHUNK].reshape(ns, CHUNK, D_MODEL)
    return (hp1, y_sample, kv(k_p), kv(v_p), u_p, c_p, s_p,
            kv(k_s[:ns]), kv(v_s[:ns]), u_s[:ns], c_s[:ns], s_s[:ns])
```

```python
import functools

import jax
import jax.numpy as jnp
import numpy as np
from jax import lax
from jax.experimental import pallas as pl
from jax.experimental.pallas import tpu as pltpu

F32 = jnp.float32
BF16 = jnp.bfloat16

D_MODEL = 1024
CHUNK = 64
N_META = 16
PADF = CHUNK - N_META
A_HEADS = 8
A_KV_HEADS = 2
HEAD_DIM = 64
WINDOW = 128
KV_W = A_KV_HEADS * HEAD_DIM
Q_W = A_HEADS * HEAD_DIM
BAND = WINDOW + CHUNK
B_WIDTH = 512
B_CONV = 3
C_WIDTH = 512
C_CONV = 31
D_HEADS = 4
D_KEY = 128
D_VAL = 128
D_WIDTH = D_HEADS * D_VAL
SUB = 16
FFN_HIDDEN = 2816
DEPTH = 2
ALPHA = (2 * DEPTH) ** 0.25
LN_EPS = 1e-5
RMS_EPS = 1e-6
NEG = -1e30

AB_IN = Q_W + 2 * KV_W + 3 * B_WIDTH
CD_IN = 2 * C_WIDTH + 4 * D_WIDTH

TILE = 512
FFN_BLOCK = 256
L0_FFN_STEPS_BETWEEN = (1,) * (2 * TILE // CHUNK)
L1_FFN_STEPS = (6, 2, 2, 2)
SUBLANES = 8
LANES = 128
CONV_ROWS = 32
C_HIST = C_CONV - 1
C_OFF = 32
B_OFF = 8
VMEM_LIMIT = 60 * 1024 * 1024


def _dot(a, b):
    return jnp.dot(a.astype(BF16), b.astype(BF16), preferred_element_type=F32)


def _dot_nt(a, b):
    return lax.dot_general(a.astype(BF16), b.astype(BF16), (((1,), (1,)), ((), ())),
                           preferred_element_type=F32)


def _dot_tn(a, b):
    return lax.dot_general(a.astype(BF16), b.astype(BF16), (((0,), (0,)), ((), ())),
                           preferred_element_type=F32)


def _sigmoid(x):
    return 1.0 / (1.0 + jnp.exp(-x))


def _silu(x):
    return x * _sigmoid(x)


def _layer_norm(x, g, b):
    mu = jnp.mean(x, axis=-1, keepdims=True)
    xc = x - mu
    var = jnp.mean(xc * xc, axis=-1, keepdims=True)
    return xc * lax.rsqrt(var + LN_EPS) * g + b


class _PostBlock:
    N_BLOCKS = FFN_HIDDEN // FFN_BLOCK
    N_STEPS = 2 + N_BLOCKS

    def __init__(self, x, mix, w_o_ref, ln_ref, wgu_ref, wd_ref):
        self.x, self.mix = x, mix
        self.w_o_ref, self.ln_ref, self.wgu_ref, self.wd_ref = w_o_ref, ln_ref, wgu_ref.at[0], wd_ref.at[0]
        self.done = 0
        self.h = self.hb = self.acc = self.act = None

    def _gate_up(self, j):
        lo = j * FFN_BLOCK
        gate = jnp.dot(self.hb, self.wgu_ref[:, lo:lo + FFN_BLOCK], preferred_element_type=F32)
        up = jnp.dot(self.hb, self.wgu_ref[:, FFN_HIDDEN + lo:FFN_HIDDEN + lo + FFN_BLOCK],
                     preferred_element_type=F32)
        return (_silu(gate) * up).astype(BF16)

    def _step(self, j):
        if j == 0:
            ln = self.ln_ref
            self.h = _layer_norm(ALPHA * self.x + _dot(self.mix, self.w_o_ref[...]), ln[0:1], ln[1:2])
            self.hb = self.h.astype(BF16)
            return
        prev = self.act
        self.act = self._gate_up(j - 1) if j - 1 < self.N_BLOCKS else None
        if prev is not None:
            lo = (j - 2) * FFN_BLOCK
            part = jnp.dot(prev, self.wd_ref[lo:lo + FFN_BLOCK, :], preferred_element_type=F32)
            self.acc = part if self.acc is None else self.acc + part

    def run(self, n):
        for _ in range(n):
            if self.done < self.N_STEPS:
                self._step(self.done)
                self.done += 1

    def result(self):
        self.run(self.N_STEPS)
        return _layer_norm(ALPHA * self.h + self.acc, self.ln_ref[2:3], self.ln_ref[3:4])


def _no_op():
    pass


def _attn_chunks(chunks, alibi_ref, sink_ref, between=_no_op):
    lane = lax.broadcasted_iota(jnp.int32, (BAND, KV_W), 1)
    low = lane < HEAD_DIM
    high = jnp.logical_not(low)

    def extend(band, swapped, g, e):
        src = band if g == e else swapped
        return jnp.where(low if e == 0 else high, src, 0.0).astype(BF16)

    scores = []
    for q, kband, _, kvbias in chunks:
        kswap = pltpu.roll(kband, HEAD_DIM, axis=1)
        for g in range(A_KV_HEADS):
            qg = jnp.concatenate([q[:, 256 * g:256 * g + 128], q[:, 256 * g + 128:256 * g + 256]],
                                 axis=0).astype(BF16)
            k2 = jnp.concatenate([extend(kband, kswap, g, e) for e in range(2)], axis=0)
            s = _dot_nt(k2, qg) + alibi_ref[g]
            for e in range(2):
                se = s[BAND * e:BAND * (e + 1)]
                scores.append(se if kvbias is None else se + kvbias)
        between()

    probs = []
    for idx, s in enumerate(scores):
        sink = sink_ref[idx % (2 * A_KV_HEADS)]
        m = jnp.maximum(jnp.max(s, axis=0, keepdims=True), sink)
        p = jnp.exp(s - m)
        den = jnp.sum(p, axis=0, keepdims=True) + jnp.exp(sink - m)
        probs.append((p * (1.0 / den)).astype(BF16))
        if idx % (2 * A_KV_HEADS) == 2 * A_KV_HEADS - 1:
            between()

    outs = []
    for ci, (_, _, vband, _) in enumerate(chunks):
        vswap = pltpu.roll(vband, HEAD_DIM, axis=1)
        parts = []
        for g in range(A_KV_HEADS):
            base = ci * 2 * A_KV_HEADS + 2 * g
            p2 = jnp.concatenate([probs[base], probs[base + 1]], axis=0)
            v2 = jnp.concatenate([extend(vband, vswap, g, e) for e in range(2)], axis=0)
            o = _dot_tn(p2, v2)
            parts += [o[0:CHUNK], o[CHUNK:2 * CHUNK]]
        outs.append(jnp.concatenate(parts, axis=1))
    return outs


def _ab_project(x, w_in_ref, b_in_ref):
    proj = _dot(x, w_in_ref[...]) + b_in_ref[...]
    q = proj[:, 0:Q_W] * (HEAD_DIM ** -0.5)
    k = proj[:, Q_W:Q_W + KV_W]
    v = proj[:, Q_W + KV_W:Q_W + 2 * KV_W]
    o = Q_W + 2 * KV_W
    bg = proj[:, o:o + B_WIDTH]
    u = proj[:, o + B_WIDTH:o + 2 * B_WIDTH] * proj[:, o + 2 * B_WIDTH:o + 3 * B_WIDTH]
    return q, k, v, bg, u


def _short_conv(u_ref, base, rows, w_ref):
    out = None
    for j in range(B_CONV):
        term = u_ref[pl.ds(base - (B_CONV - 1) + j, rows), :] * w_ref[j:j + 1, :]
        out = term if out is None else out + term
    return out


def _l0_short_kernel(x_ref, rowmask_ref, k0_ref, v0_ref, u0_ref, kvb_ref, alibi_ref, sink_ref,
                     convw_ref, w_in_ref, b_in_ref, w_o_ref, ln_ref, wgu_ref, wd_ref,
                     h_ref, kout_ref, vout_ref, uout_ref,
                     u_s, mix_s):
    ns = k0_ref.shape[0]
    x = x_ref[...]
    q, k, v, bg, u = _ab_project(x, w_in_ref, b_in_ref)
    u = u * rowmask_ref[...]

    chunks = []
    for s in range(ns):
        r0 = s * CHUNK
        kband = jnp.concatenate([k0_ref[s], k[r0:r0 + CHUNK]], axis=0)
        vband = jnp.concatenate([v0_ref[s], v[r0:r0 + CHUNK]], axis=0)
        kout_ref[s] = kband[CHUNK:]
        vout_ref[s] = vband[CHUNK:]
        chunks.append((q[r0:r0 + CHUNK], kband, vband, kvb_ref[...] if s == ns - 1 else None))
    for s, att in enumerate(_attn_chunks(chunks, alibi_ref, sink_ref)):
        mix_s[s * CHUNK:(s + 1) * CHUNK, 0:Q_W] = att

    for s in range(ns):
        r0 = s * CHUNK
        us = u_s.at[s]
        us[B_OFF - 2:B_OFF, :] = u0_ref[s]
        us[B_OFF:B_OFF + CHUNK, :] = u[r0:r0 + CHUNK]
        cb = _short_conv(us, B_OFF, CHUNK, convw_ref)
        mix_s[r0:r0 + CHUNK, Q_W:] = bg[r0:r0 + CHUNK] * cb
        uout_ref[s] = u[r0 + CHUNK - 2:r0 + CHUNK]

    h_ref[...] = _PostBlock(x, mix_s[...], w_o_ref, ln_ref, wgu_ref, wd_ref).result()


def _tile_position(nt):
    i = pl.program_id(0)
    n = pl.num_programs(0) - 1
    return i, n, lax.rem(jnp.minimum(i, n - 1), nt)


def _skewed_step(i, n, step):
    pl.when(i == 0)(lambda: step(True, False))
    pl.when(i == n)(lambda: step(False, True))
    pl.when(jnp.logical_and(i > 0, i < n))(lambda: step(True, True))


def _l0_main_kernel(nt, x_ref, xprev_ref, k0_ref, v0_ref, u0_ref, kvb_ref, alibi_ref, sink_ref,
                    convw_ref, w_in_ref, b_in_ref, w_o_ref, ln_ref, wgu_ref, wd_ref,
                    h_ref, kout_ref, vout_ref, uout_ref,
                    k_s, v_s, u_s, mix_s):
    i, n, t = _tile_position(nt)
    tm = x_ref.shape[1]

    @pl.when(t == 0)
    def _():
        k_s[0:WINDOW, :] = k0_ref[0]
        v_s[0:WINDOW, :] = v0_ref[0]
        u_s[B_OFF - 2:B_OFF, :] = u0_ref[0]

    def step(mixers, residual):
        post = None
        if residual:
            post = _PostBlock(xprev_ref[0], mix_s[...], w_o_ref, ln_ref, wgu_ref, wd_ref)
            post.run(1)
        if mixers:
            q, k, v, bg, u = _ab_project(x_ref[0], w_in_ref, b_in_ref)
            k_s[WINDOW:, :] = k
            v_s[WINDOW:, :] = v
            u_s[B_OFF:, :] = u
            mix_s[:, Q_W:] = bg * _short_conv(u_s, B_OFF, tm, convw_ref)

            first = t == 0
            chunks = []
            for c in range(tm // CHUNK):
                r0 = c * CHUNK
                kvbias = jnp.where(first, kvb_ref[c], 0.0) if c < 2 else None
                chunks.append((q[r0:r0 + CHUNK], k_s[r0:r0 + BAND, :], v_s[r0:r0 + BAND, :], kvbias))
            plan = iter(L0_FFN_STEPS_BETWEEN)
            between = (lambda: post.run(next(plan, 0))) if residual else _no_op
            for c, att in enumerate(_attn_chunks(chunks, alibi_ref, sink_ref, between)):
                mix_s[c * CHUNK:(c + 1) * CHUNK, 0:Q_W] = att
        if residual:
            h_ref[0] = post.result()
        if mixers:
            k_s[0:WINDOW, :] = k_s[tm:tm + WINDOW, :]
            v_s[0:WINDOW, :] = v_s[tm:tm + WINDOW, :]
            u_s[B_OFF - 2:B_OFF, :] = u_s[B_OFF + tm - 2:B_OFF + tm, :]

    _skewed_step(i, n, step)

    @pl.when(jnp.logical_and(t == nt - 1, i < n))
    def _():
        kout_ref[0] = k_s[0:WINDOW, :]
        vout_ref[0] = v_s[0:WINDOW, :]
        uout_ref[0] = u_s[B_OFF - 2:B_OFF, :]


def _c_project(x, rowmask, w_in_ref, b_in_ref):
    w = C_WIDTH
    proj = _dot(x, w_in_ref[:, 0:2 * w]) + b_in_ref[:, 0:2 * w]
    u = proj[:, 0:w] * _sigmoid(proj[:, w:2 * w])
    return u if rowmask is None else u * rowmask


def _d_project(x, rowmask, w_in_ref, b_in_ref, lb_ref):
    w = D_WIDTH
    proj = _dot(x, w_in_ref[:, 2 * C_WIDTH:]) + b_in_ref[:, 2 * C_WIDTH:]
    q = proj[:, 0:w]
    d0 = lb_ref[0:1]
    d1 = lb_ref[1:2]
    mx = jnp.maximum(d0, d1)
    e0 = jnp.exp(d0 - mx)
    e1 = jnp.exp(d1 - mx)
    p0 = e0 / (e0 + e1)
    p1 = e1 / (e0 + e1)
    lb = (p0 + p1) - p0
    forget = lb + (1.0 - lb) * _sigmoid(proj[:, w:2 * w])
    kk = 1.0 - forget
    lf = jnp.log(forget)
    v = proj[:, 2 * w:3 * w]
    g = proj[:, 3 * w:4 * w]
    if rowmask is not None:
        q = q * rowmask
        kk = kk * rowmask
        v = v * rowmask
        lf = lf * rowmask
    return q, kk, v, lf, g


def _conformer_conv(c_ref, base, rows, convw_ref, cvec_ref, out_ref, out_base):
    for r0 in range(0, rows, CONV_ROWS):
        cols = []
        for c0 in range(0, C_WIDTH, LANES):
            lanes = slice(c0, c0 + LANES)
            win = c_ref[base - C_OFF + r0:base + r0 + CONV_ROWS, lanes]
            acc = None
            for phase in range(SUBLANES):
                shifted = win if phase == 0 else pltpu.roll(win, CONV_ROWS + C_OFF - phase, axis=0)
                for j in range(C_CONV):
                    off = C_OFF - C_HIST + j
                    if off % SUBLANES != phase:
                        continue
                    lo = off - phase
                    term = shifted[lo:lo + CONV_ROWS] * convw_ref[j:j + 1, lanes]
                    acc = term if acc is None else acc + term
            cols.append(acc)
        acc = jnp.concatenate(cols, axis=1)
        y = _layer_norm(acc + cvec_ref[0:1], cvec_ref[1:2], cvec_ref[2:3])
        out_ref[out_base + r0:out_base + r0 + CONV_ROWS, 0:C_WIDTH] = _silu(y)


def _hgrn_chunks(chunks, init_states, between=_no_op):
    row = lax.broadcasted_iota(jnp.int32, (CHUNK, CHUNK), 0)
    col = lax.broadcasted_iota(jnp.int32, (CHUNK, CHUNK), 1)
    tril = col <= row
    ltri = jnp.where(tril, 1.0, 0.0).astype(BF16)
    nsub = CHUNK // SUB
    rsub = jnp.right_shift(lax.broadcasted_iota(jnp.int32, (CHUNK, D_KEY), 0), SUB.bit_length() - 1)
    heads = [slice(D_KEY * h, D_KEY * (h + 1)) for h in range(D_HEADS)]

    cums = []
    for _, _, _, lf in chunks:
        hi = lf.astype(BF16)
        r1 = lf - hi.astype(F32)
        mid = r1.astype(BF16)
        lo = (r1 - mid.astype(F32)).astype(BF16)
        cums.append(jnp.dot(ltri, hi, preferred_element_type=F32)
                    + jnp.dot(ltri, mid, preferred_element_type=F32)
                    + jnp.dot(ltri, lo, preferred_element_type=F32))

    between()
    pre = []
    for (q, kk, v, _), cum in zip(chunks, cums):
        tot = cum[CHUNK - 1:CHUNK]
        bases = [jnp.zeros((1, D_WIDTH), F32)] + [cum[SUB * i - 1:SUB * i] for i in range(1, nsub)]
        bsel = jnp.concatenate([jnp.broadcast_to(b, (SUB, D_WIDTH)) for b in bases], axis=0)
        qe = (q * jnp.exp(cum)).astype(BF16)
        qd = q * jnp.exp(cum - bsel)
        kdec = (kk * jnp.exp(tot - cum)).astype(BF16)
        qblks, kdcs = [], []
        for sl in heads:
            qd_h = qd[:, sl]
            qblks.append(jnp.concatenate([jnp.where(rsub == i, qd_h, 0.0) for i in range(nsub)],
                                         axis=1).astype(BF16))
            kds = []
            for i in range(nsub):
                n = SUB * (i + 1)
                kd = kk[0:n, sl] * jnp.exp(bases[i][:, sl] - cum[0:n, sl])
                if n < CHUNK:
                    kd = jnp.concatenate([kd, jnp.zeros((CHUNK - n, D_KEY), F32)], axis=0)
                kds.append(kd)
            kdcs.append(jnp.concatenate(kds, axis=1).astype(BF16))
        pre.append((qe, kdec, v.astype(BF16), jnp.exp(tot), qblks, kdcs))

    atts, incs = [], []
    for qe, kdec, vb, etot, qblks, kdcs in pre:
        atts.append([jnp.where(tril, _dot_nt(qblks[h], kdcs[h]), 0.0).astype(BF16) for h in range(D_HEADS)])
        incs.append([_dot_tn(vb[:, sl], kdec[:, sl]) for sl in heads])

    between()
    intra = [[_dot(atts[c][h], pre[c][2][:, heads[h]]) for h in range(D_HEADS)] for c in range(len(chunks))]

    between()
    outs, ends = [], []
    state = None
    for c, (qe, kdec, vb, etot, _, _) in enumerate(pre):
        if init_states[c] is not None:
            state = init_states[c]
        o = [_dot_nt(qe[:, heads[h]], state[h]) + intra[c][h] for h in range(D_HEADS)]
        state = [etot[:, heads[h]] * state[h] + incs[c][h] for h in range(D_HEADS)]
        outs.append(jnp.concatenate(o, axis=1))
        ends.append(state)
    return outs, ends


def _hgrn_readout(o, g, norm_g):
    parts = []
    for h in range(D_HEADS):
        oh = o[:, D_VAL * h:D_VAL * (h + 1)]
        parts.append(oh * lax.rsqrt(jnp.mean(oh * oh, axis=-1, keepdims=True) + RMS_EPS))
    return jnp.concatenate(parts, axis=1) * norm_g * _silu(g)


def _l1_short_kernel(x_ref, rowmask_ref, c0_ref, s0_ref, convw_ref, cvec_ref, lb_ref, normg_ref,
                     w_in_ref, b_in_ref, w_o_ref, ln_ref, wgu_ref, wd_ref,
                     h_ref, cout_ref, sout_ref,
                     c_s, mix_s):
    ns = c0_ref.shape[0]
    x = x_ref[...]
    rowmask = rowmask_ref[...]
    u = _c_project(x, rowmask, w_in_ref, b_in_ref)
    q, kk, v, lf, g = _d_project(x, rowmask, w_in_ref, b_in_ref, lb_ref)

    for s in range(ns):
        r0 = s * CHUNK
        cs = c_s.at[s]
        cs[C_OFF - C_HIST:C_OFF, :] = c0_ref[s]
        cs[C_OFF:C_OFF + CHUNK, :] = u[r0:r0 + CHUNK]
        _conformer_conv(cs, C_OFF, CHUNK, convw_ref, cvec_ref, mix_s, r0)
        cout_ref[s] = cs[C_OFF + CHUNK - C_HIST:C_OFF + CHUNK, :]

    rows = [slice(s * CHUNK, (s + 1) * CHUNK) for s in range(ns)]
    outs, ends = _hgrn_chunks([(q[r], kk[r], v[r], lf[r]) for r in rows],
                              [[s0_ref[s, h].T for h in range(D_HEADS)] for s in range(ns)])
    for s in range(ns):
        mix_s[rows[s], C_WIDTH:] = _hgrn_readout(outs[s], g[rows[s]], normg_ref[...])
        for h in range(D_HEADS):
            sout_ref[s, h] = ends[s][h].T

    h_ref[...] = _PostBlock(x, mix_s[...], w_o_ref, ln_ref, wgu_ref, wd_ref).result()


def _l1_main_kernel(nt, x_ref, xprev_ref, c0_ref, s0_ref, convw_ref, cvec_ref, lb_ref, normg_ref,
                    w_in_ref, b_in_ref, w_o_ref, ln_ref, wgu_ref, wd_ref,
                    h_ref, cout_ref, sout_ref,
                    c_s, st_s, mix_s):
    i, n, t = _tile_position(nt)
    tm = x_ref.shape[1]

    @pl.when(t == 0)
    def _():
        c_s[C_OFF - C_HIST:C_OFF, :] = c0_ref[0]
        for h in range(D_HEADS):
            st_s[h] = s0_ref[0, h].T

    def step(mixers, residual):
        post = None
        if residual:
            post = _PostBlock(xprev_ref[0], mix_s[...], w_o_ref, ln_ref, wgu_ref, wd_ref)
            post.run(1)
        if mixers:
            x = x_ref[0]
            plan = iter(L1_FFN_STEPS)
            between = (lambda: post.run(next(plan, 0))) if residual else _no_op
            c_s[C_OFF:, :] = _c_project(x, None, w_in_ref, b_in_ref)
            q, kk, v, lf, g = _d_project(x, None, w_in_ref, b_in_ref, lb_ref)
            between()
            _conformer_conv(c_s, C_OFF, tm, convw_ref, cvec_ref, mix_s, 0)
            c_s[C_OFF - C_HIST:C_OFF, :] = c_s[C_OFF + tm - C_HIST:C_OFF + tm, :]

            nc = tm // CHUNK
            rows = [slice(c * CHUNK, (c + 1) * CHUNK) for c in range(nc)]
            outs, ends = _hgrn_chunks([(q[r], kk[r], v[r], lf[r]) for r in rows],
                                      [[st_s[h] for h in range(D_HEADS)]] + [None] * (nc - 1), between)
            for c in range(nc):
                mix_s[rows[c], C_WIDTH:] = _hgrn_readout(outs[c], g[rows[c]], normg_ref[...])
            for h in range(D_HEADS):
                st_s[h] = ends[-1][h]
        if residual:
            h_ref[0] = post.result()

    _skewed_step(i, n, step)

    @pl.when(jnp.logical_and(t == nt - 1, i < n))
    def _():
        cout_ref[0] = c_s[C_OFF - C_HIST:C_OFF, :]
        for h in range(D_HEADS):
            sout_ref[0, h] = st_s[h].T


def _whole(shape):
    zeros = (0,) * len(shape)
    return pl.BlockSpec(shape, lambda *_: zeros, pipeline_mode=pl.Buffered(1))


def _weight_specs(weights, layer):
    def stacked(a):
        zeros = (0,) * (a.ndim - 1)
        return pl.BlockSpec((1,) + a.shape[1:], lambda *_: (layer,) + zeros, pipeline_mode=pl.Buffered(1))

    return [_whole(a.shape) for a in weights[:-2]] + [stacked(a) for a in weights[-2:]]


def _whole_out(shape):
    zeros = (0,) * len(shape)
    return pl.BlockSpec(shape, lambda *_: zeros)


def _params(n_axes):
    return pltpu.CompilerParams(dimension_semantics=("arbitrary",) * n_axes,
                                vmem_limit_bytes=VMEM_LIMIT)


def _l0_short(x, rowmask, k0, v0, u0, kvb, consts, weights, layer):
    rows, ns = x.shape[0], k0.shape[0]
    ins = (x, rowmask, k0, v0, u0, kvb) + consts
    return pl.pallas_call(
        _l0_short_kernel,
        grid=(1,),
        in_specs=[_whole(a.shape) for a in ins] + _weight_specs(weights, layer),
        out_specs=[_whole_out((rows, D_MODEL)), _whole_out((ns, WINDOW, KV_W)),
                   _whole_out((ns, WINDOW, KV_W)), _whole_out((ns, B_CONV - 1, B_WIDTH))],
        out_shape=[jax.ShapeDtypeStruct((rows, D_MODEL), F32),
                   jax.ShapeDtypeStruct((ns, WINDOW, KV_W), F32),
                   jax.ShapeDtypeStruct((ns, WINDOW, KV_W), F32),
                   jax.ShapeDtypeStruct((ns, B_CONV - 1, B_WIDTH), F32)],
        scratch_shapes=[pltpu.VMEM((ns, B_OFF + CHUNK, B_WIDTH), F32), pltpu.VMEM((rows, D_MODEL), F32)],
        compiler_params=_params(1),
        name="l0_short",
    )(*ins, *weights)


class _SkewedTiles:
    def __init__(self, nb, nt):
        self.nt, self.n = nt, nb * nt

    def _cur(self, i):
        return jnp.minimum(i, self.n - 1)

    def tile(self):
        return pl.BlockSpec((1, TILE, D_MODEL), lambda i: (self._cur(i) // self.nt, self._cur(i) % self.nt, 0))

    def prev_tile(self):
        prev = lambda i: jnp.maximum(i - 1, 0)
        return pl.BlockSpec((1, TILE, D_MODEL), lambda i: (prev(i) // self.nt, prev(i) % self.nt, 0))

    def fixed(self, shape, idx):
        zeros = (0,) * len(shape)
        return pl.BlockSpec((1,) + shape, lambda i: (idx,) + zeros)

    def per_prompt(self, shape):
        zeros = (0,) * len(shape)
        return pl.BlockSpec((1,) + shape, lambda i: (self._cur(i) // self.nt,) + zeros)


def _l0_main(x, k0, v0, u0, kvb, consts, weights, layer, meta_idx):
    nb, seq, _ = x.shape
    nt = seq // TILE
    sk = _SkewedTiles(nb, nt)
    ins = (x, x, k0, v0, u0, kvb) + consts + weights
    in_specs = [sk.tile(), sk.prev_tile(),
                sk.fixed((WINDOW, KV_W), meta_idx), sk.fixed((WINDOW, KV_W), meta_idx),
                sk.fixed((B_CONV - 1, B_WIDTH), meta_idx),
                _whole(kvb.shape)] + [_whole(a.shape) for a in consts] + _weight_specs(weights, layer)
    return pl.pallas_call(
        functools.partial(_l0_main_kernel, nt),
        grid=(nb * nt + 1,),
        in_specs=in_specs,
        out_specs=[sk.prev_tile(), sk.per_prompt((WINDOW, KV_W)), sk.per_prompt((WINDOW, KV_W)),
                   sk.per_prompt((B_CONV - 1, B_WIDTH))],
        out_shape=[jax.ShapeDtypeStruct((nb, seq, D_MODEL), F32),
                   jax.ShapeDtypeStruct((nb, WINDOW, KV_W), F32),
                   jax.ShapeDtypeStruct((nb, WINDOW, KV_W), F32),
                   jax.ShapeDtypeStruct((nb, B_CONV - 1, B_WIDTH), F32)],
        scratch_shapes=[pltpu.VMEM((WINDOW + TILE, KV_W), F32),
                        pltpu.VMEM((WINDOW + TILE, KV_W), F32), pltpu.VMEM((B_OFF + TILE, B_WIDTH), F32),
                        pltpu.VMEM((TILE, D_MODEL), F32)],
        compiler_params=_params(1),
        name="l0_main",
    )(*ins)


def _l1_short(x, rowmask, c0, s0, consts, weights, layer):
    rows, ns = x.shape[0], c0.shape[0]
    ins = (x, rowmask, c0, s0) + consts
    return pl.pallas_call(
        _l1_short_kernel,
        grid=(1,),
        in_specs=[_whole(a.shape) for a in ins] + _weight_specs(weights, layer),
        out_specs=[_whole_out((rows, D_MODEL)), _whole_out((ns, C_HIST, C_WIDTH)),
                   _whole_out((ns, D_HEADS, D_KEY, D_VAL))],
        out_shape=[jax.ShapeDtypeStruct((rows, D_MODEL), F32),
                   jax.ShapeDtypeStruct((ns, C_HIST, C_WIDTH), F32),
                   jax.ShapeDtypeStruct((ns, D_HEADS, D_KEY, D_VAL), F32)],
        scratch_shapes=[pltpu.VMEM((ns, C_OFF + CHUNK, C_WIDTH), F32), pltpu.VMEM((rows, D_MODEL), F32)],
        compiler_params=_params(1),
        name="l1_short",
    )(*ins, *weights)


def _l1_main(x, c0, s0, consts, weights, layer, meta_idx):
    nb, seq, _ = x.shape
    nt = seq // TILE
    sk = _SkewedTiles(nb, nt)
    ins = (x, x, c0, s0) + consts + weights
    in_specs = [sk.tile(), sk.prev_tile(), sk.fixed((C_HIST, C_WIDTH), meta_idx),
                sk.fixed((D_HEADS, D_KEY, D_VAL), meta_idx),
                ] + [_whole(a.shape) for a in consts] + _weight_specs(weights, layer)
    return pl.pallas_call(
        functools.partial(_l1_main_kernel, nt),
        grid=(nb * nt + 1,),
        in_specs=in_specs,
        out_specs=[sk.prev_tile(), sk.per_prompt((C_HIST, C_WIDTH)),
                   sk.per_prompt((D_HEADS, D_KEY, D_VAL))],
        out_shape=[jax.ShapeDtypeStruct((nb, seq, D_MODEL), F32),
                   jax.ShapeDtypeStruct((nb, C_HIST, C_WIDTH), F32),
                   jax.ShapeDtypeStruct((nb, D_HEADS, D_KEY, D_VAL), F32)],
        scratch_shapes=[pltpu.VMEM((C_OFF + TILE, C_WIDTH), F32), pltpu.VMEM((D_HEADS, D_VAL, D_KEY), F32),
                        pltpu.VMEM((TILE, D_MODEL), F32)],
        compiler_params=_params(1),
        name="l1_main",
    )(*ins)


def _alibi_tables():
    slopes = np.exp2(-8.0 * np.arange(1, A_HEADS + 1, dtype=np.float32) / A_HEADS).astype(np.float32)
    i = np.arange(CHUNK, dtype=np.float32)[None, :]
    j = np.arange(BAND, dtype=np.float32)[:, None]
    dist = np.abs(WINDOW + i - j).astype(np.float32)
    bias = np.zeros((A_KV_HEADS, 2 * BAND, 2 * CHUNK), np.float32)
    heads = np.zeros((2 * A_KV_HEADS, 2 * CHUNK), np.int32)
    for g in range(A_KV_HEADS):
        for e in range(2):
            for half, head in enumerate((4 * g + e, 4 * g + 2 + e)):
                cols = slice(CHUNK * half, CHUNK * (half + 1))
                bias[g, BAND * e:BAND * (e + 1), cols] = -slopes[head] * dist
                heads[2 * g + e, cols] = head
    return bias, heads


def kernel(x_prompt, x_sample, cache_k_a, cache_v_a, state_conv_b, state_conv_c, state_hgrn, meta_tokens,
           ab_w_in, ab_b_in, a_sinks, b_conv_w, ab_w_o, cd_w_in, cd_b_in, c_conv_w, c_conv_b, c_ln_g,
           c_ln_b, d_lower_bounds, d_norm_g, cd_w_o, ln1_g, ln1_b, ln2_g, ln2_b, ffn_w_gu, ffn_w_down):
    nb = x_prompt.shape[0]
    ns = x_sample.shape[0]
    assert x_sample.shape[1] == CHUNK and cache_k_a.shape[1] == WINDOW
    assert x_prompt.shape[1] % TILE == 0
    meta_idx = ns
    n_short = ns + 1
    rows = n_short * CHUNK

    xs = jnp.concatenate([x_sample.reshape(ns * CHUNK, D_MODEL),
                          jnp.zeros((PADF, D_MODEL), F32), meta_tokens.astype(F32)], axis=0)
    rowmask = np.ones((rows, 1), np.float32)
    rowmask[ns * CHUNK:ns * CHUNK + PADF] = 0.0
    rowmask = jnp.asarray(rowmask)

    kvb_short = np.zeros((BAND, 2 * CHUNK), np.float32)
    kvb_short[:WINDOW + PADF] = NEG
    kvb_main = np.zeros((2, BAND, 2 * CHUNK), np.float32)
    kvb_main[0, :CHUNK + PADF] = NEG
    kvb_main[1, :PADF] = NEG

    alibi, sink_heads = _alibi_tables()
    sinks = a_sinks.astype(F32)[jnp.asarray(sink_heads)][:, None, :]
    a_consts = (jnp.asarray(alibi), sinks, b_conv_w.astype(F32))

    zpad = lambda a: jnp.concatenate([a, jnp.zeros((1,) + a.shape[1:], a.dtype)], axis=0)
    k0 = zpad(cache_k_a.reshape(ns, WINDOW, KV_W))
    v0 = zpad(cache_v_a.reshape(ns, WINDOW, KV_W))
    u0 = zpad(state_conv_b)
    c0 = zpad(state_conv_c)
    s0 = zpad(state_hgrn)

    w_gu = ffn_w_gu.astype(BF16)
    w_down = ffn_w_down.astype(BF16)

    def layer_weights(l, w_in, b_in, w_o):
        ln = jnp.stack([ln1_g[l], ln1_b[l], ln2_g[l], ln2_b[l]]).astype(F32)
        return (w_in.astype(BF16), b_in.astype(F32)[None, :], w_o.astype(BF16), ln, w_gu, w_down)

    w0 = layer_weights(0, ab_w_in, ab_b_in, ab_w_o)
    w1 = layer_weights(1, cd_w_in, cd_b_in, cd_w_o)

    hs0, k_s, v_s, u_s = _l0_short(xs, rowmask, k0, v0, u0, jnp.asarray(kvb_short), a_consts, w0, 0)
    hp0, k_p, v_p, u_p = _l0_main(x_prompt, k_s, v_s, u_s, jnp.asarray(kvb_main), a_consts, w0, 0, meta_idx)

    cvec = jnp.stack([c_conv_b, c_ln_g, c_ln_b]).astype(F32)
    c_consts = (c_conv_w.astype(F32), cvec, d_lower_bounds.astype(F32), d_norm_g.astype(F32)[None, :])
    hs1, c_s, s_s = _l1_short(hs0, rowmask, c0, s0, c_consts, w1, 1)
    hp1, c_p, s_p = _l1_main(hp0, c_s, s_s, c_consts, w1, 1, meta_idx)

    kv = lambda a: a.reshape(a.shape[0], WINDOW, A_KV_HEADS, HEAD_DIM)
    y_sample = hs1[:ns * CHUNK].reshape(ns, CHUNK, D_MODEL)
    return (hp1, y_sample, kv(k_p), kv(v_p), u_p, c_p, s_p,
            kv(k_s[:ns]), kv(v_s[:ns]), u_s[:ns], c_s[:ns], s_s[:ns])
```

```python
import functools

import jax
import jax.numpy as jnp
import numpy as np
from jax import lax
from jax.experimental import pallas as pl
from jax.experimental.pallas import tpu as pltpu

F32 = jnp.float32
BF16 = jnp.bfloat16

D_MODEL = 1024
CHUNK = 64
N_META = 16
PADF = CHUNK - N_META
A_HEADS = 8
A_KV_HEADS = 2
HEAD_DIM = 64
WINDOW = 128
KV_W = A_KV_HEADS * HEAD_DIM
Q_W = A_HEADS * HEAD_DIM
BAND = WINDOW + CHUNK
B_WIDTH = 512
B_CONV = 3
C_WIDTH = 512
C_CONV = 31
D_HEADS = 4
D_KEY = 128
D_VAL = 128
D_WIDTH = D_HEADS * D_VAL
SUB = 16
FFN_HIDDEN = 2816
DEPTH = 2
ALPHA = (2 * DEPTH) ** 0.25
LN_EPS = 1e-5
RMS_EPS = 1e-6
NEG = -1e30

AB_IN = Q_W + 2 * KV_W + 3 * B_WIDTH
CD_IN = 2 * C_WIDTH + 4 * D_WIDTH

TILE = 512
FFN_BLOCK = 256
POST_ROW_GROUPS = 2
L0_FFN_STEPS_BETWEEN = (POST_ROW_GROUPS,) * (2 * TILE // CHUNK)
L1_FFN_STEPS = tuple(POST_ROW_GROUPS * n for n in (6, 2, 2, 2))
SUBLANES = 8
LANES = 128
CONV_ROWS = 32
C_HIST = C_CONV - 1
C_OFF = 32
B_OFF = 8
VMEM_LIMIT = 60 * 1024 * 1024


def _dot(a, b):
    return jnp.dot(a.astype(BF16), b.astype(BF16), preferred_element_type=F32)


def _dot_nt(a, b):
    return lax.dot_general(a.astype(BF16), b.astype(BF16), (((1,), (1,)), ((), ())),
                           preferred_element_type=F32)


def _dot_tn(a, b):
    return lax.dot_general(a.astype(BF16), b.astype(BF16), (((0,), (0,)), ((), ())),
                           preferred_element_type=F32)


def _sigmoid(x):
    return 1.0 / (1.0 + jnp.exp(-x))


def _silu(x):
    return x * _sigmoid(x)


def _layer_norm(x, g, b):
    mu = jnp.mean(x, axis=-1, keepdims=True)
    xc = x - mu
    var = jnp.mean(xc * xc, axis=-1, keepdims=True)
    return xc * lax.rsqrt(var + LN_EPS) * g + b


class _PostBlock:
    N_BLOCKS = FFN_HIDDEN // FFN_BLOCK
    N_STEPS = 2 + N_BLOCKS

    def __init__(self, x, mix, w_o_ref, ln_ref, wgu_ref, wd_ref):
        self.x, self.mix = x, mix
        self.w_o_ref, self.ln_ref, self.wgu_ref, self.wd_ref = w_o_ref, ln_ref, wgu_ref.at[0], wd_ref.at[0]
        self.done = 0
        self.h = self.hb = self.acc = self.act = None

    def _gate_up(self, j):
        lo = j * FFN_BLOCK
        gate = jnp.dot(self.hb, self.wgu_ref[:, lo:lo + FFN_BLOCK], preferred_element_type=F32)
        up = jnp.dot(self.hb, self.wgu_ref[:, FFN_HIDDEN + lo:FFN_HIDDEN + lo + FFN_BLOCK],
                     preferred_element_type=F32)
        return (_silu(gate) * up).astype(BF16)

    def _step(self, j):
        if j == 0:
            ln = self.ln_ref
            self.h = _layer_norm(ALPHA * self.x + _dot(self.mix, self.w_o_ref[...]), ln[0:1], ln[1:2])
            self.hb = self.h.astype(BF16)
            return
        prev = self.act
        self.act = self._gate_up(j - 1) if j - 1 < self.N_BLOCKS else None
        if prev is not None:
            lo = (j - 2) * FFN_BLOCK
            part = jnp.dot(prev, self.wd_ref[lo:lo + FFN_BLOCK, :], preferred_element_type=F32)
            self.acc = part if self.acc is None else self.acc + part

    def run(self, n):
        for _ in range(n):
            if self.done < self.N_STEPS:
                self._step(self.done)
                self.done += 1

    def result(self):
        self.run(self.N_STEPS)
        return _layer_norm(ALPHA * self.h + self.acc, self.ln_ref[2:3], self.ln_ref[3:4])


class _SplitPostBlock:
    def __init__(self, x, mix, w_o_ref, ln_ref, wgu_ref, wd_ref, groups):
        rows = x.shape[0] // groups
        self.parts = [_PostBlock(x[g * rows:(g + 1) * rows], mix[g * rows:(g + 1) * rows],
                                 w_o_ref, ln_ref, wgu_ref, wd_ref) for g in range(groups)]
        self.outs = []

    def run(self, n):
        for _ in range(n):
            if len(self.outs) == len(self.parts):
                return
            part = self.parts[len(self.outs)]
            part.run(1)
            if part.done == part.N_STEPS:
                self.outs.append(part.result())

    def result(self):
        self.run(sum(part.N_STEPS for part in self.parts))
        return jnp.concatenate(self.outs, axis=0)


def _no_op():
    pass


def _attn_chunks(chunks, alibi_ref, sink_ref, between=_no_op):
    lane = lax.broadcasted_iota(jnp.int32, (BAND, KV_W), 1)
    low = lane < HEAD_DIM
    high = jnp.logical_not(low)
    sink_rows = [jnp.where(low[0:1], sink_ref[4 * g + e], sink_ref[4 * g + 2 + e])
                 for g in range(A_KV_HEADS) for e in range(2)]

    def extend(band, swapped, g, e):
        src = band if g == e else swapped
        return jnp.where(low if e == 0 else high, src, 0.0).astype(BF16)

    scores = []
    for q, kband, _, kvbias in chunks:
        kswap = pltpu.roll(kband, HEAD_DIM, axis=1)
        for g in range(A_KV_HEADS):
            qg = jnp.concatenate([q[:, 256 * g:256 * g + 128], q[:, 256 * g + 128:256 * g + 256]],
                                 axis=0).astype(BF16)
            k2 = jnp.concatenate([extend(kband, kswap, g, e) for e in range(2)], axis=0)
            s = _dot_nt(k2, qg) + alibi_ref[g]
            for e in range(2):
                se = s[BAND * e:BAND * (e + 1)]
                scores.append(se if kvbias is None else se + kvbias)
        between()

    probs = []
    for idx, s in enumerate(scores):
        sink = sink_rows[idx % (2 * A_KV_HEADS)]
        m = jnp.maximum(jnp.max(s, axis=0, keepdims=True), sink)
        p = jnp.exp(s - m)
        den = jnp.sum(p, axis=0, keepdims=True) + jnp.exp(sink - m)
        probs.append((p * (1.0 / den)).astype(BF16))
        if idx % (2 * A_KV_HEADS) == 2 * A_KV_HEADS - 1:
            between()

    outs = []
    for ci, (_, _, vband, _) in enumerate(chunks):
        vswap = pltpu.roll(vband, HEAD_DIM, axis=1)
        parts = []
        for g in range(A_KV_HEADS):
            base = ci * 2 * A_KV_HEADS + 2 * g
            p2 = jnp.concatenate([probs[base], probs[base + 1]], axis=0)
            v2 = jnp.concatenate([extend(vband, vswap, g, e) for e in range(2)], axis=0)
            o = _dot_tn(p2, v2)
            parts += [o[0:CHUNK], o[CHUNK:2 * CHUNK]]
        outs.append(jnp.concatenate(parts, axis=1))
    return outs


def _ab_project(x, w_in_ref, b_in_ref):
    proj = _dot(x, w_in_ref[...]) + b_in_ref[...]
    q = proj[:, 0:Q_W] * (HEAD_DIM ** -0.5)
    k = proj[:, Q_W:Q_W + KV_W]
    v = proj[:, Q_W + KV_W:Q_W + 2 * KV_W]
    o = Q_W + 2 * KV_W
    bg = proj[:, o:o + B_WIDTH]
    u = proj[:, o + B_WIDTH:o + 2 * B_WIDTH] * proj[:, o + 2 * B_WIDTH:o + 3 * B_WIDTH]
    return q, k, v, bg, u


def _short_conv(u_ref, base, rows, w_ref):
    out = None
    for j in range(B_CONV):
        term = u_ref[pl.ds(base - (B_CONV - 1) + j, rows), :] * w_ref[j:j + 1, :]
        out = term if out is None else out + term
    return out


def _l0_short_kernel(x_ref, rowmask_ref, k0_ref, v0_ref, u0_ref, kvb_ref, alibi_ref, sink_ref,
                     convw_ref, w_in_ref, b_in_ref, w_o_ref, ln_ref, wgu_ref, wd_ref,
                     h_ref, kout_ref, vout_ref, uout_ref,
                     u_s, mix_s):
    ns = k0_ref.shape[0] + 1
    x = x_ref[...]
    q, k, v, bg, u = _ab_project(x, w_in_ref, b_in_ref)
    u = u * rowmask_ref[...]

    chunks = []
    for s in range(ns):
        r0 = s * CHUNK
        meta = s == ns - 1
        empty = jnp.zeros((WINDOW, KV_W), F32)
        kband = jnp.concatenate([empty if meta else k0_ref[s], k[r0:r0 + CHUNK]], axis=0)
        vband = jnp.concatenate([empty if meta else v0_ref[s], v[r0:r0 + CHUNK]], axis=0)
        kout_ref[s] = kband[CHUNK:]
        vout_ref[s] = vband[CHUNK:]
        chunks.append((q[r0:r0 + CHUNK], kband, vband, kvb_ref[...] if s == ns - 1 else None))
    for s, att in enumerate(_attn_chunks(chunks, alibi_ref, sink_ref)):
        mix_s[s * CHUNK:(s + 1) * CHUNK, 0:Q_W] = att

    for s in range(ns):
        r0 = s * CHUNK
        us = u_s.at[s]
        us[B_OFF - 2:B_OFF, :] = jnp.zeros((B_CONV - 1, B_WIDTH), F32) if s == ns - 1 else u0_ref[s]
        us[B_OFF:B_OFF + CHUNK, :] = u[r0:r0 + CHUNK]
        cb = _short_conv(us, B_OFF, CHUNK, convw_ref)
        mix_s[r0:r0 + CHUNK, Q_W:] = bg[r0:r0 + CHUNK] * cb
        uout_ref[s] = u[r0 + CHUNK - 2:r0 + CHUNK]

    h_ref[...] = _PostBlock(x, mix_s[...], w_o_ref, ln_ref, wgu_ref, wd_ref).result()


def _tile_position(nt):
    i = pl.program_id(0)
    n = pl.num_programs(0) - 1
    return i, n, lax.rem(jnp.minimum(i, n - 1), nt)


def _skewed_step(i, step, mix_s):
    @pl.when(i == 0)
    def _():
        mix_s[...] = jnp.zeros_like(mix_s)

    step(True, True)


def _l0_main_kernel(nt, x_ref, xprev_ref, k0_ref, v0_ref, u0_ref, kvb_ref, alibi_ref, sink_ref,
                    convw_ref, w_in_ref, b_in_ref, w_o_ref, ln_ref, wgu_ref, wd_ref,
                    h_ref, kout_ref, vout_ref, uout_ref,
                    k_s, v_s, u_s, mix_s):
    i, n, t = _tile_position(nt)
    tm = x_ref.shape[1]

    @pl.when(t == 0)
    def _():
        k_s[0:WINDOW, :] = k0_ref[0]
        v_s[0:WINDOW, :] = v0_ref[0]
        u_s[B_OFF - 2:B_OFF, :] = u0_ref[0]

    def step(mixers, residual):
        post = None
        if residual:
            post = _SplitPostBlock(xprev_ref[0], mix_s[...], w_o_ref, ln_ref, wgu_ref, wd_ref, POST_ROW_GROUPS)
            post.run(1)
        if mixers:
            q, k, v, bg, u = _ab_project(x_ref[0], w_in_ref, b_in_ref)
            k_s[WINDOW:, :] = k
            v_s[WINDOW:, :] = v
            u_s[B_OFF:, :] = u
            mix_s[:, Q_W:] = bg * _short_conv(u_s, B_OFF, tm, convw_ref)

            first = t == 0
            chunks = []
            for c in range(tm // CHUNK):
                r0 = c * CHUNK
                kvbias = jnp.where(first, kvb_ref[c], 0.0) if c < 2 else None
                chunks.append((q[r0:r0 + CHUNK], k_s[r0:r0 + BAND, :], v_s[r0:r0 + BAND, :], kvbias))
            plan = iter(L0_FFN_STEPS_BETWEEN)
            between = (lambda: post.run(next(plan, 0))) if residual else _no_op
            for c, att in enumerate(_attn_chunks(chunks, alibi_ref, sink_ref, between)):
                mix_s[c * CHUNK:(c + 1) * CHUNK, 0:Q_W] = att
        if residual:
            h_ref[0] = post.result()
        if mixers:
            k_s[0:WINDOW, :] = k_s[tm:tm + WINDOW, :]
            v_s[0:WINDOW, :] = v_s[tm:tm + WINDOW, :]
            u_s[B_OFF - 2:B_OFF, :] = u_s[B_OFF + tm - 2:B_OFF + tm, :]

    _skewed_step(i, step, mix_s)

    @pl.when(jnp.logical_and(t == nt - 1, i < n))
    def _():
        kout_ref[0] = k_s[0:WINDOW, :]
        vout_ref[0] = v_s[0:WINDOW, :]
        uout_ref[0] = u_s[B_OFF - 2:B_OFF, :]


def _c_project(x, rowmask, w_in_ref, b_in_ref):
    w = C_WIDTH
    proj = _dot(x, w_in_ref[:, 0:2 * w]) + b_in_ref[:, 0:2 * w]
    u = proj[:, 0:w] * _sigmoid(proj[:, w:2 * w])
    return u if rowmask is None else u * rowmask


def _d_project(x, rowmask, w_in_ref, b_in_ref, lb_ref):
    w = D_WIDTH
    proj = _dot(x, w_in_ref[:, 2 * C_WIDTH:]) + b_in_ref[:, 2 * C_WIDTH:]
    q = proj[:, 0:w]
    d0 = lb_ref[0:1]
    d1 = lb_ref[1:2]
    mx = jnp.maximum(d0, d1)
    e0 = jnp.exp(d0 - mx)
    e1 = jnp.exp(d1 - mx)
    p0 = e0 / (e0 + e1)
    p1 = e1 / (e0 + e1)
    lb = (p0 + p1) - p0
    forget = lb + (1.0 - lb) * _sigmoid(proj[:, w:2 * w])
    kk = 1.0 - forget
    lf = jnp.log(forget)
    v = proj[:, 2 * w:3 * w]
    g = proj[:, 3 * w:4 * w]
    if rowmask is not None:
        q = q * rowmask
        kk = kk * rowmask
        v = v * rowmask
        lf = lf * rowmask
    return q, kk, v, lf, g


def _conformer_conv(c_ref, base, rows, convw_ref, cvec_ref, out_ref, out_base):
    for r0 in range(0, rows, CONV_ROWS):
        cols = []
        for c0 in range(0, C_WIDTH, LANES):
            lanes = slice(c0, c0 + LANES)
            win = c_ref[base - C_OFF + r0:base + r0 + CONV_ROWS, lanes]
            acc = None
            for phase in range(SUBLANES):
                shifted = win if phase == 0 else pltpu.roll(win, CONV_ROWS + C_OFF - phase, axis=0)
                for j in range(C_CONV):
                    off = C_OFF - C_HIST + j
                    if off % SUBLANES != phase:
                        continue
                    lo = off - phase
                    term = shifted[lo:lo + CONV_ROWS] * convw_ref[j:j + 1, lanes]
                    acc = term if acc is None else acc + term
            cols.append(acc)
        acc = jnp.concatenate(cols, axis=1)
        y = _layer_norm(acc + cvec_ref[0:1], cvec_ref[1:2], cvec_ref[2:3])
        out_ref[out_base + r0:out_base + r0 + CONV_ROWS, 0:C_WIDTH] = _silu(y)


def _hgrn_chunks(chunks, init_states, between=_no_op):
    row = lax.broadcasted_iota(jnp.int32, (CHUNK, CHUNK), 0)
    col = lax.broadcasted_iota(jnp.int32, (CHUNK, CHUNK), 1)
    tril = col <= row
    ltri = jnp.where(tril, 1.0, 0.0).astype(BF16)
    nsub = CHUNK // SUB
    rsub = jnp.right_shift(lax.broadcasted_iota(jnp.int32, (CHUNK, D_KEY), 0), SUB.bit_length() - 1)
    heads = [slice(D_KEY * h, D_KEY * (h + 1)) for h in range(D_HEADS)]

    cums = []
    for _, _, _, lf in chunks:
        hi = lf.astype(BF16)
        r1 = lf - hi.astype(F32)
        mid = r1.astype(BF16)
        lo = (r1 - mid.astype(F32)).astype(BF16)
        cums.append(jnp.dot(ltri, hi, preferred_element_type=F32)
                    + jnp.dot(ltri, mid, preferred_element_type=F32)
                    + jnp.dot(ltri, lo, preferred_element_type=F32))

    between()
    pre = []
    for (q, kk, v, _), cum in zip(chunks, cums):
        tot = cum[CHUNK - 1:CHUNK]
        bases = [jnp.zeros((1, D_WIDTH), F32)] + [cum[SUB * i - 1:SUB * i] for i in range(1, nsub)]
        bsel = jnp.concatenate([jnp.broadcast_to(b, (SUB, D_WIDTH)) for b in bases], axis=0)
        qe = (q * jnp.exp(cum)).astype(BF16)
        qd = q * jnp.exp(cum - bsel)
        kdec = (kk * jnp.exp(tot - cum)).astype(BF16)
        qblks, kdcs = [], []
        for sl in heads:
            qd_h = qd[:, sl]
            qblks.append(jnp.concatenate([jnp.where(rsub == i, qd_h, 0.0) for i in range(nsub)],
                                         axis=1).astype(BF16))
            kds = []
            for i in range(nsub):
                n = SUB * (i + 1)
                kd = kk[0:n, sl] * jnp.exp(bases[i][:, sl] - cum[0:n, sl])
                if n < CHUNK:
                    kd = jnp.concatenate([kd, jnp.zeros((CHUNK - n, D_KEY), F32)], axis=0)
                kds.append(kd)
            kdcs.append(jnp.concatenate(kds, axis=1).astype(BF16))
        pre.append((qe, kdec, v.astype(BF16), jnp.exp(tot), qblks, kdcs))

    atts, incs = [], []
    for qe, kdec, vb, etot, qblks, kdcs in pre:
        atts.append([jnp.where(tril, _dot_nt(qblks[h], kdcs[h]), 0.0).astype(BF16) for h in range(D_HEADS)])
        incs.append([_dot_tn(vb[:, sl], kdec[:, sl]) for sl in heads])

    between()
    intra = [[_dot(atts[c][h], pre[c][2][:, heads[h]]) for h in range(D_HEADS)] for c in range(len(chunks))]

    between()
    outs, ends = [], []
    state = None
    for c, (qe, kdec, vb, etot, _, _) in enumerate(pre):
        if init_states[c] is not None:
            state = init_states[c]
        o = [_dot_nt(qe[:, heads[h]], state[h]) + intra[c][h] for h in range(D_HEADS)]
        state = [etot[:, heads[h]] * state[h] + incs[c][h] for h in range(D_HEADS)]
        outs.append(jnp.concatenate(o, axis=1))
        ends.append(state)
    return outs, ends


def _hgrn_readout(o, g, norm_g):
    parts = []
    for h in range(D_HEADS):
        oh = o[:, D_VAL * h:D_VAL * (h + 1)]
        parts.append(oh * lax.rsqrt(jnp.mean(oh * oh, axis=-1, keepdims=True) + RMS_EPS))
    return jnp.concatenate(parts, axis=1) * norm_g * _silu(g)


def _l1_short_kernel(x_ref, rowmask_ref, c0_ref, s0_ref, convw_ref, cvec_ref, lb_ref, normg_ref,
                     w_in_ref, b_in_ref, w_o_ref, ln_ref, wgu_ref, wd_ref,
                     h_ref, cout_ref, sout_ref,
                     c_s, mix_s):
    ns = c0_ref.shape[0] + 1
    x = x_ref[...]
    rowmask = rowmask_ref[...]
    u = _c_project(x, rowmask, w_in_ref, b_in_ref)
    q, kk, v, lf, g = _d_project(x, rowmask, w_in_ref, b_in_ref, lb_ref)

    for s in range(ns):
        r0 = s * CHUNK
        cs = c_s.at[s]
        cs[C_OFF - C_HIST:C_OFF, :] = jnp.zeros((C_HIST, C_WIDTH), F32) if s == ns - 1 else c0_ref[s]
        cs[C_OFF:C_OFF + CHUNK, :] = u[r0:r0 + CHUNK]
        _conformer_conv(cs, C_OFF, CHUNK, convw_ref, cvec_ref, mix_s, r0)
        cout_ref[s] = cs[C_OFF + CHUNK - C_HIST:C_OFF + CHUNK, :]

    rows = [slice(s * CHUNK, (s + 1) * CHUNK) for s in range(ns)]
    outs, ends = _hgrn_chunks([(q[r], kk[r], v[r], lf[r]) for r in rows],
                              [[jnp.zeros((D_VAL, D_KEY), F32) if s == ns - 1 else s0_ref[s, h].T
                                for h in range(D_HEADS)] for s in range(ns)])
    for s in range(ns):
        mix_s[rows[s], C_WIDTH:] = _hgrn_readout(outs[s], g[rows[s]], normg_ref[...])
        for h in range(D_HEADS):
            sout_ref[s, h] = ends[s][h].T

    h_ref[...] = _PostBlock(x, mix_s[...], w_o_ref, ln_ref, wgu_ref, wd_ref).result()


def _l1_main_kernel(nt, x_ref, xprev_ref, c0_ref, s0_ref, convw_ref, cvec_ref, lb_ref, normg_ref,
                    w_in_ref, b_in_ref, w_o_ref, ln_ref, wgu_ref, wd_ref,
                    h_ref, cout_ref, sout_ref,
                    c_s, st_s, mix_s):
    i, n, t = _tile_position(nt)
    tm = x_ref.shape[1]

    @pl.when(t == 0)
    def _():
        c_s[C_OFF - C_HIST:C_OFF, :] = c0_ref[0]
        for h in range(D_HEADS):
            st_s[h] = s0_ref[0, h].T

    def step(mixers, residual):
        post = None
        if residual:
            post = _SplitPostBlock(xprev_ref[0], mix_s[...], w_o_ref, ln_ref, wgu_ref, wd_ref, POST_ROW_GROUPS)
            post.run(1)
        if mixers:
            x = x_ref[0]
            plan = iter(L1_FFN_STEPS)
            between = (lambda: post.run(next(plan, 0))) if residual else _no_op
            c_s[C_OFF:, :] = _c_project(x, None, w_in_ref, b_in_ref)
            q, kk, v, lf, g = _d_project(x, None, w_in_ref, b_in_ref, lb_ref)
            between()
            _conformer_conv(c_s, C_OFF, tm, convw_ref, cvec_ref, mix_s, 0)
            c_s[C_OFF - C_HIST:C_OFF, :] = c_s[C_OFF + tm - C_HIST:C_OFF + tm, :]

            nc = tm // CHUNK
            rows = [slice(c * CHUNK, (c + 1) * CHUNK) for c in range(nc)]
            outs, ends = _hgrn_chunks([(q[r], kk[r], v[r], lf[r]) for r in rows],
                                      [[st_s[h] for h in range(D_HEADS)]] + [None] * (nc - 1), between)
            for c in range(nc):
                mix_s[rows[c], C_WIDTH:] = _hgrn_readout(outs[c], g[rows[c]], normg_ref[...])
            for h in range(D_HEADS):
                st_s[h] = ends[-1][h]
        if residual:
            h_ref[0] = post.result()

    _skewed_step(i, step, mix_s)

    @pl.when(jnp.logical_and(t == nt - 1, i < n))
    def _():
        cout_ref[0] = c_s[C_OFF - C_HIST:C_OFF, :]
        for h in range(D_HEADS):
            sout_ref[0, h] = st_s[h].T


def _whole(shape):
    zeros = (0,) * len(shape)
    return pl.BlockSpec(shape, lambda *_: zeros, pipeline_mode=pl.Buffered(1))


def _const_spec(a):
    return pl.BlockSpec(memory_space=pltpu.SMEM) if a.ndim == 1 else _whole(a.shape)


def _weight_specs(weights, layer):
    def stacked(a):
        zeros = (0,) * (a.ndim - 1)
        return pl.BlockSpec((1,) + a.shape[1:], lambda *_: (layer,) + zeros, pipeline_mode=pl.Buffered(1))

    return [_whole(a.shape) for a in weights[:-2]] + [stacked(a) for a in weights[-2:]]


def _whole_out(shape):
    zeros = (0,) * len(shape)
    return pl.BlockSpec(shape, lambda *_: zeros)


def _params(n_axes):
    return pltpu.CompilerParams(dimension_semantics=("arbitrary",) * n_axes,
                                vmem_limit_bytes=VMEM_LIMIT)


def _l0_short(x, rowmask, k0, v0, u0, kvb, consts, weights, layer):
    rows, ns = x.shape[0], k0.shape[0] + 1
    ins = (x, rowmask, k0, v0, u0, kvb) + consts
    return pl.pallas_call(
        _l0_short_kernel,
        grid=(1,),
        in_specs=[_const_spec(a) for a in ins] + _weight_specs(weights, layer),
        out_specs=[_whole_out((rows, D_MODEL)), _whole_out((ns, WINDOW, KV_W)),
                   _whole_out((ns, WINDOW, KV_W)), _whole_out((ns, B_CONV - 1, B_WIDTH))],
        out_shape=[jax.ShapeDtypeStruct((rows, D_MODEL), F32),
                   jax.ShapeDtypeStruct((ns, WINDOW, KV_W), F32),
                   jax.ShapeDtypeStruct((ns, WINDOW, KV_W), F32),
                   jax.ShapeDtypeStruct((ns, B_CONV - 1, B_WIDTH), F32)],
        scratch_shapes=[pltpu.VMEM((ns, B_OFF + CHUNK, B_WIDTH), F32), pltpu.VMEM((rows, D_MODEL), F32)],
        compiler_params=_params(1),
        name="l0_short",
    )(*ins, *weights)


class _SkewedTiles:
    def __init__(self, nb, nt):
        self.nt, self.n = nt, nb * nt

    def _cur(self, i):
        return jnp.minimum(i, self.n - 1)

    def tile(self):
        return pl.BlockSpec((1, TILE, D_MODEL), lambda i: (self._cur(i) // self.nt, self._cur(i) % self.nt, 0))

    def prev_tile(self):
        prev = lambda i: jnp.maximum(i - 1, 0)
        return pl.BlockSpec((1, TILE, D_MODEL), lambda i: (prev(i) // self.nt, prev(i) % self.nt, 0))

    def fixed(self, shape, idx):
        zeros = (0,) * len(shape)
        return pl.BlockSpec((1,) + shape, lambda i: (idx,) + zeros)

    def per_prompt(self, shape):
        zeros = (0,) * len(shape)
        return pl.BlockSpec((1,) + shape, lambda i: (self._cur(i) // self.nt,) + zeros)


def _l0_main(x, k0, v0, u0, kvb, consts, weights, layer, meta_idx):
    nb, seq, _ = x.shape
    nt = seq // TILE
    sk = _SkewedTiles(nb, nt)
    ins = (x, x, k0, v0, u0, kvb) + consts + weights
    in_specs = [sk.tile(), sk.prev_tile(),
                sk.fixed((WINDOW, KV_W), meta_idx), sk.fixed((WINDOW, KV_W), meta_idx),
                sk.fixed((B_CONV - 1, B_WIDTH), meta_idx),
                _whole(kvb.shape)] + [_const_spec(a) for a in consts] + _weight_specs(weights, layer)
    return pl.pallas_call(
        functools.partial(_l0_main_kernel, nt),
        grid=(nb * nt + 1,),
        in_specs=in_specs,
        out_specs=[sk.prev_tile(), sk.per_prompt((WINDOW, KV_W)), sk.per_prompt((WINDOW, KV_W)),
                   sk.per_prompt((B_CONV - 1, B_WIDTH))],
        out_shape=[jax.ShapeDtypeStruct((nb, seq, D_MODEL), F32),
                   jax.ShapeDtypeStruct((nb, WINDOW, KV_W), F32),
                   jax.ShapeDtypeStruct((nb, WINDOW, KV_W), F32),
                   jax.ShapeDtypeStruct((nb, B_CONV - 1, B_WIDTH), F32)],
        scratch_shapes=[pltpu.VMEM((WINDOW + TILE, KV_W), F32),
                        pltpu.VMEM((WINDOW + TILE, KV_W), F32), pltpu.VMEM((B_OFF + TILE, B_WIDTH), F32),
                        pltpu.VMEM((TILE, D_MODEL), F32)],
        compiler_params=_params(1),
        name="l0_main",
    )(*ins)


def _l1_short(x, rowmask, c0, s0, consts, weights, layer):
    rows, ns = x.shape[0], c0.shape[0] + 1
    ins = (x, rowmask, c0, s0) + consts
    return pl.pallas_call(
        _l1_short_kernel,
        grid=(1,),
        in_specs=[_const_spec(a) for a in ins] + _weight_specs(weights, layer),
        out_specs=[_whole_out((rows, D_MODEL)), _whole_out((ns, C_HIST, C_WIDTH)),
                   _whole_out((ns, D_HEADS, D_KEY, D_VAL))],
        out_shape=[jax.ShapeDtypeStruct((rows, D_MODEL), F32),
                   jax.ShapeDtypeStruct((ns, C_HIST, C_WIDTH), F32),
                   jax.ShapeDtypeStruct((ns, D_HEADS, D_KEY, D_VAL), F32)],
        scratch_shapes=[pltpu.VMEM((ns, C_OFF + CHUNK, C_WIDTH), F32), pltpu.VMEM((rows, D_MODEL), F32)],
        compiler_params=_params(1),
        name="l1_short",
    )(*ins, *weights)


def _l1_main(x, c0, s0, consts, weights, layer, meta_idx):
    nb, seq, _ = x.shape
    nt = seq // TILE
    sk = _SkewedTiles(nb, nt)
    ins = (x, x, c0, s0) + consts + weights
    in_specs = [sk.tile(), sk.prev_tile(), sk.fixed((C_HIST, C_WIDTH), meta_idx),
                sk.fixed((D_HEADS, D_KEY, D_VAL), meta_idx),
                ] + [_const_spec(a) for a in consts] + _weight_specs(weights, layer)
    return pl.pallas_call(
        functools.partial(_l1_main_kernel, nt),
        grid=(nb * nt + 1,),
        in_specs=in_specs,
        out_specs=[sk.prev_tile(), sk.per_prompt((C_HIST, C_WIDTH)),
                   sk.per_prompt((D_HEADS, D_KEY, D_VAL))],
        out_shape=[jax.ShapeDtypeStruct((nb, seq, D_MODEL), F32),
                   jax.ShapeDtypeStruct((nb, C_HIST, C_WIDTH), F32),
                   jax.ShapeDtypeStruct((nb, D_HEADS, D_KEY, D_VAL), F32)],
        scratch_shapes=[pltpu.VMEM((C_OFF + TILE, C_WIDTH), F32), pltpu.VMEM((D_HEADS, D_VAL, D_KEY), F32),
                        pltpu.VMEM((TILE, D_MODEL), F32)],
        compiler_params=_params(1),
        name="l1_main",
    )(*ins)


def _alibi_tables():
    slopes = np.exp2(-8.0 * np.arange(1, A_HEADS + 1, dtype=np.float32) / A_HEADS).astype(np.float32)
    i = np.arange(CHUNK, dtype=np.float32)[None, :]
    j = np.arange(BAND, dtype=np.float32)[:, None]
    dist = np.abs(WINDOW + i - j).astype(np.float32)
    bias = np.zeros((A_KV_HEADS, 2 * BAND, 2 * CHUNK), np.float32)
    for g in range(A_KV_HEADS):
        for e in range(2):
            for half, head in enumerate((4 * g + e, 4 * g + 2 + e)):
                cols = slice(CHUNK * half, CHUNK * (half + 1))
                bias[g, BAND * e:BAND * (e + 1), cols] = -slopes[head] * dist
    return bias


def kernel(x_prompt, x_sample, cache_k_a, cache_v_a, state_conv_b, state_conv_c, state_hgrn, meta_tokens,
           ab_w_in, ab_b_in, a_sinks, b_conv_w, ab_w_o, cd_w_in, cd_b_in, c_conv_w, c_conv_b, c_ln_g,
           c_ln_b, d_lower_bounds, d_norm_g, cd_w_o, ln1_g, ln1_b, ln2_g, ln2_b, ffn_w_gu, ffn_w_down):
    nb = x_prompt.shape[0]
    ns = x_sample.shape[0]
    assert x_sample.shape[1] == CHUNK and cache_k_a.shape[1] == WINDOW
    assert x_prompt.shape[1] % TILE == 0
    meta_idx = ns
    n_short = ns + 1
    rows = n_short * CHUNK

    xs = jnp.concatenate([x_sample.reshape(ns * CHUNK, D_MODEL),
                          jnp.zeros((PADF, D_MODEL), F32), meta_tokens.astype(F32)], axis=0)
    rowmask = np.ones((rows, 1), np.float32)
    rowmask[ns * CHUNK:ns * CHUNK + PADF] = 0.0
    rowmask = jnp.asarray(rowmask)

    kvb_short = np.zeros((BAND, 2 * CHUNK), np.float32)
    kvb_short[:WINDOW + PADF] = NEG
    kvb_main = np.zeros((2, BAND, 2 * CHUNK), np.float32)
    kvb_main[0, :CHUNK + PADF] = NEG
    kvb_main[1, :PADF] = NEG

    a_consts = (jnp.asarray(_alibi_tables()), a_sinks.astype(F32), b_conv_w.astype(F32))

    k0 = cache_k_a.reshape(ns, WINDOW, KV_W)
    v0 = cache_v_a.reshape(ns, WINDOW, KV_W)
    u0, c0, s0 = state_conv_b, state_conv_c, state_hgrn

    w_gu = ffn_w_gu.astype(BF16)
    w_down = ffn_w_down.astype(BF16)

    def layer_weights(l, w_in, b_in, w_o):
        ln = jnp.stack([ln1_g[l], ln1_b[l], ln2_g[l], ln2_b[l]]).astype(F32)
        return (w_in.astype(BF16), b_in.astype(F32)[None, :], w_o.astype(BF16), ln, w_gu, w_down)

    w0 = layer_weights(0, ab_w_in, ab_b_in, ab_w_o)
    w1 = layer_weights(1, cd_w_in, cd_b_in, cd_w_o)

    hs0, k_s, v_s, u_s = _l0_short(xs, rowmask, k0, v0, u0, jnp.asarray(kvb_short), a_consts, w0, 0)
    hp0, k_p, v_p, u_p = _l0_main(x_prompt, k_s, v_s, u_s, jnp.asarray(kvb_main), a_consts, w0, 0, meta_idx)

    cvec = jnp.stack([c_conv_b, c_ln_g, c_ln_b]).astype(F32)
    c_consts = (c_conv_w.astype(F32), cvec, d_lower_bounds.astype(F32), d_norm_g.astype(F32)[None, :])
    hs1, c_s, s_s = _l1_short(hs0, rowmask, c0, s0, c_consts, w1, 1)
    hp1, c_p, s_p = _l1_main(hp0, c_s, s_s, c_consts, w1, 1, meta_idx)

    kv = lambda a: a.reshape(a.shape[0], WINDOW, A_KV_HEADS, HEAD_DIM)
    y_sample = hs1[:ns * CHUNK].reshape(ns, CHUNK, D_MODEL)
    return (hp1, y_sample, kv(k_p), kv(v_p), u_p, c_p, s_p,
            kv(k_s[:ns]), kv(v_s[:ns]), u_s[:ns], c_s[:ns], s_s[:ns])
```

```python
import functools

import jax
import jax.numpy as jnp
import numpy as np
from jax import lax
from jax.experimental import pallas as pl
from jax.experimental.pallas import tpu as pltpu

F32 = jnp.float32
BF16 = jnp.bfloat16

D_MODEL = 1024
CHUNK = 64
N_META = 16
PADF = CHUNK - N_META
A_HEADS = 8
A_KV_HEADS = 2
HEAD_DIM = 64
WINDOW = 128
KV_W = A_KV_HEADS * HEAD_DIM
Q_W = A_HEADS * HEAD_DIM
BAND = WINDOW + CHUNK
B_WIDTH = 512
B_CONV = 3
C_WIDTH = 512
C_CONV = 31
D_HEADS = 4
D_KEY = 128
D_VAL = 128
D_WIDTH = D_HEADS * D_VAL
SUB = 16
FFN_HIDDEN = 2816
DEPTH = 2
ALPHA = (2 * DEPTH) ** 0.25
LN_EPS = 1e-5
RMS_EPS = 1e-6
NEG = -1e30

AB_IN = Q_W + 2 * KV_W + 3 * B_WIDTH
CD_IN = 2 * C_WIDTH + 4 * D_WIDTH

TILE = 512
FFN_BLOCK = 256
POST_ROW_GROUPS = 2
L0_FFN_STEPS_BETWEEN = (POST_ROW_GROUPS,) * (2 * TILE // CHUNK)
L1_FFN_STEPS = tuple(POST_ROW_GROUPS * n for n in (6, 2, 2, 2))
SUBLANES = 8
LANES = 128
CONV_ROWS = 32
C_HIST = C_CONV - 1
C_OFF = 32
B_OFF = 8
VMEM_LIMIT = 60 * 1024 * 1024


def _dot(a, b):
    return jnp.dot(a.astype(BF16), b.astype(BF16), preferred_element_type=F32)


def _dot_nt(a, b):
    return lax.dot_general(a.astype(BF16), b.astype(BF16), (((1,), (1,)), ((), ())),
                           preferred_element_type=F32)


def _dot_tn(a, b):
    return lax.dot_general(a.astype(BF16), b.astype(BF16), (((0,), (0,)), ((), ())),
                           preferred_element_type=F32)


def _sigmoid(x):
    return 1.0 / (1.0 + jnp.exp(-x))


def _silu(x):
    return x * _sigmoid(x)


def _layer_norm(x, g, b):
    mu = jnp.mean(x, axis=-1, keepdims=True)
    xc = x - mu
    var = jnp.mean(xc * xc, axis=-1, keepdims=True)
    return xc * lax.rsqrt(var + LN_EPS) * g + b


class _PostBlock:
    N_BLOCKS = FFN_HIDDEN // FFN_BLOCK
    N_STEPS = 2 + N_BLOCKS

    def __init__(self, x, mix, w_o_ref, ln_ref, wg_ref, wu_ref, wd_ref):
        self.x, self.mix = x, mix
        self.w_o_ref, self.ln_ref = w_o_ref, ln_ref
        self.wg_ref, self.wu_ref, self.wd_ref = wg_ref, wu_ref, wd_ref
        self.done = 0
        self.h = self.hb = self.acc = self.act = None

    def _gate_up(self, j):
        lo = j * FFN_BLOCK
        gate = jnp.dot(self.hb, self.wg_ref[:, lo:lo + FFN_BLOCK], preferred_element_type=F32)
        up = jnp.dot(self.hb, self.wu_ref[:, lo:lo + FFN_BLOCK], preferred_element_type=F32)
        return (_silu(gate) * up).astype(BF16)

    def _step(self, j):
        if j == 0:
            ln = self.ln_ref
            self.h = _layer_norm(ALPHA * self.x + _dot(self.mix, self.w_o_ref[...]), ln[0:1], ln[1:2])
            self.hb = self.h.astype(BF16)
            return
        prev = self.act
        self.act = self._gate_up(j - 1) if j - 1 < self.N_BLOCKS else None
        if prev is not None:
            lo = (j - 2) * FFN_BLOCK
            part = jnp.dot(prev, self.wd_ref[lo:lo + FFN_BLOCK, :], preferred_element_type=F32)
            self.acc = part if self.acc is None else self.acc + part

    def run(self, n):
        for _ in range(n):
            if self.done < self.N_STEPS:
                self._step(self.done)
                self.done += 1

    def result(self):
        self.run(self.N_STEPS)
        return _layer_norm(ALPHA * self.h + self.acc, self.ln_ref[2:3], self.ln_ref[3:4])


class _SplitPostBlock:
    def __init__(self, x, mix, weight_refs, groups, project_first):
        rows = x.shape[0] // groups
        self.parts = [_PostBlock(x[g * rows:(g + 1) * rows], mix[g * rows:(g + 1) * rows], *weight_refs)
                      for g in range(groups)]
        self.outs = []
        self.projected = not project_first

    def run(self, n):
        for _ in range(n):
            if not self.projected:
                for part in self.parts:
                    part.run(1)
                self.projected = True
                continue
            if len(self.outs) == len(self.parts):
                return
            part = self.parts[len(self.outs)]
            part.run(1)
            if part.done == part.N_STEPS:
                self.outs.append(part.result())

    def result(self):
        self.run(sum(part.N_STEPS for part in self.parts))
        return jnp.concatenate(self.outs, axis=0)


def _no_op():
    pass


def _attn_chunks(chunks, alibi_ref, sink_ref, between=_no_op):
    lane = lax.broadcasted_iota(jnp.int32, (BAND, KV_W), 1)
    low = lane < HEAD_DIM
    high = jnp.logical_not(low)
    sink_rows = [jnp.where(low[0:1], sink_ref[4 * g + e], sink_ref[4 * g + 2 + e])
                 for g in range(A_KV_HEADS) for e in range(2)]

    def extend(band, swapped, g, e):
        src = band if g == e else swapped
        return jnp.where(low if e == 0 else high, src, 0.0).astype(BF16)

    scores = []
    for q, kband, _, kvbias in chunks:
        kswap = pltpu.roll(kband, HEAD_DIM, axis=1)
        for g in range(A_KV_HEADS):
            qg = jnp.concatenate([q[:, 256 * g:256 * g + 128], q[:, 256 * g + 128:256 * g + 256]],
                                 axis=0).astype(BF16)
            k2 = jnp.concatenate([extend(kband, kswap, g, e) for e in range(2)], axis=0)
            s = _dot_nt(k2, qg) + alibi_ref[g]
            for e in range(2):
                se = s[BAND * e:BAND * (e + 1)]
                scores.append(se if kvbias is None else se + kvbias)
        between()

    probs = []
    for idx, s in enumerate(scores):
        sink = sink_rows[idx % (2 * A_KV_HEADS)]
        m = jnp.maximum(jnp.max(s, axis=0, keepdims=True), sink)
        p = jnp.exp(s - m)
        den = jnp.sum(p, axis=0, keepdims=True) + jnp.exp(sink - m)
        probs.append((p * (1.0 / den)).astype(BF16))
        if idx % (2 * A_KV_HEADS) == 2 * A_KV_HEADS - 1:
            between()

    outs = []
    for ci, (_, _, vband, _) in enumerate(chunks):
        vswap = pltpu.roll(vband, HEAD_DIM, axis=1)
        parts = []
        for g in range(A_KV_HEADS):
            base = ci * 2 * A_KV_HEADS + 2 * g
            p2 = jnp.concatenate([probs[base], probs[base + 1]], axis=0)
            v2 = jnp.concatenate([extend(vband, vswap, g, e) for e in range(2)], axis=0)
            o = _dot_tn(p2, v2)
            parts += [o[0:CHUNK], o[CHUNK:2 * CHUNK]]
        outs.append(jnp.concatenate(parts, axis=1))
    return outs


def _ab_project(x, w_in_ref, b_in_ref):
    proj = _dot(x, w_in_ref[...]) + b_in_ref[...]
    q = proj[:, 0:Q_W] * (HEAD_DIM ** -0.5)
    k = proj[:, Q_W:Q_W + KV_W]
    v = proj[:, Q_W + KV_W:Q_W + 2 * KV_W]
    o = Q_W + 2 * KV_W
    bg = proj[:, o:o + B_WIDTH]
    u = proj[:, o + B_WIDTH:o + 2 * B_WIDTH] * proj[:, o + 2 * B_WIDTH:o + 3 * B_WIDTH]
    return q, k, v, bg, u


def _short_conv(u_ref, base, rows, w_ref):
    out = None
    for j in range(B_CONV):
        term = u_ref[pl.ds(base - (B_CONV - 1) + j, rows), :] * w_ref[j:j + 1, :]
        out = term if out is None else out + term
    return out


def _streamed_residual_block(first, x_ref, mix_s, w_o_ref, ln_ref, wg_ref, wu_ref, wd_ref,
                             h_ref, wg_out, wu_out, wd_out, h_s, hb_s, acc_s):
    j = pl.program_id(0)

    @pl.when(j == 0)
    def _():
        first()
        h = _layer_norm(ALPHA * x_ref[...] + _dot(mix_s[...], w_o_ref[...]), ln_ref[0:1], ln_ref[1:2])
        h_s[...] = h
        hb_s[...] = h.astype(BF16)
        acc_s[...] = jnp.zeros_like(acc_s)

    wg = wg_ref[0].astype(BF16)
    wu = wu_ref[0].astype(BF16)
    wd = wd_ref[0].astype(BF16)
    wg_out[...] = wg
    wu_out[...] = wu
    wd_out[...] = wd
    hb = hb_s[...]
    gate = jnp.dot(hb, wg, preferred_element_type=F32)
    up = jnp.dot(hb, wu, preferred_element_type=F32)
    acc_s[...] += jnp.dot((_silu(gate) * up).astype(BF16), wd, preferred_element_type=F32)

    @pl.when(j == pl.num_programs(0) - 1)
    def _():
        h_ref[...] = _layer_norm(ALPHA * h_s[...] + acc_s[...], ln_ref[2:3], ln_ref[3:4])


def _l0_short_kernel(x_ref, rowmask_ref, k0_ref, v0_ref, u0_ref, kvb_ref, alibi_ref, sink_ref,
                     convw_ref, w_in_ref, b_in_ref, w_o_ref, ln_ref, wg_ref, wu_ref, wd_ref,
                     h_ref, kout_ref, vout_ref, uout_ref, wg_out, wu_out, wd_out,
                     u_s, mix_s, h_s, hb_s, acc_s):
    mixers = functools.partial(_l0_short_mixers, x_ref, rowmask_ref, k0_ref, v0_ref, u0_ref, kvb_ref,
                               alibi_ref, sink_ref, convw_ref, w_in_ref, b_in_ref,
                               kout_ref, vout_ref, uout_ref, u_s, mix_s)
    _streamed_residual_block(mixers, x_ref, mix_s, w_o_ref, ln_ref, wg_ref, wu_ref, wd_ref,
                             h_ref, wg_out, wu_out, wd_out, h_s, hb_s, acc_s)


def _l0_short_mixers(x_ref, rowmask_ref, k0_ref, v0_ref, u0_ref, kvb_ref, alibi_ref, sink_ref,
                     convw_ref, w_in_ref, b_in_ref, kout_ref, vout_ref, uout_ref, u_s, mix_s):
    ns = k0_ref.shape[0] + 1
    x = x_ref[...]
    q, k, v, bg, u = _ab_project(x, w_in_ref, b_in_ref)
    u = u * rowmask_ref[...]

    chunks = []
    for s in range(ns):
        r0 = s * CHUNK
        meta = s == ns - 1
        empty = jnp.zeros((WINDOW, KV_W), F32)
        kband = jnp.concatenate([empty if meta else k0_ref[s], k[r0:r0 + CHUNK]], axis=0)
        vband = jnp.concatenate([empty if meta else v0_ref[s], v[r0:r0 + CHUNK]], axis=0)
        kout_ref[s] = kband[CHUNK:]
        vout_ref[s] = vband[CHUNK:]
        chunks.append((q[r0:r0 + CHUNK], kband, vband, kvb_ref[...] if s == ns - 1 else None))
    for s, att in enumerate(_attn_chunks(chunks, alibi_ref, sink_ref)):
        mix_s[s * CHUNK:(s + 1) * CHUNK, 0:Q_W] = att

    for s in range(ns):
        r0 = s * CHUNK
        us = u_s.at[s]
        us[B_OFF - 2:B_OFF, :] = jnp.zeros((B_CONV - 1, B_WIDTH), F32) if s == ns - 1 else u0_ref[s]
        us[B_OFF:B_OFF + CHUNK, :] = u[r0:r0 + CHUNK]
        cb = _short_conv(us, B_OFF, CHUNK, convw_ref)
        mix_s[r0:r0 + CHUNK, Q_W:] = bg[r0:r0 + CHUNK] * cb
        uout_ref[s] = u[r0 + CHUNK - 2:r0 + CHUNK]


def _tile_position(nt):
    i = pl.program_id(0)
    n = pl.num_programs(0) - 1
    return i, n, lax.rem(jnp.minimum(i, n - 1), nt)


def _skewed_step(i, step, mix_s):
    @pl.when(i == 0)
    def _():
        mix_s[...] = jnp.zeros_like(mix_s)

    step(True, True)


def _l0_main_kernel(nt, x_ref, xprev_ref, k0_ref, v0_ref, u0_ref, kvb_ref, alibi_ref, sink_ref,
                    convw_ref, w_in_ref, b_in_ref, w_o_ref, ln_ref, wg_ref, wu_ref, wd_ref,
                    h_ref, kout_ref, vout_ref, uout_ref,
                    k_s, v_s, u_s, mix_s):
    i, n, t = _tile_position(nt)
    tm = x_ref.shape[1]

    @pl.when(t == 0)
    def _():
        k_s[0:WINDOW, :] = k0_ref[0]
        v_s[0:WINDOW, :] = v0_ref[0]
        u_s[B_OFF - 2:B_OFF, :] = u0_ref[0]

    def step(mixers, residual):
        post = None
        if residual:
            post = _SplitPostBlock(xprev_ref[0], mix_s[...], (w_o_ref, ln_ref, wg_ref, wu_ref, wd_ref),
                                   POST_ROW_GROUPS, project_first=True)
            post.run(1)
        if mixers:
            q, k, v, bg, u = _ab_project(x_ref[0], w_in_ref, b_in_ref)
            k_s[WINDOW:, :] = k
            v_s[WINDOW:, :] = v
            u_s[B_OFF:, :] = u
            mix_s[:, Q_W:] = bg * _short_conv(u_s, B_OFF, tm, convw_ref)

            first = t == 0
            chunks = []
            for c in range(tm // CHUNK):
                r0 = c * CHUNK
                kvbias = jnp.where(first, kvb_ref[c], 0.0) if c < 2 else None
                chunks.append((q[r0:r0 + CHUNK], k_s[r0:r0 + BAND, :], v_s[r0:r0 + BAND, :], kvbias))
            plan = iter(L0_FFN_STEPS_BETWEEN)
            between = (lambda: post.run(next(plan, 0))) if residual else _no_op
            for c, att in enumerate(_attn_chunks(chunks, alibi_ref, sink_ref, between)):
                mix_s[c * CHUNK:(c + 1) * CHUNK, 0:Q_W] = att
        if residual:
            h_ref[0] = post.result()
        if mixers:
            k_s[0:WINDOW, :] = k_s[tm:tm + WINDOW, :]
            v_s[0:WINDOW, :] = v_s[tm:tm + WINDOW, :]
            u_s[B_OFF - 2:B_OFF, :] = u_s[B_OFF + tm - 2:B_OFF + tm, :]

    _skewed_step(i, step, mix_s)

    @pl.when(jnp.logical_and(t == nt - 1, i < n))
    def _():
        kout_ref[0] = k_s[0:WINDOW, :]
        vout_ref[0] = v_s[0:WINDOW, :]
        uout_ref[0] = u_s[B_OFF - 2:B_OFF, :]


def _c_project(x, rowmask, w_in_ref, b_in_ref):
    w = C_WIDTH
    proj = _dot(x, w_in_ref[:, 0:2 * w]) + b_in_ref[:, 0:2 * w]
    u = proj[:, 0:w] * _sigmoid(proj[:, w:2 * w])
    return u if rowmask is None else u * rowmask


def _d_project(x, rowmask, w_in_ref, b_in_ref, lb_ref):
    w = D_WIDTH
    proj = _dot(x, w_in_ref[:, 2 * C_WIDTH:]) + b_in_ref[:, 2 * C_WIDTH:]
    q = proj[:, 0:w]
    d0 = lb_ref[0:1]
    d1 = lb_ref[1:2]
    mx = jnp.maximum(d0, d1)
    e0 = jnp.exp(d0 - mx)
    e1 = jnp.exp(d1 - mx)
    p0 = e0 / (e0 + e1)
    p1 = e1 / (e0 + e1)
    lb = (p0 + p1) - p0
    forget = lb + (1.0 - lb) * _sigmoid(proj[:, w:2 * w])
    kk = 1.0 - forget
    lf = jnp.log(forget)
    v = proj[:, 2 * w:3 * w]
    g = proj[:, 3 * w:4 * w]
    if rowmask is not None:
        q = q * rowmask
        kk = kk * rowmask
        v = v * rowmask
        lf = lf * rowmask
    return q, kk, v, lf, g


def _conformer_conv(c_ref, base, rows, convw_ref, cvec_ref, out_ref, out_base):
    for r0 in range(0, rows, CONV_ROWS):
        cols = []
        for c0 in range(0, C_WIDTH, LANES):
            lanes = slice(c0, c0 + LANES)
            win = c_ref[base - C_OFF + r0:base + r0 + CONV_ROWS, lanes]
            acc = None
            for phase in range(SUBLANES):
                shifted = win if phase == 0 else pltpu.roll(win, CONV_ROWS + C_OFF - phase, axis=0)
                for j in range(C_CONV):
                    off = C_OFF - C_HIST + j
                    if off % SUBLANES != phase:
                        continue
                    lo = off - phase
                    term = shifted[lo:lo + CONV_ROWS] * convw_ref[j:j + 1, lanes]
                    acc = term if acc is None else acc + term
            cols.append(acc)
        acc = jnp.concatenate(cols, axis=1)
        y = _layer_norm(acc + cvec_ref[0:1], cvec_ref[1:2], cvec_ref[2:3])
        out_ref[out_base + r0:out_base + r0 + CONV_ROWS, 0:C_WIDTH] = _silu(y)


def _hgrn_chunks(chunks, init_states, between=_no_op):
    row = lax.broadcasted_iota(jnp.int32, (CHUNK, CHUNK), 0)
    col = lax.broadcasted_iota(jnp.int32, (CHUNK, CHUNK), 1)
    tril = col <= row
    ltri = jnp.where(tril, 1.0, 0.0).astype(BF16)
    nsub = CHUNK // SUB
    rsub = jnp.right_shift(lax.broadcasted_iota(jnp.int32, (CHUNK, D_KEY), 0), SUB.bit_length() - 1)
    heads = [slice(D_KEY * h, D_KEY * (h + 1)) for h in range(D_HEADS)]

    cums = []
    for _, _, _, lf in chunks:
        hi = lf.astype(BF16)
        r1 = lf - hi.astype(F32)
        mid = r1.astype(BF16)
        lo = (r1 - mid.astype(F32)).astype(BF16)
        cums.append(jnp.dot(ltri, hi, preferred_element_type=F32)
                    + jnp.dot(ltri, mid, preferred_element_type=F32)
                    + jnp.dot(ltri, lo, preferred_element_type=F32))

    between()
    pre = []
    for (q, kk, v, _), cum in zip(chunks, cums):
        tot = cum[CHUNK - 1:CHUNK]
        bases = [jnp.zeros((1, D_WIDTH), F32)] + [cum[SUB * i - 1:SUB * i] for i in range(1, nsub)]
        bsel = jnp.concatenate([jnp.broadcast_to(b, (SUB, D_WIDTH)) for b in bases], axis=0)
        qe = (q * jnp.exp(cum)).astype(BF16)
        qd = q * jnp.exp(cum - bsel)
        kdec = (kk * jnp.exp(tot - cum)).astype(BF16)
        qblks, kdcs = [], []
        for sl in heads:
            qd_h = qd[:, sl]
            qblks.append(jnp.concatenate([jnp.where(rsub == i, qd_h, 0.0) for i in range(nsub)],
                                         axis=1).astype(BF16))
            kds = []
            for i in range(nsub):
                n = SUB * (i + 1)
                kd = kk[0:n, sl] * jnp.exp(bases[i][:, sl] - cum[0:n, sl])
                if n < CHUNK:
                    kd = jnp.concatenate([kd, jnp.zeros((CHUNK - n, D_KEY), F32)], axis=0)
                kds.append(kd)
            kdcs.append(jnp.concatenate(kds, axis=1).astype(BF16))
        pre.append((qe, kdec, v.astype(BF16), jnp.exp(tot), qblks, kdcs))

    atts, incs = [], []
    for qe, kdec, vb, etot, qblks, kdcs in pre:
        atts.append([jnp.where(tril, _dot_nt(qblks[h], kdcs[h]), 0.0).astype(BF16) for h in range(D_HEADS)])
        incs.append([_dot_tn(vb[:, sl], kdec[:, sl]) for sl in heads])

    between()
    intra = [[_dot(atts[c][h], pre[c][2][:, heads[h]]) for h in range(D_HEADS)] for c in range(len(chunks))]

    between()
    outs, ends = [], []
    state = None
    for c, (qe, kdec, vb, etot, _, _) in enumerate(pre):
        if init_states[c] is not None:
            state = init_states[c]
        o = [_dot_nt(qe[:, heads[h]], state[h]) + intra[c][h] for h in range(D_HEADS)]
        state = [etot[:, heads[h]] * state[h] + incs[c][h] for h in range(D_HEADS)]
        outs.append(jnp.concatenate(o, axis=1))
        ends.append(state)
    return outs, ends


def _hgrn_readout(o, g, norm_g):
    parts = []
    for h in range(D_HEADS):
        oh = o[:, D_VAL * h:D_VAL * (h + 1)]
        parts.append(oh * lax.rsqrt(jnp.mean(oh * oh, axis=-1, keepdims=True) + RMS_EPS))
    return jnp.concatenate(parts, axis=1) * norm_g * _silu(g)


def _l1_short_kernel(x_ref, rowmask_ref, c0_ref, s0_ref, convw_ref, cvec_ref, lb_ref, normg_ref,
                     w_in_ref, b_in_ref, w_o_ref, ln_ref, wg_ref, wu_ref, wd_ref,
                     h_ref, cout_ref, sout_ref, wg_out, wu_out, wd_out,
                     c_s, mix_s, h_s, hb_s, acc_s):
    mixers = functools.partial(_l1_short_mixers, x_ref, rowmask_ref, c0_ref, s0_ref, convw_ref, cvec_ref,
                               lb_ref, normg_ref, w_in_ref, b_in_ref, cout_ref, sout_ref, c_s, mix_s)
    _streamed_residual_block(mixers, x_ref, mix_s, w_o_ref, ln_ref, wg_ref, wu_ref, wd_ref,
                             h_ref, wg_out, wu_out, wd_out, h_s, hb_s, acc_s)


def _l1_short_mixers(x_ref, rowmask_ref, c0_ref, s0_ref, convw_ref, cvec_ref, lb_ref, normg_ref,
                     w_in_ref, b_in_ref, cout_ref, sout_ref, c_s, mix_s):
    ns = c0_ref.shape[0] + 1
    x = x_ref[...]
    rowmask = rowmask_ref[...]
    u = _c_project(x, rowmask, w_in_ref, b_in_ref)
    q, kk, v, lf, g = _d_project(x, rowmask, w_in_ref, b_in_ref, lb_ref)

    for s in range(ns):
        r0 = s * CHUNK
        cs = c_s.at[s]
        cs[C_OFF - C_HIST:C_OFF, :] = jnp.zeros((C_HIST, C_WIDTH), F32) if s == ns - 1 else c0_ref[s]
        cs[C_OFF:C_OFF + CHUNK, :] = u[r0:r0 + CHUNK]
        _conformer_conv(cs, C_OFF, CHUNK, convw_ref, cvec_ref, mix_s, r0)
        cout_ref[s] = cs[C_OFF + CHUNK - C_HIST:C_OFF + CHUNK, :]

    rows = [slice(s * CHUNK, (s + 1) * CHUNK) for s in range(ns)]
    outs, ends = _hgrn_chunks([(q[r], kk[r], v[r], lf[r]) for r in rows],
                              [[jnp.zeros((D_VAL, D_KEY), F32) if s == ns - 1 else s0_ref[s, h].T
                                for h in range(D_HEADS)] for s in range(ns)])
    for s in range(ns):
        mix_s[rows[s], C_WIDTH:] = _hgrn_readout(outs[s], g[rows[s]], normg_ref[...])
        for h in range(D_HEADS):
            sout_ref[s, h] = ends[s][h].T


def _l1_main_kernel(nt, x_ref, xprev_ref, c0_ref, s0_ref, convw_ref, cvec_ref, lb_ref, normg_ref,
                    w_in_ref, b_in_ref, w_o_ref, ln_ref, wg_ref, wu_ref, wd_ref,
                    h_ref, cout_ref, sout_ref,
                    c_s, st_s, mix_s):
    i, n, t = _tile_position(nt)
    tm = x_ref.shape[1]

    @pl.when(t == 0)
    def _():
        c_s[C_OFF - C_HIST:C_OFF, :] = c0_ref[0]
        for h in range(D_HEADS):
            st_s[h] = s0_ref[0, h].T

    def step(mixers, residual):
        post = None
        if residual:
            post = _SplitPostBlock(xprev_ref[0], mix_s[...], (w_o_ref, ln_ref, wg_ref, wu_ref, wd_ref),
                                   POST_ROW_GROUPS, project_first=False)
            post.run(1)
        if mixers:
            x = x_ref[0]
            plan = iter(L1_FFN_STEPS)
            between = (lambda: post.run(next(plan, 0))) if residual else _no_op
            c_s[C_OFF:, :] = _c_project(x, None, w_in_ref, b_in_ref)
            q, kk, v, lf, g = _d_project(x, None, w_in_ref, b_in_ref, lb_ref)
            between()
            _conformer_conv(c_s, C_OFF, tm, convw_ref, cvec_ref, mix_s, 0)
            c_s[C_OFF - C_HIST:C_OFF, :] = c_s[C_OFF + tm - C_HIST:C_OFF + tm, :]

            nc = tm // CHUNK
            rows = [slice(c * CHUNK, (c + 1) * CHUNK) for c in range(nc)]
            outs, ends = _hgrn_chunks([(q[r], kk[r], v[r], lf[r]) for r in rows],
                                      [[st_s[h] for h in range(D_HEADS)]] + [None] * (nc - 1), between)
            for c in range(nc):
                mix_s[rows[c], C_WIDTH:] = _hgrn_readout(outs[c], g[rows[c]], normg_ref[...])
            for h in range(D_HEADS):
                st_s[h] = ends[-1][h]
        if residual:
            h_ref[0] = post.result()

    _skewed_step(i, step, mix_s)

    @pl.when(jnp.logical_and(t == nt - 1, i < n))
    def _():
        cout_ref[0] = c_s[C_OFF - C_HIST:C_OFF, :]
        for h in range(D_HEADS):
            sout_ref[0, h] = st_s[h].T


def _whole(shape):
    zeros = (0,) * len(shape)
    return pl.BlockSpec(shape, lambda *_: zeros, pipeline_mode=pl.Buffered(1))


def _const_spec(a):
    return pl.BlockSpec(memory_space=pltpu.SMEM) if a.ndim == 1 else _whole(a.shape)


N_FFN_BLOCKS = FFN_HIDDEN // FFN_BLOCK


def _streamed_ffn_specs(layer):
    in_specs = [pl.BlockSpec((1, D_MODEL, FFN_BLOCK), lambda j: (layer, 0, j)),
                pl.BlockSpec((1, D_MODEL, FFN_BLOCK), lambda j: (layer, 0, N_FFN_BLOCKS + j)),
                pl.BlockSpec((1, FFN_BLOCK, D_MODEL), lambda j: (layer, j, 0))]
    out_specs = [pl.BlockSpec((D_MODEL, FFN_BLOCK), lambda j: (0, j)),
                 pl.BlockSpec((D_MODEL, FFN_BLOCK), lambda j: (0, j)),
                 pl.BlockSpec((FFN_BLOCK, D_MODEL), lambda j: (j, 0))]
    out_shape = [jax.ShapeDtypeStruct((D_MODEL, FFN_HIDDEN), BF16),
                 jax.ShapeDtypeStruct((D_MODEL, FFN_HIDDEN), BF16),
                 jax.ShapeDtypeStruct((FFN_HIDDEN, D_MODEL), BF16)]
    return in_specs, out_specs, out_shape


def _residual_scratch(rows):
    return [pltpu.VMEM((rows, D_MODEL), F32), pltpu.VMEM((rows, D_MODEL), BF16), pltpu.VMEM((rows, D_MODEL), F32)]


def _whole_out(shape):
    zeros = (0,) * len(shape)
    return pl.BlockSpec(shape, lambda *_: zeros)


def _params(n_axes):
    return pltpu.CompilerParams(dimension_semantics=("arbitrary",) * n_axes,
                                vmem_limit_bytes=VMEM_LIMIT)


def _l0_short(x, rowmask, k0, v0, u0, kvb, consts, weights, w_gu, w_down, layer):
    rows, ns = x.shape[0], k0.shape[0] + 1
    ins = (x, rowmask, k0, v0, u0, kvb) + consts + weights
    ffn_in, ffn_out, ffn_shape = _streamed_ffn_specs(layer)
    return pl.pallas_call(
        _l0_short_kernel,
        grid=(N_FFN_BLOCKS,),
        in_specs=[_const_spec(a) for a in ins] + ffn_in,
        out_specs=[_whole_out((rows, D_MODEL)), _whole_out((ns, WINDOW, KV_W)),
                   _whole_out((ns, WINDOW, KV_W)), _whole_out((ns, B_CONV - 1, B_WIDTH))] + ffn_out,
        out_shape=[jax.ShapeDtypeStruct((rows, D_MODEL), F32),
                   jax.ShapeDtypeStruct((ns, WINDOW, KV_W), F32),
                   jax.ShapeDtypeStruct((ns, WINDOW, KV_W), F32),
                   jax.ShapeDtypeStruct((ns, B_CONV - 1, B_WIDTH), F32)] + ffn_shape,
        scratch_shapes=[pltpu.VMEM((ns, B_OFF + CHUNK, B_WIDTH), F32), pltpu.VMEM((rows, D_MODEL), F32)]
        + _residual_scratch(rows),
        compiler_params=_params(1),
        name="l0_short",
    )(*ins, w_gu, w_gu, w_down)


class _SkewedTiles:
    def __init__(self, nb, nt):
        self.nt, self.n = nt, nb * nt

    def _cur(self, i):
        return jnp.minimum(i, self.n - 1)

    def tile(self):
        return pl.BlockSpec((1, TILE, D_MODEL), lambda i: (self._cur(i) // self.nt, self._cur(i) % self.nt, 0))

    def prev_tile(self):
        prev = lambda i: jnp.maximum(i - 1, 0)
        return pl.BlockSpec((1, TILE, D_MODEL), lambda i: (prev(i) // self.nt, prev(i) % self.nt, 0))

    def fixed(self, shape, idx):
        zeros = (0,) * len(shape)
        return pl.BlockSpec((1,) + shape, lambda i: (idx,) + zeros)

    def per_prompt(self, shape):
        zeros = (0,) * len(shape)
        return pl.BlockSpec((1,) + shape, lambda i: (self._cur(i) // self.nt,) + zeros)


def _l0_main(x, k0, v0, u0, kvb, consts, weights, meta_idx):
    nb, seq, _ = x.shape
    nt = seq // TILE
    sk = _SkewedTiles(nb, nt)
    ins = (x, x, k0, v0, u0, kvb) + consts + weights
    in_specs = [sk.tile(), sk.prev_tile(),
                sk.fixed((WINDOW, KV_W), meta_idx), sk.fixed((WINDOW, KV_W), meta_idx),
                sk.fixed((B_CONV - 1, B_WIDTH), meta_idx),
                _whole(kvb.shape)] + [_const_spec(a) for a in consts + weights]
    return pl.pallas_call(
        functools.partial(_l0_main_kernel, nt),
        grid=(nb * nt + 1,),
        in_specs=in_specs,
        out_specs=[sk.prev_tile(), sk.per_prompt((WINDOW, KV_W)), sk.per_prompt((WINDOW, KV_W)),
                   sk.per_prompt((B_CONV - 1, B_WIDTH))],
        out_shape=[jax.ShapeDtypeStruct((nb, seq, D_MODEL), F32),
                   jax.ShapeDtypeStruct((nb, WINDOW, KV_W), F32),
                   jax.ShapeDtypeStruct((nb, WINDOW, KV_W), F32),
                   jax.ShapeDtypeStruct((nb, B_CONV - 1, B_WIDTH), F32)],
        scratch_shapes=[pltpu.VMEM((WINDOW + TILE, KV_W), F32),
                        pltpu.VMEM((WINDOW + TILE, KV_W), F32), pltpu.VMEM((B_OFF + TILE, B_WIDTH), F32),
                        pltpu.VMEM((TILE, D_MODEL), F32)],
        compiler_params=_params(1),
        name="l0_main",
    )(*ins)


def _l1_short(x, rowmask, c0, s0, consts, weights, w_gu, w_down, layer):
    rows, ns = x.shape[0], c0.shape[0] + 1
    ins = (x, rowmask, c0, s0) + consts + weights
    ffn_in, ffn_out, ffn_shape = _streamed_ffn_specs(layer)
    return pl.pallas_call(
        _l1_short_kernel,
        grid=(N_FFN_BLOCKS,),
        in_specs=[_const_spec(a) for a in ins] + ffn_in,
        out_specs=[_whole_out((rows, D_MODEL)), _whole_out((ns, C_HIST, C_WIDTH)),
                   _whole_out((ns, D_HEADS, D_KEY, D_VAL))] + ffn_out,
        out_shape=[jax.ShapeDtypeStruct((rows, D_MODEL), F32),
                   jax.ShapeDtypeStruct((ns, C_HIST, C_WIDTH), F32),
                   jax.ShapeDtypeStruct((ns, D_HEADS, D_KEY, D_VAL), F32)] + ffn_shape,
        scratch_shapes=[pltpu.VMEM((ns, C_OFF + CHUNK, C_WIDTH), F32), pltpu.VMEM((rows, D_MODEL), F32)]
        + _residual_scratch(rows),
        compiler_params=_params(1),
        name="l1_short",
    )(*ins, w_gu, w_gu, w_down)


def _l1_main(x, c0, s0, consts, weights, meta_idx):
    nb, seq, _ = x.shape
    nt = seq // TILE
    sk = _SkewedTiles(nb, nt)
    ins = (x, x, c0, s0) + consts + weights
    in_specs = [sk.tile(), sk.prev_tile(), sk.fixed((C_HIST, C_WIDTH), meta_idx),
                sk.fixed((D_HEADS, D_KEY, D_VAL), meta_idx),
                ] + [_const_spec(a) for a in consts + weights]
    return pl.pallas_call(
        functools.partial(_l1_main_kernel, nt),
        grid=(nb * nt + 1,),
        in_specs=in_specs,
        out_specs=[sk.prev_tile(), sk.per_prompt((C_HIST, C_WIDTH)),
                   sk.per_prompt((D_HEADS, D_KEY, D_VAL))],
        out_shape=[jax.ShapeDtypeStruct((nb, seq, D_MODEL), F32),
                   jax.ShapeDtypeStruct((nb, C_HIST, C_WIDTH), F32),
                   jax.ShapeDtypeStruct((nb, D_HEADS, D_KEY, D_VAL), F32)],
        scratch_shapes=[pltpu.VMEM((C_OFF + TILE, C_WIDTH), F32), pltpu.VMEM((D_HEADS, D_VAL, D_KEY), F32),
                        pltpu.VMEM((TILE, D_MODEL), F32)],
        compiler_params=_params(1),
        name="l1_main",
    )(*ins)


def _alibi_tables():
    slopes = np.exp2(-8.0 * np.arange(1, A_HEADS + 1, dtype=np.float32) / A_HEADS).astype(np.float32)
    i = np.arange(CHUNK, dtype=np.float32)[None, :]
    j = np.arange(BAND, dtype=np.float32)[:, None]
    dist = np.abs(WINDOW + i - j).astype(np.float32)
    bias = np.zeros((A_KV_HEADS, 2 * BAND, 2 * CHUNK), np.float32)
    for g in range(A_KV_HEADS):
        for e in range(2):
            for half, head in enumerate((4 * g + e, 4 * g + 2 + e)):
                cols = slice(CHUNK * half, CHUNK * (half + 1))
                bias[g, BAND * e:BAND * (e + 1), cols] = -slopes[head] * dist
    return bias


def kernel(x_prompt, x_sample, cache_k_a, cache_v_a, state_conv_b, state_conv_c, state_hgrn, meta_tokens,
           ab_w_in, ab_b_in, a_sinks, b_conv_w, ab_w_o, cd_w_in, cd_b_in, c_conv_w, c_conv_b, c_ln_g,
           c_ln_b, d_lower_bounds, d_norm_g, cd_w_o, ln1_g, ln1_b, ln2_g, ln2_b, ffn_w_gu, ffn_w_down):
    nb = x_prompt.shape[0]
    ns = x_sample.shape[0]
    assert x_sample.shape[1] == CHUNK and cache_k_a.shape[1] == WINDOW
    assert x_prompt.shape[1] % TILE == 0
    meta_idx = ns
    n_short = ns + 1
    rows = n_short * CHUNK

    xs = jnp.concatenate([x_sample.reshape(ns * CHUNK, D_MODEL),
                          jnp.zeros((PADF, D_MODEL), F32), meta_tokens.astype(F32)], axis=0)
    rowmask = np.ones((rows, 1), np.float32)
    rowmask[ns * CHUNK:ns * CHUNK + PADF] = 0.0
    rowmask = jnp.asarray(rowmask)

    kvb_short = np.zeros((BAND, 2 * CHUNK), np.float32)
    kvb_short[:WINDOW + PADF] = NEG
    kvb_main = np.zeros((2, BAND, 2 * CHUNK), np.float32)
    kvb_main[0, :CHUNK + PADF] = NEG
    kvb_main[1, :PADF] = NEG

    a_consts = (jnp.asarray(_alibi_tables()), a_sinks.astype(F32), b_conv_w.astype(F32))

    k0 = cache_k_a.reshape(ns, WINDOW, KV_W)
    v0 = cache_v_a.reshape(ns, WINDOW, KV_W)
    u0, c0, s0 = state_conv_b, state_conv_c, state_hgrn

    def layer_weights(l, w_in, b_in, w_o):
        ln = jnp.stack([ln1_g[l], ln1_b[l], ln2_g[l], ln2_b[l]]).astype(F32)
        return (w_in.astype(BF16), b_in.astype(F32)[None, :], w_o.astype(BF16), ln)

    w0 = layer_weights(0, ab_w_in, ab_b_in, ab_w_o)
    w1 = layer_weights(1, cd_w_in, cd_b_in, cd_w_o)
    w_gu, w_down = ffn_w_gu.astype(F32), ffn_w_down.astype(F32)

    hs0, k_s, v_s, u_s, *ffn0 = _l0_short(xs, rowmask, k0, v0, u0, jnp.asarray(kvb_short), a_consts, w0,
                                          w_gu, w_down, 0)
    hp0, k_p, v_p, u_p = _l0_main(x_prompt, k_s, v_s, u_s, jnp.asarray(kvb_main), a_consts,
                                  w0 + tuple(ffn0), meta_idx)

    cvec = jnp.stack([c_conv_b, c_ln_g, c_ln_b]).astype(F32)
    c_consts = (c_conv_w.astype(F32), cvec, d_lower_bounds.astype(F32), d_norm_g.astype(F32)[None, :])
    hs1, c_s, s_s, *ffn1 = _l1_short(hs0, rowmask, c0, s0, c_consts, w1, w_gu, w_down, 1)
    hp1, c_p, s_p = _l1_main(hp0, c_s, s_s, c_consts, w1 + tuple(ffn1), meta_idx)

    kv = lambda a: a.reshape(a.shape[0], WINDOW, A_KV_HEADS, HEAD_DIM)
    y_sample = hs1[:ns * CHUNK].reshape(ns, CHUNK, D_MODEL)
    return (hp1, y_sample, kv(k_p), kv(v_p), u_p, c_p, s_p,
            kv(k_s[:ns]), kv(v_s[:ns]), u_s[:ns], c_s[:ns], s_s[:ns])
```

```python
import functools

import jax
import jax.numpy as jnp
import numpy as np
from jax import lax
from jax.experimental import pallas as pl
from jax.experimental.pallas import tpu as pltpu

F32 = jnp.float32
BF16 = jnp.bfloat16

D_MODEL = 1024
CHUNK = 64
N_META = 16
PADF = CHUNK - N_META
A_HEADS = 8
A_KV_HEADS = 2
HEAD_DIM = 64
WINDOW = 128
KV_W = A_KV_HEADS * HEAD_DIM
Q_W = A_HEADS * HEAD_DIM
BAND = WINDOW + CHUNK
B_WIDTH = 512
B_CONV = 3
C_WIDTH = 512
C_CONV = 31
D_HEADS = 4
D_KEY = 128
D_VAL = 128
D_WIDTH = D_HEADS * D_VAL
SUB = 16
FFN_HIDDEN = 2816
DEPTH = 2
ALPHA = (2 * DEPTH) ** 0.25
LN_EPS = 1e-5
RMS_EPS = 1e-6
NEG = -1e30

AB_IN = Q_W + 2 * KV_W + 3 * B_WIDTH
CD_IN = 2 * C_WIDTH + 4 * D_WIDTH

TILE = 512
FFN_BLOCK = 256
POST_ROW_GROUPS = 2
L0_FFN_STEPS_BETWEEN = (POST_ROW_GROUPS,) * (2 * TILE // CHUNK)
L1_FFN_STEPS = tuple(POST_ROW_GROUPS * n for n in (6, 2, 2, 2))
SUBLANES = 8
LANES = 128
CONV_ROWS = 32
C_HIST = C_CONV - 1
C_OFF = 32
B_OFF = 8
VMEM_LIMIT = 60 * 1024 * 1024


def _dot(a, b):
    return jnp.dot(a.astype(BF16), b.astype(BF16), preferred_element_type=F32)


def _dot_nt(a, b):
    return lax.dot_general(a.astype(BF16), b.astype(BF16), (((1,), (1,)), ((), ())),
                           preferred_element_type=F32)


def _dot_tn(a, b):
    return lax.dot_general(a.astype(BF16), b.astype(BF16), (((0,), (0,)), ((), ())),
                           preferred_element_type=F32)


def _sigmoid(x):
    return 1.0 / (1.0 + jnp.exp(-x))


def _silu(x):
    return x * _sigmoid(x)


def _layer_norm(x, g, b):
    mu = jnp.mean(x, axis=-1, keepdims=True)
    xc = x - mu
    var = jnp.mean(xc * xc, axis=-1, keepdims=True)
    return xc * lax.rsqrt(var + LN_EPS) * g + b


class _PostBlock:
    N_BLOCKS = FFN_HIDDEN // FFN_BLOCK
    N_STEPS = 2 + N_BLOCKS

    def __init__(self, x, mix, w_o_ref, ln_ref, wg_ref, wu_ref, wd_ref):
        self.x, self.mix = x, mix
        self.w_o_ref, self.ln_ref = w_o_ref, ln_ref
        self.wg_ref, self.wu_ref, self.wd_ref = wg_ref, wu_ref, wd_ref
        self.done = 0
        self.h = self.hb = self.acc = self.act = None

    def _gate_up(self, j):
        lo = j * FFN_BLOCK
        gate = jnp.dot(self.hb, self.wg_ref[:, lo:lo + FFN_BLOCK], preferred_element_type=F32)
        up = jnp.dot(self.hb, self.wu_ref[:, lo:lo + FFN_BLOCK], preferred_element_type=F32)
        return (_silu(gate) * up).astype(BF16)

    def _step(self, j):
        if j == 0:
            ln = self.ln_ref
            self.h = _layer_norm(ALPHA * self.x + _dot(self.mix, self.w_o_ref[...]), ln[0:1], ln[1:2])
            self.hb = self.h.astype(BF16)
            return
        prev = self.act
        self.act = self._gate_up(j - 1) if j - 1 < self.N_BLOCKS else None
        if prev is not None:
            lo = (j - 2) * FFN_BLOCK
            part = jnp.dot(prev, self.wd_ref[lo:lo + FFN_BLOCK, :], preferred_element_type=F32)
            self.acc = part if self.acc is None else self.acc + part

    def run(self, n):
        for _ in range(n):
            if self.done < self.N_STEPS:
                self._step(self.done)
                self.done += 1

    def result(self):
        self.run(self.N_STEPS)
        return _layer_norm(ALPHA * self.h + self.acc, self.ln_ref[2:3], self.ln_ref[3:4])


class _SplitPostBlock:
    def __init__(self, x, mix, weight_refs, groups):
        rows = x.shape[0] // groups
        self.parts = [_PostBlock(x[g * rows:(g + 1) * rows], mix[g * rows:(g + 1) * rows], *weight_refs)
                      for g in range(groups)]
        self.outs = []

    def run(self, n):
        for _ in range(n):
            if len(self.outs) == len(self.parts):
                return
            part = self.parts[len(self.outs)]
            part.run(1)
            if part.done == part.N_STEPS:
                self.outs.append(part.result())

    def result(self):
        self.run(sum(part.N_STEPS for part in self.parts))
        return jnp.concatenate(self.outs, axis=0)


def _no_op():
    pass


def _attn_chunks(chunks, alibi_ref, sink_ref, between=_no_op):
    lane = lax.broadcasted_iota(jnp.int32, (BAND, KV_W), 1)
    low = lane < HEAD_DIM
    high = jnp.logical_not(low)
    sink_rows = [jnp.where(low[0:1], sink_ref[4 * g + e], sink_ref[4 * g + 2 + e])
                 for g in range(A_KV_HEADS) for e in range(2)]

    def extend(band, swapped, g, e):
        src = band if g == e else swapped
        return jnp.where(low if e == 0 else high, src, 0.0).astype(BF16)

    scores = []
    for q, kband, _, kvbias in chunks:
        kswap = pltpu.roll(kband, HEAD_DIM, axis=1)
        for g in range(A_KV_HEADS):
            qg = jnp.concatenate([q[:, 256 * g:256 * g + 128], q[:, 256 * g + 128:256 * g + 256]],
                                 axis=0).astype(BF16)
            k2 = jnp.concatenate([extend(kband, kswap, g, e) for e in range(2)], axis=0)
            s = _dot_nt(k2, qg) + alibi_ref[g]
            for e in range(2):
                se = s[BAND * e:BAND * (e + 1)]
                scores.append(se if kvbias is None else se + kvbias)
        between()

    probs = []
    for idx, s in enumerate(scores):
        sink = sink_rows[idx % (2 * A_KV_HEADS)]
        m = jnp.maximum(jnp.max(s, axis=0, keepdims=True), sink)
        p = jnp.exp(s - m)
        den = jnp.sum(p, axis=0, keepdims=True) + jnp.exp(sink - m)
        probs.append((p * (1.0 / den)).astype(BF16))
        if idx % (2 * A_KV_HEADS) == 2 * A_KV_HEADS - 1:
            between()

    outs = []
    for ci, (_, _, vband, _) in enumerate(chunks):
        vswap = pltpu.roll(vband, HEAD_DIM, axis=1)
        parts = []
        for g in range(A_KV_HEADS):
            base = ci * 2 * A_KV_HEADS + 2 * g
            p2 = jnp.concatenate([probs[base], probs[base + 1]], axis=0)
            v2 = jnp.concatenate([extend(vband, vswap, g, e) for e in range(2)], axis=0)
            o = _dot_tn(p2, v2)
            parts += [o[0:CHUNK], o[CHUNK:2 * CHUNK]]
        outs.append(jnp.concatenate(parts, axis=1))
    return outs


def _ab_project(x, w_in_ref, b_in_ref):
    proj = _dot(x, w_in_ref[...]) + b_in_ref[...]
    q = proj[:, 0:Q_W] * (HEAD_DIM ** -0.5)
    k = proj[:, Q_W:Q_W + KV_W]
    v = proj[:, Q_W + KV_W:Q_W + 2 * KV_W]
    o = Q_W + 2 * KV_W
    bg = proj[:, o:o + B_WIDTH]
    u = proj[:, o + B_WIDTH:o + 2 * B_WIDTH] * proj[:, o + 2 * B_WIDTH:o + 3 * B_WIDTH]
    return q, k, v, bg, u


def _short_conv(u_ref, base, rows, w_ref):
    out = None
    for j in range(B_CONV):
        term = u_ref[pl.ds(base - (B_CONV - 1) + j, rows), :] * w_ref[j:j + 1, :]
        out = term if out is None else out + term
    return out


def _streamed_residual_block(first, x_ref, mix_s, w_o_ref, ln_ref, wg_ref, wu_ref, wd_ref,
                             h_ref, wg_out, wu_out, wd_out, h_s, hb_s, acc_s):
    j = pl.program_id(0)

    @pl.when(j == 0)
    def _():
        first()
        h = _layer_norm(ALPHA * x_ref[...] + _dot(mix_s[...], w_o_ref[...]), ln_ref[0:1], ln_ref[1:2])
        h_s[...] = h
        hb_s[...] = h.astype(BF16)
        acc_s[...] = jnp.zeros_like(acc_s)

    wg = wg_ref[0].astype(BF16)
    wu = wu_ref[0].astype(BF16)
    wd = wd_ref[0].astype(BF16)
    wg_out[...] = wg
    wu_out[...] = wu
    wd_out[...] = wd
    hb = hb_s[...]
    gate = jnp.dot(hb, wg, preferred_element_type=F32)
    up = jnp.dot(hb, wu, preferred_element_type=F32)
    acc_s[...] += jnp.dot((_silu(gate) * up).astype(BF16), wd, preferred_element_type=F32)

    @pl.when(j == pl.num_programs(0) - 1)
    def _():
        h_ref[...] = _layer_norm(ALPHA * h_s[...] + acc_s[...], ln_ref[2:3], ln_ref[3:4])


def _l0_short_kernel(x_ref, rowmask_ref, k0_ref, v0_ref, u0_ref, kvb_ref, alibi_ref, sink_ref,
                     convw_ref, w_in_ref, b_in_ref, w_o_ref, ln_ref, wg_ref, wu_ref, wd_ref,
                     h_ref, kout_ref, vout_ref, uout_ref, wg_out, wu_out, wd_out,
                     u_s, mix_s, h_s, hb_s, acc_s):
    mixers = functools.partial(_l0_short_mixers, x_ref, rowmask_ref, k0_ref, v0_ref, u0_ref, kvb_ref,
                               alibi_ref, sink_ref, convw_ref, w_in_ref, b_in_ref,
                               kout_ref, vout_ref, uout_ref, u_s, mix_s)
    _streamed_residual_block(mixers, x_ref, mix_s, w_o_ref, ln_ref, wg_ref, wu_ref, wd_ref,
                             h_ref, wg_out, wu_out, wd_out, h_s, hb_s, acc_s)


def _l0_short_mixers(x_ref, rowmask_ref, k0_ref, v0_ref, u0_ref, kvb_ref, alibi_ref, sink_ref,
                     convw_ref, w_in_ref, b_in_ref, kout_ref, vout_ref, uout_ref, u_s, mix_s):
    ns = k0_ref.shape[0] + 1
    x = x_ref[...]
    q, k, v, bg, u = _ab_project(x, w_in_ref, b_in_ref)
    u = u * rowmask_ref[...]

    chunks = []
    for s in range(ns):
        r0 = s * CHUNK
        meta = s == ns - 1
        empty = jnp.zeros((WINDOW, KV_W), F32)
        kband = jnp.concatenate([empty if meta else k0_ref[s], k[r0:r0 + CHUNK]], axis=0)
        vband = jnp.concatenate([empty if meta else v0_ref[s], v[r0:r0 + CHUNK]], axis=0)
        kout_ref[s] = kband[CHUNK:]
        vout_ref[s] = vband[CHUNK:]
        chunks.append((q[r0:r0 + CHUNK], kband, vband, kvb_ref[...] if s == ns - 1 else None))
    for s, att in enumerate(_attn_chunks(chunks, alibi_ref, sink_ref)):
        mix_s[s * CHUNK:(s + 1) * CHUNK, 0:Q_W] = att

    for s in range(ns):
        r0 = s * CHUNK
        us = u_s.at[s]
        us[B_OFF - 2:B_OFF, :] = jnp.zeros((B_CONV - 1, B_WIDTH), F32) if s == ns - 1 else u0_ref[s]
        us[B_OFF:B_OFF + CHUNK, :] = u[r0:r0 + CHUNK]
        cb = _short_conv(us, B_OFF, CHUNK, convw_ref)
        mix_s[r0:r0 + CHUNK, Q_W:] = bg[r0:r0 + CHUNK] * cb
        uout_ref[s] = u[r0 + CHUNK - 2:r0 + CHUNK]


def _tile_position(nt):
    i = pl.program_id(0)
    n = pl.num_programs(0) - 1
    return i, n, lax.rem(jnp.minimum(i, n - 1), nt)


def _zero_on_first_step(i, mix_s):
    @pl.when(i == 0)
    def _():
        mix_s[...] = jnp.zeros_like(mix_s)


def _l0_main_kernel(nt, x_ref, xprev_ref, k0_ref, v0_ref, u0_ref, kvb_ref, alibi_ref, sink_ref,
                    convw_ref, w_in_ref, b_in_ref, w_o_ref, ln_ref, wg_ref, wu_ref, wd_ref,
                    h_ref, kout_ref, vout_ref, uout_ref,
                    k_s, v_s, u_s, mix_s):
    i, n, t = _tile_position(nt)
    tm = x_ref.shape[1]

    @pl.when(t == 0)
    def _():
        k_s[0:WINDOW, :] = k0_ref[0]
        v_s[0:WINDOW, :] = v0_ref[0]
        u_s[B_OFF - 2:B_OFF, :] = u0_ref[0]

    _zero_on_first_step(i, mix_s)

    post = _SplitPostBlock(xprev_ref[0], mix_s[...], (w_o_ref, ln_ref, wg_ref, wu_ref, wd_ref), POST_ROW_GROUPS)
    post.run(1)

    q, k, v, bg, u = _ab_project(x_ref[0], w_in_ref, b_in_ref)
    k_s[WINDOW:, :] = k
    v_s[WINDOW:, :] = v
    u_s[B_OFF:, :] = u
    mix_s[:, Q_W:] = bg * _short_conv(u_s, B_OFF, tm, convw_ref)

    first = t == 0
    chunks = []
    for c in range(tm // CHUNK):
        r0 = c * CHUNK
        kvbias = jnp.where(first, kvb_ref[c], 0.0) if c < 2 else None
        chunks.append((q[r0:r0 + CHUNK], k_s[r0:r0 + BAND, :], v_s[r0:r0 + BAND, :], kvbias))
    plan = iter(L0_FFN_STEPS_BETWEEN)
    outs = _attn_chunks(chunks, alibi_ref, sink_ref, between=lambda: post.run(next(plan, 0)))
    for c, att in enumerate(outs):
        mix_s[c * CHUNK:(c + 1) * CHUNK, 0:Q_W] = att

    h_ref[0] = post.result()

    k_s[0:WINDOW, :] = k_s[tm:tm + WINDOW, :]
    v_s[0:WINDOW, :] = v_s[tm:tm + WINDOW, :]
    u_s[B_OFF - 2:B_OFF, :] = u_s[B_OFF + tm - 2:B_OFF + tm, :]

    @pl.when(jnp.logical_and(t == nt - 1, i < n))
    def _():
        kout_ref[0] = k_s[0:WINDOW, :]
        vout_ref[0] = v_s[0:WINDOW, :]
        uout_ref[0] = u_s[B_OFF - 2:B_OFF, :]


def _c_project(x, rowmask, w_in_ref, b_in_ref):
    w = C_WIDTH
    proj = _dot(x, w_in_ref[:, 0:2 * w]) + b_in_ref[:, 0:2 * w]
    u = proj[:, 0:w] * _sigmoid(proj[:, w:2 * w])
    return u if rowmask is None else u * rowmask


def _d_project(x, rowmask, w_in_ref, b_in_ref, lb_ref):
    w = D_WIDTH
    proj = _dot(x, w_in_ref[:, 2 * C_WIDTH:]) + b_in_ref[:, 2 * C_WIDTH:]
    q = proj[:, 0:w]
    d0 = lb_ref[0:1]
    d1 = lb_ref[1:2]
    mx = jnp.maximum(d0, d1)
    e0 = jnp.exp(d0 - mx)
    e1 = jnp.exp(d1 - mx)
    p0 = e0 / (e0 + e1)
    p1 = e1 / (e0 + e1)
    lb = (p0 + p1) - p0
    forget = lb + (1.0 - lb) * _sigmoid(proj[:, w:2 * w])
    kk = 1.0 - forget
    lf = jnp.log(forget)
    v = proj[:, 2 * w:3 * w]
    g = proj[:, 3 * w:4 * w]
    if rowmask is not None:
        q = q * rowmask
        kk = kk * rowmask
        v = v * rowmask
        lf = lf * rowmask
    return q, kk, v, lf, g


def _conformer_conv(c_ref, base, rows, convw_ref, cvec_ref, out_ref, out_base):
    for r0 in range(0, rows, CONV_ROWS):
        cols = []
        for c0 in range(0, C_WIDTH, LANES):
            lanes = slice(c0, c0 + LANES)
            win = c_ref[base - C_OFF + r0:base + r0 + CONV_ROWS, lanes]
            acc = None
            for phase in range(SUBLANES):
                shifted = win if phase == 0 else pltpu.roll(win, CONV_ROWS + C_OFF - phase, axis=0)
                for j in range(C_CONV):
                    off = C_OFF - C_HIST + j
                    if off % SUBLANES != phase:
                        continue
                    lo = off - phase
                    term = shifted[lo:lo + CONV_ROWS] * convw_ref[j:j + 1, lanes]
                    acc = term if acc is None else acc + term
            cols.append(acc)
        acc = jnp.concatenate(cols, axis=1)
        y = _layer_norm(acc + cvec_ref[0:1], cvec_ref[1:2], cvec_ref[2:3])
        out_ref[out_base + r0:out_base + r0 + CONV_ROWS, 0:C_WIDTH] = _silu(y)


def _hgrn_chunks(chunks, init_states, between=_no_op):
    row = lax.broadcasted_iota(jnp.int32, (CHUNK, CHUNK), 0)
    col = lax.broadcasted_iota(jnp.int32, (CHUNK, CHUNK), 1)
    tril = col <= row
    ltri = jnp.where(tril, 1.0, 0.0).astype(BF16)
    nsub = CHUNK // SUB
    rsub = jnp.right_shift(lax.broadcasted_iota(jnp.int32, (CHUNK, D_KEY), 0), SUB.bit_length() - 1)
    heads = [slice(D_KEY * h, D_KEY * (h + 1)) for h in range(D_HEADS)]

    cums = []
    for _, _, _, lf in chunks:
        hi = lf.astype(BF16)
        r1 = lf - hi.astype(F32)
        mid = r1.astype(BF16)
        lo = (r1 - mid.astype(F32)).astype(BF16)
        cums.append(jnp.dot(ltri, hi, preferred_element_type=F32)
                    + jnp.dot(ltri, mid, preferred_element_type=F32)
                    + jnp.dot(ltri, lo, preferred_element_type=F32))

    between()
    pre = []
    for (q, kk, v, _), cum in zip(chunks, cums):
        tot = cum[CHUNK - 1:CHUNK]
        bases = [jnp.zeros((1, D_WIDTH), F32)] + [cum[SUB * i - 1:SUB * i] for i in range(1, nsub)]
        bsel = jnp.concatenate([jnp.broadcast_to(b, (SUB, D_WIDTH)) for b in bases], axis=0)
        qe = (q * jnp.exp(cum)).astype(BF16)
        qd = q * jnp.exp(cum - bsel)
        kdec = (kk * jnp.exp(tot - cum)).astype(BF16)
        qblks, kdcs = [], []
        for sl in heads:
            qd_h = qd[:, sl]
            qblks.append(jnp.concatenate([jnp.where(rsub == i, qd_h, 0.0) for i in range(nsub)],
                                         axis=1).astype(BF16))
            kds = []
            for i in range(nsub):
                n = SUB * (i + 1)
                kd = kk[0:n, sl] * jnp.exp(bases[i][:, sl] - cum[0:n, sl])
                if n < CHUNK:
                    kd = jnp.concatenate([kd, jnp.zeros((CHUNK - n, D_KEY), F32)], axis=0)
                kds.append(kd)
            kdcs.append(jnp.concatenate(kds, axis=1).astype(BF16))
        pre.append((qe, kdec, v.astype(BF16), jnp.exp(tot), qblks, kdcs))

    atts, incs = [], []
    for qe, kdec, vb, etot, qblks, kdcs in pre:
        atts.append([jnp.where(tril, _dot_nt(qblks[h], kdcs[h]), 0.0).astype(BF16) for h in range(D_HEADS)])
        incs.append([_dot_tn(vb[:, sl], kdec[:, sl]) for sl in heads])

    between()
    intra = [[_dot(atts[c][h], pre[c][2][:, heads[h]]) for h in range(D_HEADS)] for c in range(len(chunks))]

    between()
    outs, ends = [], []
    state = None
    for c, (qe, kdec, vb, etot, _, _) in enumerate(pre):
        if init_states[c] is not None:
            state = init_states[c]
        o = [_dot_nt(qe[:, heads[h]], state[h]) + intra[c][h] for h in range(D_HEADS)]
        state = [etot[:, heads[h]] * state[h] + incs[c][h] for h in range(D_HEADS)]
        outs.append(jnp.concatenate(o, axis=1))
        ends.append(state)
    return outs, ends


def _hgrn_readout(o, g, norm_g):
    parts = []
    for h in range(D_HEADS):
        oh = o[:, D_VAL * h:D_VAL * (h + 1)]
        parts.append(oh * lax.rsqrt(jnp.mean(oh * oh, axis=-1, keepdims=True) + RMS_EPS))
    return jnp.concatenate(parts, axis=1) * norm_g * _silu(g)


def _l1_short_kernel(x_ref, rowmask_ref, c0_ref, s0_ref, convw_ref, cvec_ref, lb_ref, normg_ref,
                     w_in_ref, b_in_ref, w_o_ref, ln_ref, wg_ref, wu_ref, wd_ref,
                     h_ref, cout_ref, sout_ref, wg_out, wu_out, wd_out,
                     c_s, mix_s, h_s, hb_s, acc_s):
    mixers = functools.partial(_l1_short_mixers, x_ref, rowmask_ref, c0_ref, s0_ref, convw_ref, cvec_ref,
                               lb_ref, normg_ref, w_in_ref, b_in_ref, cout_ref, sout_ref, c_s, mix_s)
    _streamed_residual_block(mixers, x_ref, mix_s, w_o_ref, ln_ref, wg_ref, wu_ref, wd_ref,
                             h_ref, wg_out, wu_out, wd_out, h_s, hb_s, acc_s)


def _l1_short_mixers(x_ref, rowmask_ref, c0_ref, s0_ref, convw_ref, cvec_ref, lb_ref, normg_ref,
                     w_in_ref, b_in_ref, cout_ref, sout_ref, c_s, mix_s):
    ns = c0_ref.shape[0] + 1
    x = x_ref[...]
    rowmask = rowmask_ref[...]
    u = _c_project(x, rowmask, w_in_ref, b_in_ref)
    q, kk, v, lf, g = _d_project(x, rowmask, w_in_ref, b_in_ref, lb_ref)

    for s in range(ns):
        r0 = s * CHUNK
        cs = c_s.at[s]
        cs[C_OFF - C_HIST:C_OFF, :] = jnp.zeros((C_HIST, C_WIDTH), F32) if s == ns - 1 else c0_ref[s]
        cs[C_OFF:C_OFF + CHUNK, :] = u[r0:r0 + CHUNK]
        _conformer_conv(cs, C_OFF, CHUNK, convw_ref, cvec_ref, mix_s, r0)
        cout_ref[s] = cs[C_OFF + CHUNK - C_HIST:C_OFF + CHUNK, :]

    rows = [slice(s * CHUNK, (s + 1) * CHUNK) for s in range(ns)]
    outs, ends = _hgrn_chunks([(q[r], kk[r], v[r], lf[r]) for r in rows],
                              [[jnp.zeros((D_VAL, D_KEY), F32) if s == ns - 1 else s0_ref[s, h].T
                                for h in range(D_HEADS)] for s in range(ns)])
    for s in range(ns):
        mix_s[rows[s], C_WIDTH:] = _hgrn_readout(outs[s], g[rows[s]], normg_ref[...])
        for h in range(D_HEADS):
            sout_ref[s, h] = ends[s][h].T


def _l1_main_kernel(nt, x_ref, xprev_ref, c0_ref, s0_ref, convw_ref, cvec_ref, lb_ref, normg_ref,
                    w_in_ref, b_in_ref, w_o_ref, ln_ref, wg_ref, wu_ref, wd_ref,
                    h_ref, cout_ref, sout_ref,
                    c_s, st_s, mix_s):
    i, n, t = _tile_position(nt)
    tm = x_ref.shape[1]

    @pl.when(t == 0)
    def _():
        c_s[C_OFF - C_HIST:C_OFF, :] = c0_ref[0]
        for h in range(D_HEADS):
            st_s[h] = s0_ref[0, h].T

    _zero_on_first_step(i, mix_s)

    post = _SplitPostBlock(xprev_ref[0], mix_s[...], (w_o_ref, ln_ref, wg_ref, wu_ref, wd_ref), POST_ROW_GROUPS)
    post.run(1)

    x = x_ref[0]
    plan = iter(L1_FFN_STEPS)
    between = lambda: post.run(next(plan, 0))
    c_s[C_OFF:, :] = _c_project(x, None, w_in_ref, b_in_ref)
    q, kk, v, lf, g = _d_project(x, None, w_in_ref, b_in_ref, lb_ref)
    between()
    _conformer_conv(c_s, C_OFF, tm, convw_ref, cvec_ref, mix_s, 0)
    c_s[C_OFF - C_HIST:C_OFF, :] = c_s[C_OFF + tm - C_HIST:C_OFF + tm, :]

    nc = tm // CHUNK
    rows = [slice(c * CHUNK, (c + 1) * CHUNK) for c in range(nc)]
    outs, ends = _hgrn_chunks([(q[r], kk[r], v[r], lf[r]) for r in rows],
                              [[st_s[h] for h in range(D_HEADS)]] + [None] * (nc - 1), between)
    for c in range(nc):
        mix_s[rows[c], C_WIDTH:] = _hgrn_readout(outs[c], g[rows[c]], normg_ref[...])
    for h in range(D_HEADS):
        st_s[h] = ends[-1][h]

    h_ref[0] = post.result()

    @pl.when(jnp.logical_and(t == nt - 1, i < n))
    def _():
        cout_ref[0] = c_s[C_OFF - C_HIST:C_OFF, :]
        for h in range(D_HEADS):
            sout_ref[0, h] = st_s[h].T


def _whole(shape):
    zeros = (0,) * len(shape)
    return pl.BlockSpec(shape, lambda *_: zeros, pipeline_mode=pl.Buffered(1))


def _const_spec(a):
    return pl.BlockSpec(memory_space=pltpu.SMEM) if a.ndim == 1 else _whole(a.shape)


N_FFN_BLOCKS = FFN_HIDDEN // FFN_BLOCK


def _streamed_ffn_specs(layer):
    in_specs = [pl.BlockSpec((1, D_MODEL, FFN_BLOCK), lambda j: (layer, 0, j)),
                pl.BlockSpec((1, D_MODEL, FFN_BLOCK), lambda j: (layer, 0, N_FFN_BLOCKS + j)),
                pl.BlockSpec((1, FFN_BLOCK, D_MODEL), lambda j: (layer, j, 0))]
    out_specs = [pl.BlockSpec((D_MODEL, FFN_BLOCK), lambda j: (0, j)),
                 pl.BlockSpec((D_MODEL, FFN_BLOCK), lambda j: (0, j)),
                 pl.BlockSpec((FFN_BLOCK, D_MODEL), lambda j: (j, 0))]
    out_shape = [jax.ShapeDtypeStruct((D_MODEL, FFN_HIDDEN), BF16),
                 jax.ShapeDtypeStruct((D_MODEL, FFN_HIDDEN), BF16),
                 jax.ShapeDtypeStruct((FFN_HIDDEN, D_MODEL), BF16)]
    return in_specs, out_specs, out_shape


def _residual_scratch(rows):
    return [pltpu.VMEM((rows, D_MODEL), F32), pltpu.VMEM((rows, D_MODEL), BF16), pltpu.VMEM((rows, D_MODEL), F32)]


def _whole_out(shape):
    zeros = (0,) * len(shape)
    return pl.BlockSpec(shape, lambda *_: zeros)


def _params(n_axes):
    return pltpu.CompilerParams(dimension_semantics=("arbitrary",) * n_axes,
                                vmem_limit_bytes=VMEM_LIMIT)


def _l0_short(x, rowmask, k0, v0, u0, kvb, consts, weights, w_gu, w_down, layer):
    rows, ns = x.shape[0], k0.shape[0] + 1
    ins = (x, rowmask, k0, v0, u0, kvb) + consts + weights
    ffn_in, ffn_out, ffn_shape = _streamed_ffn_specs(layer)
    return pl.pallas_call(
        _l0_short_kernel,
        grid=(N_FFN_BLOCKS,),
        in_specs=[_const_spec(a) for a in ins] + ffn_in,
        out_specs=[_whole_out((rows, D_MODEL)), _whole_out((ns, WINDOW, KV_W)),
                   _whole_out((ns, WINDOW, KV_W)), _whole_out((ns, B_CONV - 1, B_WIDTH))] + ffn_out,
        out_shape=[jax.ShapeDtypeStruct((rows, D_MODEL), F32),
                   jax.ShapeDtypeStruct((ns, WINDOW, KV_W), F32),
                   jax.ShapeDtypeStruct((ns, WINDOW, KV_W), F32),
                   jax.ShapeDtypeStruct((ns, B_CONV - 1, B_WIDTH), F32)] + ffn_shape,
        scratch_shapes=[pltpu.VMEM((ns, B_OFF + CHUNK, B_WIDTH), F32), pltpu.VMEM((rows, D_MODEL), F32)]
        + _residual_scratch(rows),
        compiler_params=_params(1),
        name="l0_short",
    )(*ins, w_gu, w_gu, w_down)


class _SkewedTiles:
    def __init__(self, nb, nt):
        self.nt, self.n = nt, nb * nt

    def _cur(self, i):
        return jnp.minimum(i, self.n - 1)

    def tile(self):
        return pl.BlockSpec((1, TILE, D_MODEL), lambda i: (self._cur(i) // self.nt, self._cur(i) % self.nt, 0))

    def prev_tile(self):
        prev = lambda i: jnp.maximum(i - 1, 0)
        return pl.BlockSpec((1, TILE, D_MODEL), lambda i: (prev(i) // self.nt, prev(i) % self.nt, 0))

    def fixed(self, shape, idx):
        zeros = (0,) * len(shape)
        return pl.BlockSpec((1,) + shape, lambda i: (idx,) + zeros)

    def per_prompt(self, shape):
        zeros = (0,) * len(shape)
        return pl.BlockSpec((1,) + shape, lambda i: (self._cur(i) // self.nt,) + zeros)


def _l0_main(x, k0, v0, u0, kvb, consts, weights, meta_idx):
    nb, seq, _ = x.shape
    nt = seq // TILE
    sk = _SkewedTiles(nb, nt)
    ins = (x, x, k0, v0, u0, kvb) + consts + weights
    in_specs = [sk.tile(), sk.prev_tile(),
                sk.fixed((WINDOW, KV_W), meta_idx), sk.fixed((WINDOW, KV_W), meta_idx),
                sk.fixed((B_CONV - 1, B_WIDTH), meta_idx),
                _whole(kvb.shape)] + [_const_spec(a) for a in consts + weights]
    return pl.pallas_call(
        functools.partial(_l0_main_kernel, nt),
        grid=(nb * nt + 1,),
        in_specs=in_specs,
        out_specs=[sk.prev_tile(), sk.per_prompt((WINDOW, KV_W)), sk.per_prompt((WINDOW, KV_W)),
                   sk.per_prompt((B_CONV - 1, B_WIDTH))],
        out_shape=[jax.ShapeDtypeStruct((nb, seq, D_MODEL), F32),
                   jax.ShapeDtypeStruct((nb, WINDOW, KV_W), F32),
                   jax.ShapeDtypeStruct((nb, WINDOW, KV_W), F32),
                   jax.ShapeDtypeStruct((nb, B_CONV - 1, B_WIDTH), F32)],
        scratch_shapes=[pltpu.VMEM((WINDOW + TILE, KV_W), F32),
                        pltpu.VMEM((WINDOW + TILE, KV_W), F32), pltpu.VMEM((B_OFF + TILE, B_WIDTH), F32),
                        pltpu.VMEM((TILE, D_MODEL), F32)],
        compiler_params=_params(1),
        name="l0_main",
    )(*ins)


def _l1_short(x, rowmask, c0, s0, consts, weights, w_gu, w_down, layer):
    rows, ns = x.shape[0], c0.shape[0] + 1
    ins = (x, rowmask, c0, s0) + consts + weights
    ffn_in, ffn_out, ffn_shape = _streamed_ffn_specs(layer)
    return pl.pallas_call(
        _l1_short_kernel,
        grid=(N_FFN_BLOCKS,),
        in_specs=[_const_spec(a) for a in ins] + ffn_in,
        out_specs=[_whole_out((rows, D_MODEL)), _whole_out((ns, C_HIST, C_WIDTH)),
                   _whole_out((ns, D_HEADS, D_KEY, D_VAL))] + ffn_out,
        out_shape=[jax.ShapeDtypeStruct((rows, D_MODEL), F32),
                   jax.ShapeDtypeStruct((ns, C_HIST, C_WIDTH), F32),
                   jax.ShapeDtypeStruct((ns, D_HEADS, D_KEY, D_VAL), F32)] + ffn_shape,
        scratch_shapes=[pltpu.VMEM((ns, C_OFF + CHUNK, C_WIDTH), F32), pltpu.VMEM((rows, D_MODEL), F32)]
        + _residual_scratch(rows),
        compiler_params=_params(1),
        name="l1_short",
    )(*ins, w_gu, w_gu, w_down)


def _l1_main(x, c0, s0, consts, weights, meta_idx):
    nb, seq, _ = x.shape
    nt = seq // TILE
    sk = _SkewedTiles(nb, nt)
    ins = (x, x, c0, s0) + consts + weights
    in_specs = [sk.tile(), sk.prev_tile(), sk.fixed((C_HIST, C_WIDTH), meta_idx),
                sk.fixed((D_HEADS, D_KEY, D_VAL), meta_idx),
                ] + [_const_spec(a) for a in consts + weights]
    return pl.pallas_call(
        functools.partial(_l1_main_kernel, nt),
        grid=(nb * nt + 1,),
        in_specs=in_specs,
        out_specs=[sk.prev_tile(), sk.per_prompt((C_HIST, C_WIDTH)),
                   sk.per_prompt((D_HEADS, D_KEY, D_VAL))],
        out_shape=[jax.ShapeDtypeStruct((nb, seq, D_MODEL), F32),
                   jax.ShapeDtypeStruct((nb, C_HIST, C_WIDTH), F32),
                   jax.ShapeDtypeStruct((nb, D_HEADS, D_KEY, D_VAL), F32)],
        scratch_shapes=[pltpu.VMEM((C_OFF + TILE, C_WIDTH), F32), pltpu.VMEM((D_HEADS, D_VAL, D_KEY), F32),
                        pltpu.VMEM((TILE, D_MODEL), F32)],
        compiler_params=_params(1),
        name="l1_main",
    )(*ins)


def _alibi_tables():
    slopes = np.exp2(-8.0 * np.arange(1, A_HEADS + 1, dtype=np.float32) / A_HEADS).astype(np.float32)
    i = np.arange(CHUNK, dtype=np.float32)[None, :]
    j = np.arange(BAND, dtype=np.float32)[:, None]
    dist = np.abs(WINDOW + i - j).astype(np.float32)
    bias = np.zeros((A_KV_HEADS, 2 * BAND, 2 * CHUNK), np.float32)
    for g in range(A_KV_HEADS):
        for e in range(2):
            for half, head in enumerate((4 * g + e, 4 * g + 2 + e)):
                cols = slice(CHUNK * half, CHUNK * (half + 1))
                bias[g, BAND * e:BAND * (e + 1), cols] = -slopes[head] * dist
    return bias


def kernel(x_prompt, x_sample, cache_k_a, cache_v_a, state_conv_b, state_conv_c, state_hgrn, meta_tokens,
           ab_w_in, ab_b_in, a_sinks, b_conv_w, ab_w_o, cd_w_in, cd_b_in, c_conv_w, c_conv_b, c_ln_g,
           c_ln_b, d_lower_bounds, d_norm_g, cd_w_o, ln1_g, ln1_b, ln2_g, ln2_b, ffn_w_gu, ffn_w_down):
    nb = x_prompt.shape[0]
    ns = x_sample.shape[0]
    assert x_sample.shape[1] == CHUNK and cache_k_a.shape[1] == WINDOW
    assert x_prompt.shape[1] % TILE == 0
    meta_idx = ns
    n_short = ns + 1
    rows = n_short * CHUNK

    xs = jnp.concatenate([x_sample.reshape(ns * CHUNK, D_MODEL),
                          jnp.zeros((PADF, D_MODEL), F32), meta_tokens.astype(F32)], axis=0)
    rowmask = np.ones((rows, 1), np.float32)
    rowmask[ns * CHUNK:ns * CHUNK + PADF] = 0.0
    rowmask = jnp.asarray(rowmask)

    kvb_short = np.zeros((BAND, 2 * CHUNK), np.float32)
    kvb_short[:WINDOW + PADF] = NEG
    kvb_main = np.zeros((2, BAND, 2 * CHUNK), np.float32)
    kvb_main[0, :CHUNK + PADF] = NEG
    kvb_main[1, :PADF] = NEG

    a_consts = (jnp.asarray(_alibi_tables()), a_sinks.astype(F32), b_conv_w.astype(F32))

    k0 = cache_k_a.reshape(ns, WINDOW, KV_W)
    v0 = cache_v_a.reshape(ns, WINDOW, KV_W)
    u0, c0, s0 = state_conv_b, state_conv_c, state_hgrn

    def layer_weights(l, w_in, b_in, w_o):
        ln = jnp.stack([ln1_g[l], ln1_b[l], ln2_g[l], ln2_b[l]]).astype(F32)
        return (w_in.astype(BF16), b_in.astype(F32)[None, :], w_o.astype(BF16), ln)

    w0 = layer_weights(0, ab_w_in, ab_b_in, ab_w_o)
    w1 = layer_weights(1, cd_w_in, cd_b_in, cd_w_o)
    w_gu, w_down = ffn_w_gu.astype(F32), ffn_w_down.astype(F32)

    hs0, k_s, v_s, u_s, *ffn0 = _l0_short(xs, rowmask, k0, v0, u0, jnp.asarray(kvb_short), a_consts, w0,
                                          w_gu, w_down, 0)
    hp0, k_p, v_p, u_p = _l0_main(x_prompt, k_s, v_s, u_s, jnp.asarray(kvb_main), a_consts,
                                  w0 + tuple(ffn0), meta_idx)

    cvec = jnp.stack([c_conv_b, c_ln_g, c_ln_b]).astype(F32)
    c_consts = (c_conv_w.astype(F32), cvec, d_lower_bounds.astype(F32), d_norm_g.astype(F32)[None, :])
    hs1, c_s, s_s, *ffn1 = _l1_short(hs0, rowmask, c0, s0, c_consts, w1, w_gu, w_down, 1)
    hp1, c_p, s_p = _l1_main(hp0, c_s, s_s, c_consts, w1 + tuple(ffn1), meta_idx)

    kv = lambda a: a.reshape(a.shape[0], WINDOW, A_KV_HEADS, HEAD_DIM)
    y_sample = hs1[:ns * CHUNK].reshape(ns, CHUNK, D_MODEL)
    return (hp1, y_sample, kv(k_p), kv(v_p), u_p, c_p, s_p,
            kv(k_s[:ns]), kv(v_s[:ns]), u_s[:ns], c_s[:ns], s_s[:ns])
```

```python
import functools

import jax
import jax.numpy as jnp
import numpy as np
from jax import lax
from jax.experimental import pallas as pl
from jax.experimental.pallas import tpu as pltpu

F32 = jnp.float32
BF16 = jnp.bfloat16

D_MODEL = 1024
CHUNK = 64
N_META = 16
PADF = CHUNK - N_META
A_HEADS = 8
A_KV_HEADS = 2
HEAD_DIM = 64
WINDOW = 128
KV_W = A_KV_HEADS * HEAD_DIM
Q_W = A_HEADS * HEAD_DIM
BAND = WINDOW + CHUNK
B_WIDTH = 512
B_CONV = 3
C_WIDTH = 512
C_CONV = 31
D_HEADS = 4
D_KEY = 128
D_VAL = 128
D_WIDTH = D_HEADS * D_VAL
SUB = 16
FFN_HIDDEN = 2816
DEPTH = 2
ALPHA = (2 * DEPTH) ** 0.25
LN_EPS = 1e-5
RMS_EPS = 1e-6
NEG = -1e30

AB_IN = Q_W + 2 * KV_W + 3 * B_WIDTH
CD_IN = 2 * C_WIDTH + 4 * D_WIDTH

TILE = 512
FFN_BLOCK = 256
POST_ROW_GROUPS = 2
L0_FFN_STEPS_BETWEEN = (POST_ROW_GROUPS,) * (2 * TILE // CHUNK)
L1_FFN_STEPS = tuple(POST_ROW_GROUPS * n for n in (6, 2, 2, 2))
SUBLANES = 8
LANES = 128
CONV_ROWS = 32
C_HIST = C_CONV - 1
C_OFF = 32
B_OFF = 8
VMEM_LIMIT = 60 * 1024 * 1024


def _dot(a, b):
    return jnp.dot(a.astype(BF16), b.astype(BF16), preferred_element_type=F32)


def _dot_nt(a, b):
    return lax.dot_general(a.astype(BF16), b.astype(BF16), (((1,), (1,)), ((), ())),
                           preferred_element_type=F32)


def _dot_tn(a, b):
    return lax.dot_general(a.astype(BF16), b.astype(BF16), (((0,), (0,)), ((), ())),
                           preferred_element_type=F32)


def _sigmoid(x):
    return 1.0 / (1.0 + jnp.exp(-x))


def _silu(x):
    return x * _sigmoid(x)


def _layer_norm(x, g, b):
    mu = jnp.mean(x, axis=-1, keepdims=True)
    xc = x - mu
    var = jnp.mean(xc * xc, axis=-1, keepdims=True)
    return xc * lax.rsqrt(var + LN_EPS) * g + b


class _PostBlock:
    N_BLOCKS = FFN_HIDDEN // FFN_BLOCK
    N_STEPS = 2 + N_BLOCKS

    def __init__(self, x, mix, w_o_ref, ln_ref, wg_ref, wu_ref, wd_ref):
        self.x, self.mix = x, mix
        self.w_o_ref, self.ln_ref = w_o_ref, ln_ref
        self.wg_ref, self.wu_ref, self.wd_ref = wg_ref, wu_ref, wd_ref
        self.done = 0
        self.h = self.hb = self.acc = self.act = None

    def _gate_up(self, j):
        lo = j * FFN_BLOCK
        gate = jnp.dot(self.hb, self.wg_ref[:, lo:lo + FFN_BLOCK], preferred_element_type=F32)
        up = jnp.dot(self.hb, self.wu_ref[:, lo:lo + FFN_BLOCK], preferred_element_type=F32)
        return (_silu(gate) * up).astype(BF16)

    def _step(self, j):
        if j == 0:
            ln = self.ln_ref
            self.h = _layer_norm(ALPHA * self.x + _dot(self.mix, self.w_o_ref[...]), ln[0:1], ln[1:2])
            self.hb = self.h.astype(BF16)
            return
        prev = self.act
        self.act = self._gate_up(j - 1) if j - 1 < self.N_BLOCKS else None
        if prev is not None:
            lo = (j - 2) * FFN_BLOCK
            part = jnp.dot(prev, self.wd_ref[lo:lo + FFN_BLOCK, :], preferred_element_type=F32)
            self.acc = part if self.acc is None else self.acc + part

    def run(self, n):
        for _ in range(n):
            if self.done < self.N_STEPS:
                self._step(self.done)
                self.done += 1

    def result(self):
        self.run(self.N_STEPS)
        return _layer_norm(ALPHA * self.h + self.acc, self.ln_ref[2:3], self.ln_ref[3:4])


class _SplitPostBlock:
    def __init__(self, x, mix, weight_refs, groups):
        rows = x.shape[0] // groups
        self.parts = [_PostBlock(x[g * rows:(g + 1) * rows], mix[g * rows:(g + 1) * rows], *weight_refs)
                      for g in range(groups)]
        self.outs = []

    def run(self, n):
        for _ in range(n):
            if len(self.outs) == len(self.parts):
                return
            part = self.parts[len(self.outs)]
            part.run(1)
            if part.done == part.N_STEPS:
                self.outs.append(part.result())

    def result(self):
        self.run(sum(part.N_STEPS for part in self.parts))
        return jnp.concatenate(self.outs, axis=0)


def _no_op():
    pass


def _attn_chunks(chunks, alibi_ref, sink_ref, between=_no_op):
    lane = lax.broadcasted_iota(jnp.int32, (BAND, KV_W), 1)
    low = lane < HEAD_DIM
    high = jnp.logical_not(low)
    sink_rows = [jnp.where(low[0:1], sink_ref[4 * g + e], sink_ref[4 * g + 2 + e])
                 for g in range(A_KV_HEADS) for e in range(2)]

    def extend(band, swapped, g, e):
        src = band if g == e else swapped
        return jnp.where(low if e == 0 else high, src, 0.0).astype(BF16)

    scores = []
    for q, kband, _, kvbias in chunks:
        kswap = pltpu.roll(kband, HEAD_DIM, axis=1)
        for g in range(A_KV_HEADS):
            qg = jnp.concatenate([q[:, 256 * g:256 * g + 128], q[:, 256 * g + 128:256 * g + 256]],
                                 axis=0).astype(BF16)
            k2 = jnp.concatenate([extend(kband, kswap, g, e) for e in range(2)], axis=0)
            s = _dot_nt(k2, qg) + alibi_ref[g]
            for e in range(2):
                se = s[BAND * e:BAND * (e + 1)]
                scores.append(se if kvbias is None else se + kvbias)
        between()

    probs = []
    for idx, s in enumerate(scores):
        sink = sink_rows[idx % (2 * A_KV_HEADS)]
        m = jnp.maximum(jnp.max(s, axis=0, keepdims=True), sink)
        p = jnp.exp(s - m)
        den = jnp.sum(p, axis=0, keepdims=True) + jnp.exp(sink - m)
        probs.append((p * (1.0 / den)).astype(BF16))
        if idx % (2 * A_KV_HEADS) == 2 * A_KV_HEADS - 1:
            between()

    outs = []
    for ci, (_, _, vband, _) in enumerate(chunks):
        vswap = pltpu.roll(vband, HEAD_DIM, axis=1)
        parts = []
        for g in range(A_KV_HEADS):
            base = ci * 2 * A_KV_HEADS + 2 * g
            p2 = jnp.concatenate([probs[base], probs[base + 1]], axis=0)
            v2 = jnp.concatenate([extend(vband, vswap, g, e) for e in range(2)], axis=0)
            o = _dot_tn(p2, v2)
            parts += [o[0:CHUNK], o[CHUNK:2 * CHUNK]]
        outs.append(jnp.concatenate(parts, axis=1))
    return outs


def _ab_project(x, w_in_ref, b_in_ref):
    proj = _dot(x, w_in_ref[...]) + b_in_ref[...]
    q = proj[:, 0:Q_W] * (HEAD_DIM ** -0.5)
    k = proj[:, Q_W:Q_W + KV_W]
    v = proj[:, Q_W + KV_W:Q_W + 2 * KV_W]
    o = Q_W + 2 * KV_W
    bg = proj[:, o:o + B_WIDTH]
    u = proj[:, o + B_WIDTH:o + 2 * B_WIDTH] * proj[:, o + 2 * B_WIDTH:o + 3 * B_WIDTH]
    return q, k, v, bg, u


def _short_conv(u_ref, base, rows, w_ref):
    out = None
    for j in range(B_CONV):
        term = u_ref[pl.ds(base - (B_CONV - 1) + j, rows), :] * w_ref[j:j + 1, :]
        out = term if out is None else out + term
    return out


def _streamed_residual_block(first, x_ref, mix_s, w_o_ref, ln_ref, wg_ref, wu_ref, wd_ref,
                             h_ref, wg_out, wu_out, wd_out, h_s, hb_s, acc_s):
    j = pl.program_id(0)

    @pl.when(j == 0)
    def _():
        first()
        h = _layer_norm(ALPHA * x_ref[...] + _dot(mix_s[...], w_o_ref[...]), ln_ref[0:1], ln_ref[1:2])
        h_s[...] = h
        hb_s[...] = h.astype(BF16)
        acc_s[...] = jnp.zeros_like(acc_s)

    wg = wg_ref[0].astype(BF16)
    wu = wu_ref[0].astype(BF16)
    wd = wd_ref[0].astype(BF16)
    wg_out[...] = wg
    wu_out[...] = wu
    wd_out[...] = wd
    hb = hb_s[...]
    gate = jnp.dot(hb, wg, preferred_element_type=F32)
    up = jnp.dot(hb, wu, preferred_element_type=F32)
    acc_s[...] += jnp.dot((_silu(gate) * up).astype(BF16), wd, preferred_element_type=F32)

    @pl.when(j == pl.num_programs(0) - 1)
    def _():
        h_ref[...] = _layer_norm(ALPHA * h_s[...] + acc_s[...], ln_ref[2:3], ln_ref[3:4])


def _l0_short_kernel(x_ref, rowmask_ref, k0_ref, v0_ref, u0_ref, kvb_ref, alibi_ref, sink_ref,
                     convw_ref, w_in_ref, b_in_ref, w_o_ref, ln_ref, wg_ref, wu_ref, wd_ref,
                     h_ref, kout_ref, vout_ref, uout_ref, wg_out, wu_out, wd_out,
                     u_s, mix_s, h_s, hb_s, acc_s):
    mixers = functools.partial(_l0_short_mixers, x_ref, rowmask_ref, k0_ref, v0_ref, u0_ref, kvb_ref,
                               alibi_ref, sink_ref, convw_ref, w_in_ref, b_in_ref,
                               kout_ref, vout_ref, uout_ref, u_s, mix_s)
    _streamed_residual_block(mixers, x_ref, mix_s, w_o_ref, ln_ref, wg_ref, wu_ref, wd_ref,
                             h_ref, wg_out, wu_out, wd_out, h_s, hb_s, acc_s)


def _l0_short_mixers(x_ref, rowmask_ref, k0_ref, v0_ref, u0_ref, kvb_ref, alibi_ref, sink_ref,
                     convw_ref, w_in_ref, b_in_ref, kout_ref, vout_ref, uout_ref, u_s, mix_s):
    ns = k0_ref.shape[0] + 1
    x = x_ref[...]
    q, k, v, bg, u = _ab_project(x, w_in_ref, b_in_ref)
    u = u * rowmask_ref[...]

    chunks = []
    for s in range(ns):
        r0 = s * CHUNK
        meta = s == ns - 1
        empty = jnp.zeros((WINDOW, KV_W), F32)
        kband = jnp.concatenate([empty if meta else k0_ref[s], k[r0:r0 + CHUNK]], axis=0)
        vband = jnp.concatenate([empty if meta else v0_ref[s], v[r0:r0 + CHUNK]], axis=0)
        kout_ref[s] = kband[CHUNK:]
        vout_ref[s] = vband[CHUNK:]
        chunks.append((q[r0:r0 + CHUNK], kband, vband, kvb_ref[...] if s == ns - 1 else None))
    for s, att in enumerate(_attn_chunks(chunks, alibi_ref, sink_ref)):
        mix_s[s * CHUNK:(s + 1) * CHUNK, 0:Q_W] = att

    for s in range(ns):
        r0 = s * CHUNK
        us = u_s.at[s]
        us[B_OFF - 2:B_OFF, :] = jnp.zeros((B_CONV - 1, B_WIDTH), F32) if s == ns - 1 else u0_ref[s]
        us[B_OFF:B_OFF + CHUNK, :] = u[r0:r0 + CHUNK]
        cb = _short_conv(us, B_OFF, CHUNK, convw_ref)
        mix_s[r0:r0 + CHUNK, Q_W:] = bg[r0:r0 + CHUNK] * cb
        uout_ref[s] = u[r0 + CHUNK - 2:r0 + CHUNK]


def _tile_position(nt):
    i = pl.program_id(0)
    n = pl.num_programs(0) - 1
    return i, n, lax.rem(jnp.minimum(i, n - 1), nt)


def _first_or_fused_step(i, step):
    pl.when(i == 0)(lambda: step(False))
    pl.when(i > 0)(lambda: step(True))


def _l0_main_kernel(nt, x_ref, xprev_ref, k0_ref, v0_ref, u0_ref, kvb_ref, alibi_ref, sink_ref,
                    convw_ref, w_in_ref, b_in_ref, w_o_ref, ln_ref, wg_ref, wu_ref, wd_ref,
                    h_ref, kout_ref, vout_ref, uout_ref,
                    k_s, v_s, u_s, mix_s):
    i, n, t = _tile_position(nt)
    tm = x_ref.shape[1]

    @pl.when(t == 0)
    def _():
        k_s[0:WINDOW, :] = k0_ref[0]
        v_s[0:WINDOW, :] = v0_ref[0]
        u_s[B_OFF - 2:B_OFF, :] = u0_ref[0]

    def step(with_residual):
        post = None
        if with_residual:
            post = _SplitPostBlock(xprev_ref[0], mix_s[...], (w_o_ref, ln_ref, wg_ref, wu_ref, wd_ref),
                                   POST_ROW_GROUPS)
            post.run(1)

        q, k, v, bg, u = _ab_project(x_ref[0], w_in_ref, b_in_ref)
        k_s[WINDOW:, :] = k
        v_s[WINDOW:, :] = v
        u_s[B_OFF:, :] = u
        mix_s[:, Q_W:] = bg * _short_conv(u_s, B_OFF, tm, convw_ref)

        first = t == 0
        chunks = []
        for c in range(tm // CHUNK):
            r0 = c * CHUNK
            kvbias = jnp.where(first, kvb_ref[c], 0.0) if c < 2 else None
            chunks.append((q[r0:r0 + CHUNK], k_s[r0:r0 + BAND, :], v_s[r0:r0 + BAND, :], kvbias))
        plan = iter(L0_FFN_STEPS_BETWEEN)
        between = (lambda: post.run(next(plan, 0))) if with_residual else _no_op
        for c, att in enumerate(_attn_chunks(chunks, alibi_ref, sink_ref, between)):
            mix_s[c * CHUNK:(c + 1) * CHUNK, 0:Q_W] = att

        if with_residual:
            h_ref[0] = post.result()

        k_s[0:WINDOW, :] = k_s[tm:tm + WINDOW, :]
        v_s[0:WINDOW, :] = v_s[tm:tm + WINDOW, :]
        u_s[B_OFF - 2:B_OFF, :] = u_s[B_OFF + tm - 2:B_OFF + tm, :]

    _first_or_fused_step(i, step)

    @pl.when(jnp.logical_and(t == nt - 1, i < n))
    def _():
        kout_ref[0] = k_s[0:WINDOW, :]
        vout_ref[0] = v_s[0:WINDOW, :]
        uout_ref[0] = u_s[B_OFF - 2:B_OFF, :]


def _c_project(x, rowmask, w_in_ref, b_in_ref):
    w = C_WIDTH
    proj = _dot(x, w_in_ref[:, 0:2 * w]) + b_in_ref[:, 0:2 * w]
    u = proj[:, 0:w] * _sigmoid(proj[:, w:2 * w])
    return u if rowmask is None else u * rowmask


def _d_project(x, rowmask, w_in_ref, b_in_ref, lb_ref):
    w = D_WIDTH
    proj = _dot(x, w_in_ref[:, 2 * C_WIDTH:]) + b_in_ref[:, 2 * C_WIDTH:]
    q = proj[:, 0:w]
    d0 = lb_ref[0:1]
    d1 = lb_ref[1:2]
    mx = jnp.maximum(d0, d1)
    e0 = jnp.exp(d0 - mx)
    e1 = jnp.exp(d1 - mx)
    p0 = e0 / (e0 + e1)
    p1 = e1 / (e0 + e1)
    lb = (p0 + p1) - p0
    forget = lb + (1.0 - lb) * _sigmoid(proj[:, w:2 * w])
    kk = 1.0 - forget
    lf = jnp.log(forget)
    v = proj[:, 2 * w:3 * w]
    g = proj[:, 3 * w:4 * w]
    if rowmask is not None:
        q = q * rowmask
        kk = kk * rowmask
        v = v * rowmask
        lf = lf * rowmask
    return q, kk, v, lf, g


def _conformer_conv(c_ref, base, rows, convw_ref, cvec_ref, out_ref, out_base):
    for r0 in range(0, rows, CONV_ROWS):
        cols = []
        for c0 in range(0, C_WIDTH, LANES):
            lanes = slice(c0, c0 + LANES)
            win = c_ref[base - C_OFF + r0:base + r0 + CONV_ROWS, lanes]
            acc = None
            for phase in range(SUBLANES):
                shifted = win if phase == 0 else pltpu.roll(win, CONV_ROWS + C_OFF - phase, axis=0)
                for j in range(C_CONV):
                    off = C_OFF - C_HIST + j
                    if off % SUBLANES != phase:
                        continue
                    lo = off - phase
                    term = shifted[lo:lo + CONV_ROWS] * convw_ref[j:j + 1, lanes]
                    acc = term if acc is None else acc + term
            cols.append(acc)
        acc = jnp.concatenate(cols, axis=1)
        y = _layer_norm(acc + cvec_ref[0:1], cvec_ref[1:2], cvec_ref[2:3])
        out_ref[out_base + r0:out_base + r0 + CONV_ROWS, 0:C_WIDTH] = _silu(y)


def _hgrn_chunks(chunks, init_states, between=_no_op):
    row = lax.broadcasted_iota(jnp.int32, (CHUNK, CHUNK), 0)
    col = lax.broadcasted_iota(jnp.int32, (CHUNK, CHUNK), 1)
    tril = col <= row
    ltri = jnp.where(tril, 1.0, 0.0).astype(BF16)
    nsub = CHUNK // SUB
    rsub = jnp.right_shift(lax.broadcasted_iota(jnp.int32, (CHUNK, D_KEY), 0), SUB.bit_length() - 1)
    heads = [slice(D_KEY * h, D_KEY * (h + 1)) for h in range(D_HEADS)]

    cums = []
    for _, _, _, lf in chunks:
        hi = lf.astype(BF16)
        r1 = lf - hi.astype(F32)
        mid = r1.astype(BF16)
        lo = (r1 - mid.astype(F32)).astype(BF16)
        cums.append(jnp.dot(ltri, hi, preferred_element_type=F32)
                    + jnp.dot(ltri, mid, preferred_element_type=F32)
                    + jnp.dot(ltri, lo, preferred_element_type=F32))

    between()
    pre = []
    for (q, kk, v, _), cum in zip(chunks, cums):
        tot = cum[CHUNK - 1:CHUNK]
        bases = [jnp.zeros((1, D_WIDTH), F32)] + [cum[SUB * i - 1:SUB * i] for i in range(1, nsub)]
        bsel = jnp.concatenate([jnp.broadcast_to(b, (SUB, D_WIDTH)) for b in bases], axis=0)
        qe = (q * jnp.exp(cum)).astype(BF16)
        qd = q * jnp.exp(cum - bsel)
        kdec = (kk * jnp.exp(tot - cum)).astype(BF16)
        qblks, kdcs = [], []
        for sl in heads:
            qd_h = qd[:, sl]
            qblks.append(jnp.concatenate([jnp.where(rsub == i, qd_h, 0.0) for i in range(nsub)],
                                         axis=1).astype(BF16))
            kds = []
            for i in range(nsub):
                n = SUB * (i + 1)
                kd = kk[0:n, sl] * jnp.exp(bases[i][:, sl] - cum[0:n, sl])
                if n < CHUNK:
                    kd = jnp.concatenate([kd, jnp.zeros((CHUNK - n, D_KEY), F32)], axis=0)
                kds.append(kd)
            kdcs.append(jnp.concatenate(kds, axis=1).astype(BF16))
        pre.append((qe, kdec, v.astype(BF16), jnp.exp(tot), qblks, kdcs))

    atts, incs = [], []
    for qe, kdec, vb, etot, qblks, kdcs in pre:
        atts.append([jnp.where(tril, _dot_nt(qblks[h], kdcs[h]), 0.0).astype(BF16) for h in range(D_HEADS)])
        incs.append([_dot_tn(vb[:, sl], kdec[:, sl]) for sl in heads])

    between()
    intra = [[_dot(atts[c][h], pre[c][2][:, heads[h]]) for h in range(D_HEADS)] for c in range(len(chunks))]

    between()
    outs, ends = [], []
    state = None
    for c, (qe, kdec, vb, etot, _, _) in enumerate(pre):
        if init_states[c] is not None:
            state = init_states[c]
        o = [_dot_nt(qe[:, heads[h]], state[h]) + intra[c][h] for h in range(D_HEADS)]
        state = [etot[:, heads[h]] * state[h] + incs[c][h] for h in range(D_HEADS)]
        outs.append(jnp.concatenate(o, axis=1))
        ends.append(state)
    return outs, ends


def _hgrn_readout(o, g, norm_g):
    parts = []
    for h in range(D_HEADS):
        oh = o[:, D_VAL * h:D_VAL * (h + 1)]
        parts.append(oh * lax.rsqrt(jnp.mean(oh * oh, axis=-1, keepdims=True) + RMS_EPS))
    return jnp.concatenate(parts, axis=1) * norm_g * _silu(g)


def _l1_short_kernel(x_ref, rowmask_ref, c0_ref, s0_ref, convw_ref, cvec_ref, lb_ref, normg_ref,
                     w_in_ref, b_in_ref, w_o_ref, ln_ref, wg_ref, wu_ref, wd_ref,
                     h_ref, cout_ref, sout_ref, wg_out, wu_out, wd_out,
                     c_s, mix_s, h_s, hb_s, acc_s):
    mixers = functools.partial(_l1_short_mixers, x_ref, rowmask_ref, c0_ref, s0_ref, convw_ref, cvec_ref,
                               lb_ref, normg_ref, w_in_ref, b_in_ref, cout_ref, sout_ref, c_s, mix_s)
    _streamed_residual_block(mixers, x_ref, mix_s, w_o_ref, ln_ref, wg_ref, wu_ref, wd_ref,
                             h_ref, wg_out, wu_out, wd_out, h_s, hb_s, acc_s)


def _l1_short_mixers(x_ref, rowmask_ref, c0_ref, s0_ref, convw_ref, cvec_ref, lb_ref, normg_ref,
                     w_in_ref, b_in_ref, cout_ref, sout_ref, c_s, mix_s):
    ns = c0_ref.shape[0] + 1
    x = x_ref[...]
    rowmask = rowmask_ref[...]
    u = _c_project(x, rowmask, w_in_ref, b_in_ref)
    q, kk, v, lf, g = _d_project(x, rowmask, w_in_ref, b_in_ref, lb_ref)

    for s in range(ns):
        r0 = s * CHUNK
        cs = c_s.at[s]
        cs[C_OFF - C_HIST:C_OFF, :] = jnp.zeros((C_HIST, C_WIDTH), F32) if s == ns - 1 else c0_ref[s]
        cs[C_OFF:C_OFF + CHUNK, :] = u[r0:r0 + CHUNK]
        _conformer_conv(cs, C_OFF, CHUNK, convw_ref, cvec_ref, mix_s, r0)
        cout_ref[s] = cs[C_OFF + CHUNK - C_HIST:C_OFF + CHUNK, :]

    rows = [slice(s * CHUNK, (s + 1) * CHUNK) for s in range(ns)]
    outs, ends = _hgrn_chunks([(q[r], kk[r], v[r], lf[r]) for r in rows],
                              [[jnp.zeros((D_VAL, D_KEY), F32) if s == ns - 1 else s0_ref[s, h].T
                                for h in range(D_HEADS)] for s in range(ns)])
    for s in range(ns):
        mix_s[rows[s], C_WIDTH:] = _hgrn_readout(outs[s], g[rows[s]], normg_ref[...])
        for h in range(D_HEADS):
            sout_ref[s, h] = ends[s][h].T


def _l1_main_kernel(nt, x_ref, xprev_ref, c0_ref, s0_ref, convw_ref, cvec_ref, lb_ref, normg_ref,
                    w_in_ref, b_in_ref, w_o_ref, ln_ref, wg_ref, wu_ref, wd_ref,
                    h_ref, cout_ref, sout_ref,
                    c_s, st_s, mix_s):
    i, n, t = _tile_position(nt)
    tm = x_ref.shape[1]

    @pl.when(t == 0)
    def _():
        c_s[C_OFF - C_HIST:C_OFF, :] = c0_ref[0]
        for h in range(D_HEADS):
            st_s[h] = s0_ref[0, h].T

    def step(with_residual):
        post = None
        if with_residual:
            post = _SplitPostBlock(xprev_ref[0], mix_s[...], (w_o_ref, ln_ref, wg_ref, wu_ref, wd_ref),
                                   POST_ROW_GROUPS)
            post.run(1)

        x = x_ref[0]
        plan = iter(L1_FFN_STEPS)
        between = (lambda: post.run(next(plan, 0))) if with_residual else _no_op
        c_s[C_OFF:, :] = _c_project(x, None, w_in_ref, b_in_ref)
        q, kk, v, lf, g = _d_project(x, None, w_in_ref, b_in_ref, lb_ref)
        between()
        _conformer_conv(c_s, C_OFF, tm, convw_ref, cvec_ref, mix_s, 0)
        c_s[C_OFF - C_HIST:C_OFF, :] = c_s[C_OFF + tm - C_HIST:C_OFF + tm, :]

        nc = tm // CHUNK
        rows = [slice(c * CHUNK, (c + 1) * CHUNK) for c in range(nc)]
        outs, ends = _hgrn_chunks([(q[r], kk[r], v[r], lf[r]) for r in rows],
                                  [[st_s[h] for h in range(D_HEADS)]] + [None] * (nc - 1), between)
        for c in range(nc):
            mix_s[rows[c], C_WIDTH:] = _hgrn_readout(outs[c], g[rows[c]], normg_ref[...])
        for h in range(D_HEADS):
            st_s[h] = ends[-1][h]

        if with_residual:
            h_ref[0] = post.result()

    _first_or_fused_step(i, step)

    @pl.when(jnp.logical_and(t == nt - 1, i < n))
    def _():
        cout_ref[0] = c_s[C_OFF - C_HIST:C_OFF, :]
        for h in range(D_HEADS):
            sout_ref[0, h] = st_s[h].T


def _whole(shape):
    zeros = (0,) * len(shape)
    return pl.BlockSpec(shape, lambda *_: zeros, pipeline_mode=pl.Buffered(1))


def _const_spec(a):
    return pl.BlockSpec(memory_space=pltpu.SMEM) if a.ndim == 1 else _whole(a.shape)


N_FFN_BLOCKS = FFN_HIDDEN // FFN_BLOCK


def _streamed_ffn_specs(layer):
    in_specs = [pl.BlockSpec((1, D_MODEL, FFN_BLOCK), lambda j: (layer, 0, j)),
                pl.BlockSpec((1, D_MODEL, FFN_BLOCK), lambda j: (layer, 0, N_FFN_BLOCKS + j)),
                pl.BlockSpec((1, FFN_BLOCK, D_MODEL), lambda j: (layer, j, 0))]
    out_specs = [pl.BlockSpec((D_MODEL, FFN_BLOCK), lambda j: (0, j)),
                 pl.BlockSpec((D_MODEL, FFN_BLOCK), lambda j: (0, j)),
                 pl.BlockSpec((FFN_BLOCK, D_MODEL), lambda j: (j, 0))]
    out_shape = [jax.ShapeDtypeStruct((D_MODEL, FFN_HIDDEN), BF16),
                 jax.ShapeDtypeStruct((D_MODEL, FFN_HIDDEN), BF16),
                 jax.ShapeDtypeStruct((FFN_HIDDEN, D_MODEL), BF16)]
    return in_specs, out_specs, out_shape


def _residual_scratch(rows):
    return [pltpu.VMEM((rows, D_MODEL), F32), pltpu.VMEM((rows, D_MODEL), BF16), pltpu.VMEM((rows, D_MODEL), F32)]


def _whole_out(shape):
    zeros = (0,) * len(shape)
    return pl.BlockSpec(shape, lambda *_: zeros)


def _params(n_axes):
    return pltpu.CompilerParams(dimension_semantics=("arbitrary",) * n_axes,
                                vmem_limit_bytes=VMEM_LIMIT)


def _l0_short(x, rowmask, k0, v0, u0, kvb, consts, weights, w_gu, w_down, layer):
    rows, ns = x.shape[0], k0.shape[0] + 1
    ins = (x, rowmask, k0, v0, u0, kvb) + consts + weights
    ffn_in, ffn_out, ffn_shape = _streamed_ffn_specs(layer)
    return pl.pallas_call(
        _l0_short_kernel,
        grid=(N_FFN_BLOCKS,),
        in_specs=[_const_spec(a) for a in ins] + ffn_in,
        out_specs=[_whole_out((rows, D_MODEL)), _whole_out((ns, WINDOW, KV_W)),
                   _whole_out((ns, WINDOW, KV_W)), _whole_out((ns, B_CONV - 1, B_WIDTH))] + ffn_out,
        out_shape=[jax.ShapeDtypeStruct((rows, D_MODEL), F32),
                   jax.ShapeDtypeStruct((ns, WINDOW, KV_W), F32),
                   jax.ShapeDtypeStruct((ns, WINDOW, KV_W), F32),
                   jax.ShapeDtypeStruct((ns, B_CONV - 1, B_WIDTH), F32)] + ffn_shape,
        scratch_shapes=[pltpu.VMEM((ns, B_OFF + CHUNK, B_WIDTH), F32), pltpu.VMEM((rows, D_MODEL), F32)]
        + _residual_scratch(rows),
        compiler_params=_params(1),
        name="l0_short",
    )(*ins, w_gu, w_gu, w_down)


class _SkewedTiles:
    def __init__(self, nb, nt):
        self.nt, self.n = nt, nb * nt

    def _cur(self, i):
        return jnp.minimum(i, self.n - 1)

    def tile(self):
        return pl.BlockSpec((1, TILE, D_MODEL), lambda i: (self._cur(i) // self.nt, self._cur(i) % self.nt, 0))

    def prev_tile(self):
        prev = lambda i: jnp.maximum(i - 1, 0)
        return pl.BlockSpec((1, TILE, D_MODEL), lambda i: (prev(i) // self.nt, prev(i) % self.nt, 0))

    def fixed(self, shape, idx):
        zeros = (0,) * len(shape)
        return pl.BlockSpec((1,) + shape, lambda i: (idx,) + zeros)

    def per_prompt(self, shape):
        zeros = (0,) * len(shape)
        return pl.BlockSpec((1,) + shape, lambda i: (self._cur(i) // self.nt,) + zeros)


def _l0_main(x, k0, v0, u0, kvb, consts, weights, meta_idx):
    nb, seq, _ = x.shape
    nt = seq // TILE
    sk = _SkewedTiles(nb, nt)
    ins = (x, x, k0, v0, u0, kvb) + consts + weights
    in_specs = [sk.tile(), sk.prev_tile(),
                sk.fixed((WINDOW, KV_W), meta_idx), sk.fixed((WINDOW, KV_W), meta_idx),
                sk.fixed((B_CONV - 1, B_WIDTH), meta_idx),
                _whole(kvb.shape)] + [_const_spec(a) for a in consts + weights]
    return pl.pallas_call(
        functools.partial(_l0_main_kernel, nt),
        grid=(nb * nt + 1,),
        in_specs=in_specs,
        out_specs=[sk.prev_tile(), sk.per_prompt((WINDOW, KV_W)), sk.per_prompt((WINDOW, KV_W)),
                   sk.per_prompt((B_CONV - 1, B_WIDTH))],
        out_shape=[jax.ShapeDtypeStruct((nb, seq, D_MODEL), F32),
                   jax.ShapeDtypeStruct((nb, WINDOW, KV_W), F32),
                   jax.ShapeDtypeStruct((nb, WINDOW, KV_W), F32),
                   jax.ShapeDtypeStruct((nb, B_CONV - 1, B_WIDTH), F32)],
        scratch_shapes=[pltpu.VMEM((WINDOW + TILE, KV_W), F32),
                        pltpu.VMEM((WINDOW + TILE, KV_W), F32), pltpu.VMEM((B_OFF + TILE, B_WIDTH), F32),
                        pltpu.VMEM((TILE, D_MODEL), F32)],
        compiler_params=_params(1),
        name="l0_main",
    )(*ins)


def _l1_short(x, rowmask, c0, s0, consts, weights, w_gu, w_down, layer):
    rows, ns = x.shape[0], c0.shape[0] + 1
    ins = (x, rowmask, c0, s0) + consts + weights
    ffn_in, ffn_out, ffn_shape = _streamed_ffn_specs(layer)
    return pl.pallas_call(
        _l1_short_kernel,
        grid=(N_FFN_BLOCKS,),
        in_specs=[_const_spec(a) for a in ins] + ffn_in,
        out_specs=[_whole_out((rows, D_MODEL)), _whole_out((ns, C_HIST, C_WIDTH)),
                   _whole_out((ns, D_HEADS, D_KEY, D_VAL))] + ffn_out,
        out_shape=[jax.ShapeDtypeStruct((rows, D_MODEL), F32),
                   jax.ShapeDtypeStruct((ns, C_HIST, C_WIDTH), F32),
                   jax.ShapeDtypeStruct((ns, D_HEADS, D_KEY, D_VAL), F32)] + ffn_shape,
        scratch_shapes=[pltpu.VMEM((ns, C_OFF + CHUNK, C_WIDTH), F32), pltpu.VMEM((rows, D_MODEL), F32)]
        + _residual_scratch(rows),
        compiler_params=_params(1),
        name="l1_short",
    )(*ins, w_gu, w_gu, w_down)


def _l1_main(x, c0, s0, consts, weights, meta_idx):
    nb, seq, _ = x.shape
    nt = seq // TILE
    sk = _SkewedTiles(nb, nt)
    ins = (x, x, c0, s0) + consts + weights
    in_specs = [sk.tile(), sk.prev_tile(), sk.fixed((C_HIST, C_WIDTH), meta_idx),
                sk.fixed((D_HEADS, D_KEY, D_VAL), meta_idx),
                ] + [_const_spec(a) for a in consts + weights]
    return pl.pallas_call(
        functools.partial(_l1_main_kernel, nt),
        grid=(nb * nt + 1,),
        in_specs=in_specs,
        out_specs=[sk.prev_tile(), sk.per_prompt((C_HIST, C_WIDTH)),
                   sk.per_prompt((D_HEADS, D_KEY, D_VAL))],
        out_shape=[jax.ShapeDtypeStruct((nb, seq, D_MODEL), F32),
                   jax.ShapeDtypeStruct((nb, C_HIST, C_WIDTH), F32),
                   jax.ShapeDtypeStruct((nb, D_HEADS, D_KEY, D_VAL), F32)],
        scratch_shapes=[pltpu.VMEM((C_OFF + TILE, C_WIDTH), F32), pltpu.VMEM((D_HEADS, D_VAL, D_KEY), F32),
                        pltpu.VMEM((TILE, D_MODEL), F32)],
        compiler_params=_params(1),
        name="l1_main",
    )(*ins)


def _alibi_tables():
    slopes = np.exp2(-8.0 * np.arange(1, A_HEADS + 1, dtype=np.float32) / A_HEADS).astype(np.float32)
    i = np.arange(CHUNK, dtype=np.float32)[None, :]
    j = np.arange(BAND, dtype=np.float32)[:, None]
    dist = np.abs(WINDOW + i - j).astype(np.float32)
    bias = np.zeros((A_KV_HEADS, 2 * BAND, 2 * CHUNK), np.float32)
    for g in range(A_KV_HEADS):
        for e in range(2):
            for half, head in enumerate((4 * g + e, 4 * g + 2 + e)):
                cols = slice(CHUNK * half, CHUNK * (half + 1))
                bias[g, BAND * e:BAND * (e + 1), cols] = -slopes[head] * dist
    return bias


def kernel(x_prompt, x_sample, cache_k_a, cache_v_a, state_conv_b, state_conv_c, state_hgrn, meta_tokens,
           ab_w_in, ab_b_in, a_sinks, b_conv_w, ab_w_o, cd_w_in, cd_b_in, c_conv_w, c_conv_b, c_ln_g,
           c_ln_b, d_lower_bounds, d_norm_g, cd_w_o, ln1_g, ln1_b, ln2_g, ln2_b, ffn_w_gu, ffn_w_down):
    nb = x_prompt.shape[0]
    ns = x_sample.shape[0]
    assert x_sample.shape[1] == CHUNK and cache_k_a.shape[1] == WINDOW
    assert x_prompt.shape[1] % TILE == 0
    meta_idx = ns
    n_short = ns + 1
    rows = n_short * CHUNK

    xs = jnp.concatenate([x_sample.reshape(ns * CHUNK, D_MODEL),
                          jnp.zeros((PADF, D_MODEL), F32), meta_tokens.astype(F32)], axis=0)
    rowmask = np.ones((rows, 1), np.float32)
    rowmask[ns * CHUNK:ns * CHUNK + PADF] = 0.0
    rowmask = jnp.asarray(rowmask)

    kvb_short = np.zeros((BAND, 2 * CHUNK), np.float32)
    kvb_short[:WINDOW + PADF] = NEG
    kvb_main = np.zeros((2, BAND, 2 * CHUNK), np.float32)
    kvb_main[0, :CHUNK + PADF] = NEG
    kvb_main[1, :PADF] = NEG

    a_consts = (jnp.asarray(_alibi_tables()), a_sinks.astype(F32), b_conv_w.astype(F32))

    k0 = cache_k_a.reshape(ns, WINDOW, KV_W)
    v0 = cache_v_a.reshape(ns, WINDOW, KV_W)
    u0, c0, s0 = state_conv_b, state_conv_c, state_hgrn

    def layer_weights(l, w_in, b_in, w_o):
        ln = jnp.stack([ln1_g[l], ln1_b[l], ln2_g[l], ln2_b[l]]).astype(F32)
        return (w_in.astype(BF16), b_in.astype(F32)[None, :], w_o.astype(BF16), ln)

    w0 = layer_weights(0, ab_w_in, ab_b_in, ab_w_o)
    w1 = layer_weights(1, cd_w_in, cd_b_in, cd_w_o)
    w_gu, w_down = ffn_w_gu.astype(F32), ffn_w_down.astype(F32)

    hs0, k_s, v_s, u_s, *ffn0 = _l0_short(xs, rowmask, k0, v0, u0, jnp.asarray(kvb_short), a_consts, w0,
                                          w_gu, w_down, 0)
    hp0, k_p, v_p, u_p = _l0_main(x_prompt, k_s, v_s, u_s, jnp.asarray(kvb_main), a_consts,
                                  w0 + tuple(ffn0), meta_idx)

    cvec = jnp.stack([c_conv_b, c_ln_g, c_ln_b]).astype(F32)
    c_consts = (c_conv_w.astype(F32), cvec, d_lower_bounds.astype(F32), d_norm_g.astype(F32)[None, :])
    hs1, c_s, s_s, *ffn1 = _l1_short(hs0, rowmask, c0, s0, c_consts, w1, w_gu, w_down, 1)
    hp1, c_p, s_p = _l1_main(hp0, c_s, s_s, c_consts, w1 + tuple(ffn1), meta_idx)

    kv = lambda a: a.reshape(a.shape[0], WINDOW, A_KV_HEADS, HEAD_DIM)
    y_sample = hs1[:ns * CHUNK].reshape(ns, CHUNK, D_MODEL)
    return (hp1, y_sample, kv(k_p), kv(v_p), u_p, c_p, s_p,
            kv(k_s[:ns]), kv(v_s[:ns]), u_s[:ns], c_s[:ns], s_s[:ns])
```

```python
import functools

import jax
import jax.numpy as jnp
import numpy as np
from jax import lax
from jax.experimental import pallas as pl
from jax.experimental.pallas import tpu as pltpu

F32 = jnp.float32
BF16 = jnp.bfloat16

D_MODEL = 1024
CHUNK = 64
N_META = 16
PADF = CHUNK - N_META
A_HEADS = 8
A_KV_HEADS = 2
HEAD_DIM = 64
WINDOW = 128
KV_W = A_KV_HEADS * HEAD_DIM
Q_W = A_HEADS * HEAD_DIM
BAND = WINDOW + CHUNK
B_WIDTH = 512
B_CONV = 3
C_WIDTH = 512
C_CONV = 31
D_HEADS = 4
D_KEY = 128
D_VAL = 128
D_WIDTH = D_HEADS * D_VAL
SUB = 16
FFN_HIDDEN = 2816
DEPTH = 2
ALPHA = (2 * DEPTH) ** 0.25
LN_EPS = 1e-5
RMS_EPS = 1e-6
NEG = -1e30

AB_IN = Q_W + 2 * KV_W + 3 * B_WIDTH
CD_IN = 2 * C_WIDTH + 4 * D_WIDTH

TILE = 512
FFN_BLOCK = 256
POST_ROW_GROUPS = 2
L0_FFN_STEPS_BETWEEN = (POST_ROW_GROUPS,) * (2 * TILE // CHUNK)
L1_FFN_STEPS = tuple(POST_ROW_GROUPS * n for n in (8, 2, 1, 1))
SUBLANES = 8
LANES = 128
CONV_ROWS = 32
C_HIST = C_CONV - 1
C_OFF = 32
B_OFF = 8
VMEM_LIMIT = 60 * 1024 * 1024


def _dot(a, b):
    return jnp.dot(a.astype(BF16), b.astype(BF16), preferred_element_type=F32)


def _dot_nt(a, b):
    return lax.dot_general(a.astype(BF16), b.astype(BF16), (((1,), (1,)), ((), ())),
                           preferred_element_type=F32)


def _dot_tn(a, b):
    return lax.dot_general(a.astype(BF16), b.astype(BF16), (((0,), (0,)), ((), ())),
                           preferred_element_type=F32)


def _sigmoid(x):
    return 0.5 * jnp.tanh(0.5 * x) + 0.5


def _silu(x):
    return x * _sigmoid(x)


def _layer_norm(x, g, b):
    mu = jnp.mean(x, axis=-1, keepdims=True)
    xc = x - mu
    var = jnp.mean(xc * xc, axis=-1, keepdims=True)
    return xc * lax.rsqrt(var + LN_EPS) * g + b


class _PostBlock:
    N_BLOCKS = FFN_HIDDEN // FFN_BLOCK
    N_STEPS = 2 + N_BLOCKS

    def __init__(self, x, mix, w_o_ref, ln_ref, wg_ref, wu_ref, wd_ref):
        self.x, self.mix = x, mix
        self.w_o_ref, self.ln_ref = w_o_ref, ln_ref
        self.wg_ref, self.wu_ref, self.wd_ref = wg_ref, wu_ref, wd_ref
        self.done = 0
        self.h = self.hb = self.acc = self.act = None

    def _gate_up(self, j):
        lo = j * FFN_BLOCK
        gate = jnp.dot(self.hb, self.wg_ref[:, lo:lo + FFN_BLOCK], preferred_element_type=F32)
        up = jnp.dot(self.hb, self.wu_ref[:, lo:lo + FFN_BLOCK], preferred_element_type=F32)
        return (_silu(gate) * up).astype(BF16)

    def _step(self, j):
        if j == 0:
            ln = self.ln_ref
            self.h = _layer_norm(ALPHA * self.x + _dot(self.mix, self.w_o_ref[...]), ln[0:1], ln[1:2])
            self.hb = self.h.astype(BF16)
            return
        prev = self.act
        self.act = self._gate_up(j - 1) if j - 1 < self.N_BLOCKS else None
        if prev is not None:
            lo = (j - 2) * FFN_BLOCK
            part = jnp.dot(prev, self.wd_ref[lo:lo + FFN_BLOCK, :], preferred_element_type=F32)
            self.acc = part if self.acc is None else self.acc + part

    def run(self, n):
        for _ in range(n):
            if self.done < self.N_STEPS:
                self._step(self.done)
                self.done += 1

    def result(self):
        self.run(self.N_STEPS)
        return _layer_norm(ALPHA * self.h + self.acc, self.ln_ref[2:3], self.ln_ref[3:4])


class _SplitPostBlock:
    def __init__(self, x, mix, weight_refs, groups):
        rows = x.shape[0] // groups
        self.parts = [_PostBlock(x[g * rows:(g + 1) * rows], mix[g * rows:(g + 1) * rows], *weight_refs)
                      for g in range(groups)]
        self.outs = []

    def run(self, n):
        for _ in range(n):
            if len(self.outs) == len(self.parts):
                return
            part = self.parts[len(self.outs)]
            part.run(1)
            if part.done == part.N_STEPS:
                self.outs.append(part.result())

    def result(self):
        self.run(sum(part.N_STEPS for part in self.parts))
        return jnp.concatenate(self.outs, axis=0)


def _no_op():
    pass


def _attn_chunks(chunks, alibi_ref, sink_ref, between=_no_op):
    lane = lax.broadcasted_iota(jnp.int32, (BAND, KV_W), 1)
    low = lane < HEAD_DIM
    high = jnp.logical_not(low)
    sink_rows = [jnp.where(low[0:1], sink_ref[4 * g + e], sink_ref[4 * g + 2 + e])
                 for g in range(A_KV_HEADS) for e in range(2)]

    def extend(band, swapped, g, e):
        src = band if g == e else swapped
        return jnp.where(low if e == 0 else high, src, 0.0).astype(BF16)

    scores = []
    for q, kband, _, kvbias in chunks:
        kswap = pltpu.roll(kband, HEAD_DIM, axis=1)
        for g in range(A_KV_HEADS):
            qg = jnp.concatenate([q[:, 256 * g:256 * g + 128], q[:, 256 * g + 128:256 * g + 256]],
                                 axis=0).astype(BF16)
            k2 = jnp.concatenate([extend(kband, kswap, g, e) for e in range(2)], axis=0)
            s = _dot_nt(k2, qg) + alibi_ref[g]
            for e in range(2):
                se = s[BAND * e:BAND * (e + 1)]
                scores.append(se if kvbias is None else se + kvbias)
        between()

    probs = []
    for idx, s in enumerate(scores):
        sink = sink_rows[idx % (2 * A_KV_HEADS)]
        m = jnp.maximum(jnp.max(s, axis=0, keepdims=True), sink)
        p = jnp.exp(s - m)
        den = jnp.sum(p, axis=0, keepdims=True) + jnp.exp(sink - m)
        probs.append((p * (1.0 / den)).astype(BF16))
        if idx % (2 * A_KV_HEADS) == 2 * A_KV_HEADS - 1:
            between()

    outs = []
    for ci, (_, _, vband, _) in enumerate(chunks):
        vswap = pltpu.roll(vband, HEAD_DIM, axis=1)
        parts = []
        for g in range(A_KV_HEADS):
            base = ci * 2 * A_KV_HEADS + 2 * g
            p2 = jnp.concatenate([probs[base], probs[base + 1]], axis=0)
            v2 = jnp.concatenate([extend(vband, vswap, g, e) for e in range(2)], axis=0)
            o = _dot_tn(p2, v2)
            parts += [o[0:CHUNK], o[CHUNK:2 * CHUNK]]
        outs.append(jnp.concatenate(parts, axis=1))
    return outs


def _ab_project(x, w_in_ref, b_in_ref):
    proj = _dot(x, w_in_ref[...]) + b_in_ref[...]
    q = proj[:, 0:Q_W] * (HEAD_DIM ** -0.5)
    k = proj[:, Q_W:Q_W + KV_W]
    v = proj[:, Q_W + KV_W:Q_W + 2 * KV_W]
    o = Q_W + 2 * KV_W
    bg = proj[:, o:o + B_WIDTH]
    u = proj[:, o + B_WIDTH:o + 2 * B_WIDTH] * proj[:, o + 2 * B_WIDTH:o + 3 * B_WIDTH]
    return q, k, v, bg, u


def _short_conv(u_ref, base, rows, w_ref):
    out = None
    for j in range(B_CONV):
        term = u_ref[pl.ds(base - (B_CONV - 1) + j, rows), :] * w_ref[j:j + 1, :]
        out = term if out is None else out + term
    return out


def _streamed_residual_block(first, x_ref, mix_s, w_o_ref, ln_ref, wg_ref, wu_ref, wd_ref,
                             h_ref, wg_out, wu_out, wd_out, h_s, hb_s, acc_s):
    j = pl.program_id(0)

    @pl.when(j == 0)
    def _():
        first()
        h = _layer_norm(ALPHA * x_ref[...] + _dot(mix_s[...], w_o_ref[...]), ln_ref[0:1], ln_ref[1:2])
        h_s[...] = h
        hb_s[...] = h.astype(BF16)
        acc_s[...] = jnp.zeros_like(acc_s)

    wg = wg_ref[0].astype(BF16)
    wu = wu_ref[0].astype(BF16)
    wd = wd_ref[0].astype(BF16)
    wg_out[...] = wg
    wu_out[...] = wu
    wd_out[...] = wd
    hb = hb_s[...]
    gate = jnp.dot(hb, wg, preferred_element_type=F32)
    up = jnp.dot(hb, wu, preferred_element_type=F32)
    acc_s[...] += jnp.dot((_silu(gate) * up).astype(BF16), wd, preferred_element_type=F32)

    @pl.when(j == pl.num_programs(0) - 1)
    def _():
        h_ref[...] = _layer_norm(ALPHA * h_s[...] + acc_s[...], ln_ref[2:3], ln_ref[3:4])


def _l0_short_kernel(x_ref, rowmask_ref, k0_ref, v0_ref, u0_ref, kvb_ref, alibi_ref, sink_ref,
                     convw_ref, w_in_ref, b_in_ref, w_o_ref, ln_ref, wg_ref, wu_ref, wd_ref,
                     h_ref, kout_ref, vout_ref, uout_ref, wg_out, wu_out, wd_out,
                     u_s, mix_s, h_s, hb_s, acc_s):
    mixers = functools.partial(_l0_short_mixers, x_ref, rowmask_ref, k0_ref, v0_ref, u0_ref, kvb_ref,
                               alibi_ref, sink_ref, convw_ref, w_in_ref, b_in_ref,
                               kout_ref, vout_ref, uout_ref, u_s, mix_s)
    _streamed_residual_block(mixers, x_ref, mix_s, w_o_ref, ln_ref, wg_ref, wu_ref, wd_ref,
                             h_ref, wg_out, wu_out, wd_out, h_s, hb_s, acc_s)


def _l0_short_mixers(x_ref, rowmask_ref, k0_ref, v0_ref, u0_ref, kvb_ref, alibi_ref, sink_ref,
                     convw_ref, w_in_ref, b_in_ref, kout_ref, vout_ref, uout_ref, u_s, mix_s):
    ns = k0_ref.shape[0] + 1
    x = x_ref[...]
    q, k, v, bg, u = _ab_project(x, w_in_ref, b_in_ref)
    u = u * rowmask_ref[...]

    chunks = []
    for s in range(ns):
        r0 = s * CHUNK
        meta = s == ns - 1
        empty = jnp.zeros((WINDOW, KV_W), F32)
        kband = jnp.concatenate([empty if meta else k0_ref[s], k[r0:r0 + CHUNK]], axis=0)
        vband = jnp.concatenate([empty if meta else v0_ref[s], v[r0:r0 + CHUNK]], axis=0)
        kout_ref[s] = kband[CHUNK:]
        vout_ref[s] = vband[CHUNK:]
        chunks.append((q[r0:r0 + CHUNK], kband, vband, kvb_ref[...] if s == ns - 1 else None))
    for s, att in enumerate(_attn_chunks(chunks, alibi_ref, sink_ref)):
        mix_s[s * CHUNK:(s + 1) * CHUNK, 0:Q_W] = att

    for s in range(ns):
        r0 = s * CHUNK
        us = u_s.at[s]
        us[B_OFF - 2:B_OFF, :] = jnp.zeros((B_CONV - 1, B_WIDTH), F32) if s == ns - 1 else u0_ref[s]
        us[B_OFF:B_OFF + CHUNK, :] = u[r0:r0 + CHUNK]
        cb = _short_conv(us, B_OFF, CHUNK, convw_ref)
        mix_s[r0:r0 + CHUNK, Q_W:] = bg[r0:r0 + CHUNK] * cb
        uout_ref[s] = u[r0 + CHUNK - 2:r0 + CHUNK]


def _tile_position(nt):
    i = pl.program_id(0)
    n = pl.num_programs(0) - 1
    return i, n, lax.rem(jnp.minimum(i, n - 1), nt)


def _zero_on_first_step(i, mix_s):
    @pl.when(i == 0)
    def _():
        mix_s[...] = jnp.zeros_like(mix_s)


def _l0_main_kernel(nt, x_ref, xprev_ref, k0_ref, v0_ref, u0_ref, kvb_ref, alibi_ref, sink_ref,
                    convw_ref, w_in_ref, b_in_ref, w_o_ref, ln_ref, wg_ref, wu_ref, wd_ref,
                    h_ref, kout_ref, vout_ref, uout_ref,
                    k_s, v_s, u_s, mix_s):
    i, n, t = _tile_position(nt)
    tm = x_ref.shape[1]

    @pl.when(t == 0)
    def _():
        k_s[0:WINDOW, :] = k0_ref[0]
        v_s[0:WINDOW, :] = v0_ref[0]
        u_s[B_OFF - 2:B_OFF, :] = u0_ref[0]

    _zero_on_first_step(i, mix_s)

    post = _SplitPostBlock(xprev_ref[0], mix_s[...], (w_o_ref, ln_ref, wg_ref, wu_ref, wd_ref), POST_ROW_GROUPS)
    post.run(1)

    q, k, v, bg, u = _ab_project(x_ref[0], w_in_ref, b_in_ref)
    k_s[WINDOW:, :] = k
    v_s[WINDOW:, :] = v
    u_s[B_OFF:, :] = u
    mix_s[:, Q_W:] = bg * _short_conv(u_s, B_OFF, tm, convw_ref)

    first = t == 0
    chunks = []
    for c in range(tm // CHUNK):
        r0 = c * CHUNK
        kvbias = jnp.where(first, kvb_ref[c], 0.0) if c < 2 else None
        chunks.append((q[r0:r0 + CHUNK], k_s[r0:r0 + BAND, :], v_s[r0:r0 + BAND, :], kvbias))
    plan = iter(L0_FFN_STEPS_BETWEEN)
    outs = _attn_chunks(chunks, alibi_ref, sink_ref, between=lambda: post.run(next(plan, 0)))
    for c, att in enumerate(outs):
        mix_s[c * CHUNK:(c + 1) * CHUNK, 0:Q_W] = att

    h_ref[0] = post.result()

    k_s[0:WINDOW, :] = k_s[tm:tm + WINDOW, :]
    v_s[0:WINDOW, :] = v_s[tm:tm + WINDOW, :]
    u_s[B_OFF - 2:B_OFF, :] = u_s[B_OFF + tm - 2:B_OFF + tm, :]

    @pl.when(jnp.logical_and(t == nt - 1, i < n))
    def _():
        kout_ref[0] = k_s[0:WINDOW, :]
        vout_ref[0] = v_s[0:WINDOW, :]
        uout_ref[0] = u_s[B_OFF - 2:B_OFF, :]


def _c_project(x, rowmask, w_in_ref, b_in_ref):
    w = C_WIDTH
    proj = _dot(x, w_in_ref[:, 0:2 * w]) + b_in_ref[:, 0:2 * w]
    u = proj[:, 0:w] * _sigmoid(proj[:, w:2 * w])
    return u if rowmask is None else u * rowmask


def _d_project(x, rowmask, w_in_ref, b_in_ref, lb_ref):
    w = D_WIDTH
    proj = _dot(x, w_in_ref[:, 2 * C_WIDTH:]) + b_in_ref[:, 2 * C_WIDTH:]
    q = proj[:, 0:w]
    d0 = lb_ref[0:1]
    d1 = lb_ref[1:2]
    mx = jnp.maximum(d0, d1)
    e0 = jnp.exp(d0 - mx)
    e1 = jnp.exp(d1 - mx)
    p0 = e0 / (e0 + e1)
    p1 = e1 / (e0 + e1)
    lb = (p0 + p1) - p0
    forget = lb + (1.0 - lb) * _sigmoid(proj[:, w:2 * w])
    kk = 1.0 - forget
    lf = jnp.log(forget)
    v = proj[:, 2 * w:3 * w]
    g = proj[:, 3 * w:4 * w]
    if rowmask is not None:
        q = q * rowmask
        kk = kk * rowmask
        v = v * rowmask
        lf = lf * rowmask
    return q, kk, v, lf, g


def _conformer_conv(c_ref, base, rows, convw_ref, cvec_ref, out_ref, out_base):
    for r0 in range(0, rows, CONV_ROWS):
        cols = []
        for c0 in range(0, C_WIDTH, LANES):
            lanes = slice(c0, c0 + LANES)
            win = c_ref[base - C_OFF + r0:base + r0 + CONV_ROWS, lanes]
            acc = None
            for phase in range(SUBLANES):
                shifted = win if phase == 0 else pltpu.roll(win, CONV_ROWS + C_OFF - phase, axis=0)
                for j in range(C_CONV):
                    off = C_OFF - C_HIST + j
                    if off % SUBLANES != phase:
                        continue
                    lo = off - phase
                    term = shifted[lo:lo + CONV_ROWS] * convw_ref[j:j + 1, lanes]
                    acc = term if acc is None else acc + term
            cols.append(acc)
        acc = jnp.concatenate(cols, axis=1)
        y = _layer_norm(acc + cvec_ref[0:1], cvec_ref[1:2], cvec_ref[2:3])
        out_ref[out_base + r0:out_base + r0 + CONV_ROWS, 0:C_WIDTH] = _silu(y)


def _hgrn_chunks(chunks, init_states, between=_no_op):
    row = lax.broadcasted_iota(jnp.int32, (CHUNK, CHUNK), 0)
    col = lax.broadcasted_iota(jnp.int32, (CHUNK, CHUNK), 1)
    tril = col <= row
    ltri = jnp.where(tril, 1.0, 0.0).astype(BF16)
    nsub = CHUNK // SUB
    rsub = jnp.right_shift(lax.broadcasted_iota(jnp.int32, (CHUNK, D_KEY), 0), SUB.bit_length() - 1)
    heads = [slice(D_KEY * h, D_KEY * (h + 1)) for h in range(D_HEADS)]

    cums = []
    for _, _, _, lf in chunks:
        hi = lf.astype(BF16)
        r1 = lf - hi.astype(F32)
        mid = r1.astype(BF16)
        lo = (r1 - mid.astype(F32)).astype(BF16)
        cums.append(jnp.dot(ltri, hi, preferred_element_type=F32)
                    + jnp.dot(ltri, mid, preferred_element_type=F32)
                    + jnp.dot(ltri, lo, preferred_element_type=F32))

    between()
    pre = []
    for (q, kk, v, _), cum in zip(chunks, cums):
        tot = cum[CHUNK - 1:CHUNK]
        bases = [jnp.zeros((1, D_WIDTH), F32)] + [cum[SUB * i - 1:SUB * i] for i in range(1, nsub)]
        bsel = jnp.concatenate([jnp.broadcast_to(b, (SUB, D_WIDTH)) for b in bases], axis=0)
        qe = (q * jnp.exp(cum)).astype(BF16)
        qd = q * jnp.exp(cum - bsel)
        kdec = (kk * jnp.exp(tot - cum)).astype(BF16)
        qblks, kdcs = [], []
        for sl in heads:
            qd_h = qd[:, sl]
            qblks.append(jnp.concatenate([jnp.where(rsub == i, qd_h, 0.0) for i in range(nsub)],
                                         axis=1).astype(BF16))
            kds = []
            for i in range(nsub):
                n = SUB * (i + 1)
                kd = kk[0:n, sl] * jnp.exp(bases[i][:, sl] - cum[0:n, sl])
                if n < CHUNK:
                    kd = jnp.concatenate([kd, jnp.zeros((CHUNK - n, D_KEY), F32)], axis=0)
                kds.append(kd)
            kdcs.append(jnp.concatenate(kds, axis=1).astype(BF16))
        pre.append((qe, kdec, v.astype(BF16), jnp.exp(tot), qblks, kdcs))

    atts, incs = [], []
    for qe, kdec, vb, etot, qblks, kdcs in pre:
        atts.append([jnp.where(tril, _dot_nt(qblks[h], kdcs[h]), 0.0).astype(BF16) for h in range(D_HEADS)])
        incs.append([_dot_tn(vb[:, sl], kdec[:, sl]) for sl in heads])

    between()
    intra = [[_dot(atts[c][h], pre[c][2][:, heads[h]]) for h in range(D_HEADS)] for c in range(len(chunks))]

    between()
    outs, ends = [], []
    state = None
    for c, (qe, kdec, vb, etot, _, _) in enumerate(pre):
        if init_states[c] is not None:
            state = init_states[c]
        o = [_dot_nt(qe[:, heads[h]], state[h]) + intra[c][h] for h in range(D_HEADS)]
        state = [etot[:, heads[h]] * state[h] + incs[c][h] for h in range(D_HEADS)]
        outs.append(jnp.concatenate(o, axis=1))
        ends.append(state)
    return outs, ends


def _hgrn_readout(o, g, norm_g):
    parts = []
    for h in range(D_HEADS):
        oh = o[:, D_VAL * h:D_VAL * (h + 1)]
        parts.append(oh * lax.rsqrt(jnp.mean(oh * oh, axis=-1, keepdims=True) + RMS_EPS))
    return jnp.concatenate(parts, axis=1) * norm_g * _silu(g)


def _l1_short_kernel(x_ref, rowmask_ref, c0_ref, s0_ref, convw_ref, cvec_ref, lb_ref, normg_ref,
                     w_in_ref, b_in_ref, w_o_ref, ln_ref, wg_ref, wu_ref, wd_ref,
                     h_ref, cout_ref, sout_ref, wg_out, wu_out, wd_out,
                     c_s, mix_s, h_s, hb_s, acc_s):
    mixers = functools.partial(_l1_short_mixers, x_ref, rowmask_ref, c0_ref, s0_ref, convw_ref, cvec_ref,
                               lb_ref, normg_ref, w_in_ref, b_in_ref, cout_ref, sout_ref, c_s, mix_s)
    _streamed_residual_block(mixers, x_ref, mix_s, w_o_ref, ln_ref, wg_ref, wu_ref, wd_ref,
                             h_ref, wg_out, wu_out, wd_out, h_s, hb_s, acc_s)


def _l1_short_mixers(x_ref, rowmask_ref, c0_ref, s0_ref, convw_ref, cvec_ref, lb_ref, normg_ref,
                     w_in_ref, b_in_ref, cout_ref, sout_ref, c_s, mix_s):
    ns = c0_ref.shape[0] + 1
    x = x_ref[...]
    rowmask = rowmask_ref[...]
    u = _c_project(x, rowmask, w_in_ref, b_in_ref)
    q, kk, v, lf, g = _d_project(x, rowmask, w_in_ref, b_in_ref, lb_ref)

    for s in range(ns):
        r0 = s * CHUNK
        cs = c_s.at[s]
        cs[C_OFF - C_HIST:C_OFF, :] = jnp.zeros((C_HIST, C_WIDTH), F32) if s == ns - 1 else c0_ref[s]
        cs[C_OFF:C_OFF + CHUNK, :] = u[r0:r0 + CHUNK]
        _conformer_conv(cs, C_OFF, CHUNK, convw_ref, cvec_ref, mix_s, r0)
        cout_ref[s] = cs[C_OFF + CHUNK - C_HIST:C_OFF + CHUNK, :]

    rows = [slice(s * CHUNK, (s + 1) * CHUNK) for s in range(ns)]
    outs, ends = _hgrn_chunks([(q[r], kk[r], v[r], lf[r]) for r in rows],
                              [[jnp.zeros((D_VAL, D_KEY), F32) if s == ns - 1 else s0_ref[s, h].T
                                for h in range(D_HEADS)] for s in range(ns)])
    for s in range(ns):
        mix_s[rows[s], C_WIDTH:] = _hgrn_readout(outs[s], g[rows[s]], normg_ref[...])
        for h in range(D_HEADS):
            sout_ref[s, h] = ends[s][h].T


def _l1_main_kernel(nt, x_ref, xprev_ref, c0_ref, s0_ref, convw_ref, cvec_ref, lb_ref, normg_ref,
                    w_in_ref, b_in_ref, w_o_ref, ln_ref, wg_ref, wu_ref, wd_ref,
                    h_ref, cout_ref, sout_ref,
                    c_s, st_s, mix_s):
    i, n, t = _tile_position(nt)
    tm = x_ref.shape[1]

    @pl.when(t == 0)
    def _():
        c_s[C_OFF - C_HIST:C_OFF, :] = c0_ref[0]
        for h in range(D_HEADS):
            st_s[h] = s0_ref[0, h].T

    _zero_on_first_step(i, mix_s)

    post = _SplitPostBlock(xprev_ref[0], mix_s[...], (w_o_ref, ln_ref, wg_ref, wu_ref, wd_ref), POST_ROW_GROUPS)
    post.run(1)

    x = x_ref[0]
    plan = iter(L1_FFN_STEPS)
    between = lambda: post.run(next(plan, 0))
    c_s[C_OFF:, :] = _c_project(x, None, w_in_ref, b_in_ref)
    q, kk, v, lf, g = _d_project(x, None, w_in_ref, b_in_ref, lb_ref)
    between()
    _conformer_conv(c_s, C_OFF, tm, convw_ref, cvec_ref, mix_s, 0)
    c_s[C_OFF - C_HIST:C_OFF, :] = c_s[C_OFF + tm - C_HIST:C_OFF + tm, :]

    nc = tm // CHUNK
    rows = [slice(c * CHUNK, (c + 1) * CHUNK) for c in range(nc)]
    outs, ends = _hgrn_chunks([(q[r], kk[r], v[r], lf[r]) for r in rows],
                              [[st_s[h] for h in range(D_HEADS)]] + [None] * (nc - 1), between)
    for c in range(nc):
        mix_s[rows[c], C_WIDTH:] = _hgrn_readout(outs[c], g[rows[c]], normg_ref[...])
    for h in range(D_HEADS):
        st_s[h] = ends[-1][h]

    h_ref[0] = post.result()

    @pl.when(jnp.logical_and(t == nt - 1, i < n))
    def _():
        cout_ref[0] = c_s[C_OFF - C_HIST:C_OFF, :]
        for h in range(D_HEADS):
            sout_ref[0, h] = st_s[h].T


def _whole(shape):
    zeros = (0,) * len(shape)
    return pl.BlockSpec(shape, lambda *_: zeros, pipeline_mode=pl.Buffered(1))


def _const_spec(a):
    return pl.BlockSpec(memory_space=pltpu.SMEM) if a.ndim == 1 else _whole(a.shape)


N_FFN_BLOCKS = FFN_HIDDEN // FFN_BLOCK


def _streamed_ffn_specs(layer):
    in_specs = [pl.BlockSpec((1, D_MODEL, FFN_BLOCK), lambda j: (layer, 0, j)),
                pl.BlockSpec((1, D_MODEL, FFN_BLOCK), lambda j: (layer, 0, N_FFN_BLOCKS + j)),
                pl.BlockSpec((1, FFN_BLOCK, D_MODEL), lambda j: (layer, j, 0))]
    out_specs = [pl.BlockSpec((D_MODEL, FFN_BLOCK), lambda j: (0, j)),
                 pl.BlockSpec((D_MODEL, FFN_BLOCK), lambda j: (0, j)),
                 pl.BlockSpec((FFN_BLOCK, D_MODEL), lambda j: (j, 0))]
    out_shape = [jax.ShapeDtypeStruct((D_MODEL, FFN_HIDDEN), BF16),
                 jax.ShapeDtypeStruct((D_MODEL, FFN_HIDDEN), BF16),
                 jax.ShapeDtypeStruct((FFN_HIDDEN, D_MODEL), BF16)]
    return in_specs, out_specs, out_shape


def _residual_scratch(rows):
    return [pltpu.VMEM((rows, D_MODEL), F32), pltpu.VMEM((rows, D_MODEL), BF16), pltpu.VMEM((rows, D_MODEL), F32)]


def _whole_out(shape):
    zeros = (0,) * len(shape)
    return pl.BlockSpec(shape, lambda *_: zeros)


def _params(n_axes):
    return pltpu.CompilerParams(dimension_semantics=("arbitrary",) * n_axes,
                                vmem_limit_bytes=VMEM_LIMIT)


def _l0_short(x, rowmask, k0, v0, u0, kvb, consts, weights, w_gu, w_down, layer):
    rows, ns = x.shape[0], k0.shape[0] + 1
    ins = (x, rowmask, k0, v0, u0, kvb) + consts + weights
    ffn_in, ffn_out, ffn_shape = _streamed_ffn_specs(layer)
    return pl.pallas_call(
        _l0_short_kernel,
        grid=(N_FFN_BLOCKS,),
        in_specs=[_const_spec(a) for a in ins] + ffn_in,
        out_specs=[_whole_out((rows, D_MODEL)), _whole_out((ns, WINDOW, KV_W)),
                   _whole_out((ns, WINDOW, KV_W)), _whole_out((ns, B_CONV - 1, B_WIDTH))] + ffn_out,
        out_shape=[jax.ShapeDtypeStruct((rows, D_MODEL), F32),
                   jax.ShapeDtypeStruct((ns, WINDOW, KV_W), F32),
                   jax.ShapeDtypeStruct((ns, WINDOW, KV_W), F32),
                   jax.ShapeDtypeStruct((ns, B_CONV - 1, B_WIDTH), F32)] + ffn_shape,
        scratch_shapes=[pltpu.VMEM((ns, B_OFF + CHUNK, B_WIDTH), F32), pltpu.VMEM((rows, D_MODEL), F32)]
        + _residual_scratch(rows),
        compiler_params=_params(1),
        name="l0_short",
    )(*ins, w_gu, w_gu, w_down)


class _SkewedTiles:
    def __init__(self, nb, nt):
        self.nt, self.n = nt, nb * nt

    def _cur(self, i):
        return jnp.minimum(i, self.n - 1)

    def tile(self):
        return pl.BlockSpec((1, TILE, D_MODEL), lambda i: (self._cur(i) // self.nt, self._cur(i) % self.nt, 0))

    def prev_tile(self):
        prev = lambda i: jnp.maximum(i - 1, 0)
        return pl.BlockSpec((1, TILE, D_MODEL), lambda i: (prev(i) // self.nt, prev(i) % self.nt, 0))

    def fixed(self, shape, idx):
        zeros = (0,) * len(shape)
        return pl.BlockSpec((1,) + shape, lambda i: (idx,) + zeros)

    def per_prompt(self, shape):
        zeros = (0,) * len(shape)
        return pl.BlockSpec((1,) + shape, lambda i: (self._cur(i) // self.nt,) + zeros)


def _l0_main(x, k0, v0, u0, kvb, consts, weights, meta_idx):
    nb, seq, _ = x.shape
    nt = seq // TILE
    sk = _SkewedTiles(nb, nt)
    ins = (x, x, k0, v0, u0, kvb) + consts + weights
    in_specs = [sk.tile(), sk.prev_tile(),
                sk.fixed((WINDOW, KV_W), meta_idx), sk.fixed((WINDOW, KV_W), meta_idx),
                sk.fixed((B_CONV - 1, B_WIDTH), meta_idx),
                _whole(kvb.shape)] + [_const_spec(a) for a in consts + weights]
    return pl.pallas_call(
        functools.partial(_l0_main_kernel, nt),
        grid=(nb * nt + 1,),
        in_specs=in_specs,
        out_specs=[sk.prev_tile(), sk.per_prompt((WINDOW, KV_W)), sk.per_prompt((WINDOW, KV_W)),
                   sk.per_prompt((B_CONV - 1, B_WIDTH))],
        out_shape=[jax.ShapeDtypeStruct((nb, seq, D_MODEL), F32),
                   jax.ShapeDtypeStruct((nb, WINDOW, KV_W), F32),
                   jax.ShapeDtypeStruct((nb, WINDOW, KV_W), F32),
                   jax.ShapeDtypeStruct((nb, B_CONV - 1, B_WIDTH), F32)],
        scratch_shapes=[pltpu.VMEM((WINDOW + TILE, KV_W), F32),
                        pltpu.VMEM((WINDOW + TILE, KV_W), F32), pltpu.VMEM((B_OFF + TILE, B_WIDTH), F32),
                        pltpu.VMEM((TILE, D_MODEL), F32)],
        compiler_params=_params(1),
        name="l0_main",
    )(*ins)


def _l1_short(x, rowmask, c0, s0, consts, weights, w_gu, w_down, layer):
    rows, ns = x.shape[0], c0.shape[0] + 1
    ins = (x, rowmask, c0, s0) + consts + weights
    ffn_in, ffn_out, ffn_shape = _streamed_ffn_specs(layer)
    return pl.pallas_call(
        _l1_short_kernel,
        grid=(N_FFN_BLOCKS,),
        in_specs=[_const_spec(a) for a in ins] + ffn_in,
        out_specs=[_whole_out((rows, D_MODEL)), _whole_out((ns, C_HIST, C_WIDTH)),
                   _whole_out((ns, D_HEADS, D_KEY, D_VAL))] + ffn_out,
        out_shape=[jax.ShapeDtypeStruct((rows, D_MODEL), F32),
                   jax.ShapeDtypeStruct((ns, C_HIST, C_WIDTH), F32),
                   jax.ShapeDtypeStruct((ns, D_HEADS, D_KEY, D_VAL), F32)] + ffn_shape,
        scratch_shapes=[pltpu.VMEM((ns, C_OFF + CHUNK, C_WIDTH), F32), pltpu.VMEM((rows, D_MODEL), F32)]
        + _residual_scratch(rows),
        compiler_params=_params(1),
        name="l1_short",
    )(*ins, w_gu, w_gu, w_down)


def _l1_main(x, c0, s0, consts, weights, meta_idx):
    nb, seq, _ = x.shape
    nt = seq // TILE
    sk = _SkewedTiles(nb, nt)
    ins = (x, x, c0, s0) + consts + weights
    in_specs = [sk.tile(), sk.prev_tile(), sk.fixed((C_HIST, C_WIDTH), meta_idx),
                sk.fixed((D_HEADS, D_KEY, D_VAL), meta_idx),
                ] + [_const_spec(a) for a in consts + weights]
    return pl.pallas_call(
        functools.partial(_l1_main_kernel, nt),
        grid=(nb * nt + 1,),
        in_specs=in_specs,
        out_specs=[sk.prev_tile(), sk.per_prompt((C_HIST, C_WIDTH)),
                   sk.per_prompt((D_HEADS, D_KEY, D_VAL))],
        out_shape=[jax.ShapeDtypeStruct((nb, seq, D_MODEL), F32),
                   jax.ShapeDtypeStruct((nb, C_HIST, C_WIDTH), F32),
                   jax.ShapeDtypeStruct((nb, D_HEADS, D_KEY, D_VAL), F32)],
        scratch_shapes=[pltpu.VMEM((C_OFF + TILE, C_WIDTH), F32), pltpu.VMEM((D_HEADS, D_VAL, D_KEY), F32),
                        pltpu.VMEM((TILE, D_MODEL), F32)],
        compiler_params=_params(1),
        name="l1_main",
    )(*ins)


def _alibi_tables():
    slopes = np.exp2(-8.0 * np.arange(1, A_HEADS + 1, dtype=np.float32) / A_HEADS).astype(np.float32)
    i = np.arange(CHUNK, dtype=np.float32)[None, :]
    j = np.arange(BAND, dtype=np.float32)[:, None]
    dist = np.abs(WINDOW + i - j).astype(np.float32)
    bias = np.zeros((A_KV_HEADS, 2 * BAND, 2 * CHUNK), np.float32)
    for g in range(A_KV_HEADS):
        for e in range(2):
            for half, head in enumerate((4 * g + e, 4 * g + 2 + e)):
                cols = slice(CHUNK * half, CHUNK * (half + 1))
                bias[g, BAND * e:BAND * (e + 1), cols] = -slopes[head] * dist
    return bias


def kernel(x_prompt, x_sample, cache_k_a, cache_v_a, state_conv_b, state_conv_c, state_hgrn, meta_tokens,
           ab_w_in, ab_b_in, a_sinks, b_conv_w, ab_w_o, cd_w_in, cd_b_in, c_conv_w, c_conv_b, c_ln_g,
           c_ln_b, d_lower_bounds, d_norm_g, cd_w_o, ln1_g, ln1_b, ln2_g, ln2_b, ffn_w_gu, ffn_w_down):
    nb = x_prompt.shape[0]
    ns = x_sample.shape[0]
    assert x_sample.shape[1] == CHUNK and cache_k_a.shape[1] == WINDOW
    assert x_prompt.shape[1] % TILE == 0
    meta_idx = ns
    n_short = ns + 1
    rows = n_short * CHUNK

    xs = jnp.concatenate([x_sample.reshape(ns * CHUNK, D_MODEL),
                          jnp.zeros((PADF, D_MODEL), F32), meta_tokens.astype(F32)], axis=0)
    rowmask = np.ones((rows, 1), np.float32)
    rowmask[ns * CHUNK:ns * CHUNK + PADF] = 0.0
    rowmask = jnp.asarray(rowmask)

    kvb_short = np.zeros((BAND, 2 * CHUNK), np.float32)
    kvb_short[:WINDOW + PADF] = NEG
    kvb_main = np.zeros((2, BAND, 2 * CHUNK), np.float32)
    kvb_main[0, :CHUNK + PADF] = NEG
    kvb_main[1, :PADF] = NEG

    a_consts = (jnp.asarray(_alibi_tables()), a_sinks.astype(F32), b_conv_w.astype(F32))

    k0 = cache_k_a.reshape(ns, WINDOW, KV_W)
    v0 = cache_v_a.reshape(ns, WINDOW, KV_W)
    u0, c0, s0 = state_conv_b, state_conv_c, state_hgrn

    def layer_weights(l, w_in, b_in, w_o):
        ln = jnp.stack([ln1_g[l], ln1_b[l], ln2_g[l], ln2_b[l]]).astype(F32)
        return (w_in.astype(BF16), b_in.astype(F32)[None, :], w_o.astype(BF16), ln)

    w0 = layer_weights(0, ab_w_in, ab_b_in, ab_w_o)
    w1 = layer_weights(1, cd_w_in, cd_b_in, cd_w_o)
    w_gu, w_down = ffn_w_gu.astype(F32), ffn_w_down.astype(F32)

    hs0, k_s, v_s, u_s, *ffn0 = _l0_short(xs, rowmask, k0, v0, u0, jnp.asarray(kvb_short), a_consts, w0,
                                          w_gu, w_down, 0)
    hp0, k_p, v_p, u_p = _l0_main(x_prompt, k_s, v_s, u_s, jnp.asarray(kvb_main), a_consts,
                                  w0 + tuple(ffn0), meta_idx)

    cvec = jnp.stack([c_conv_b, c_ln_g, c_ln_b]).astype(F32)
    c_consts = (c_conv_w.astype(F32), cvec, d_lower_bounds.astype(F32), d_norm_g.astype(F32)[None, :])
    hs1, c_s, s_s, *ffn1 = _l1_short(hs0, rowmask, c0, s0, c_consts, w1, w_gu, w_down, 1)
    hp1, c_p, s_p = _l1_main(hp0, c_s, s_s, c_consts, w1 + tuple(ffn1), meta_idx)

    kv = lambda a: a.reshape(a.shape[0], WINDOW, A_KV_HEADS, HEAD_DIM)
    y_sample = hs1[:ns * CHUNK].reshape(ns, CHUNK, D_MODEL)
    return (hp1, y_sample, kv(k_p), kv(v_p), u_p, c_p, s_p,
            kv(k_s[:ns]), kv(v_s[:ns]), u_s[:ns], c_s[:ns], s_s[:ns])
```

```python
import functools

import jax
import jax.numpy as jnp
import numpy as np
from jax import lax
from jax.experimental import pallas as pl
from jax.experimental.pallas import tpu as pltpu

F32 = jnp.float32
BF16 = jnp.bfloat16

D_MODEL = 1024
CHUNK = 64
N_META = 16
PADF = CHUNK - N_META
A_HEADS = 8
A_KV_HEADS = 2
HEAD_DIM = 64
WINDOW = 128
KV_W = A_KV_HEADS * HEAD_DIM
Q_W = A_HEADS * HEAD_DIM
BAND = WINDOW + CHUNK
B_WIDTH = 512
B_CONV = 3
C_WIDTH = 512
C_CONV = 31
D_HEADS = 4
D_KEY = 128
D_VAL = 128
D_WIDTH = D_HEADS * D_VAL
SUB = 16
FFN_HIDDEN = 2816
DEPTH = 2
ALPHA = (2 * DEPTH) ** 0.25
LN_EPS = 1e-5
RMS_EPS = 1e-6
NEG = -1e30

AB_IN = Q_W + 2 * KV_W + 3 * B_WIDTH
CD_IN = 2 * C_WIDTH + 4 * D_WIDTH

TILE = 512
FFN_BLOCK = 256
POST_ROW_GROUPS = 2
L0_FFN_STEPS_BETWEEN = (POST_ROW_GROUPS,) * (2 * TILE // CHUNK)
L1_FFN_STEPS = tuple(POST_ROW_GROUPS * n for n in (8, 2, 1, 1))
SUBLANES = 8
LANES = 128
CONV_ROWS = 32
C_HIST = C_CONV - 1
C_OFF = 32
B_OFF = 8
VMEM_LIMIT = 60 * 1024 * 1024


def _dot(a, b):
    return jnp.dot(a.astype(BF16), b.astype(BF16), preferred_element_type=F32)


def _dot_nt(a, b):
    return lax.dot_general(a.astype(BF16), b.astype(BF16), (((1,), (1,)), ((), ())),
                           preferred_element_type=F32)


def _dot_tn(a, b):
    return lax.dot_general(a.astype(BF16), b.astype(BF16), (((0,), (0,)), ((), ())),
                           preferred_element_type=F32)


def _sigmoid(x):
    return 0.5 * jnp.tanh(0.5 * x) + 0.5


def _silu(x):
    return x * _sigmoid(x)


def _layer_norm(x, g, b):
    mu = jnp.mean(x, axis=-1, keepdims=True)
    xc = x - mu
    var = jnp.mean(xc * xc, axis=-1, keepdims=True)
    return xc * lax.rsqrt(var + LN_EPS) * g + b


class _PostBlock:
    N_BLOCKS = FFN_HIDDEN // FFN_BLOCK
    N_STEPS = 2 + N_BLOCKS

    def __init__(self, x, mix, w_o_ref, ln_ref, wg_ref, wu_ref, wd_ref):
        self.x, self.mix = x, mix
        self.w_o_ref, self.ln_ref = w_o_ref, ln_ref
        self.wg_ref, self.wu_ref, self.wd_ref = wg_ref, wu_ref, wd_ref
        self.done = 0
        self.h = self.hb = self.acc = self.act = None

    def _gate_up(self, j):
        lo = j * FFN_BLOCK
        gate = jnp.dot(self.hb, self.wg_ref[:, lo:lo + FFN_BLOCK], preferred_element_type=F32)
        up = jnp.dot(self.hb, self.wu_ref[:, lo:lo + FFN_BLOCK], preferred_element_type=F32)
        return (_silu(gate) * up).astype(BF16)

    def _step(self, j):
        if j == 0:
            ln = self.ln_ref
            self.h = _layer_norm(ALPHA * self.x + _dot(self.mix, self.w_o_ref[...]), ln[0:1], ln[1:2])
            self.hb = self.h.astype(BF16)
            return
        prev = self.act
        self.act = self._gate_up(j - 1) if j - 1 < self.N_BLOCKS else None
        if prev is not None:
            lo = (j - 2) * FFN_BLOCK
            part = jnp.dot(prev, self.wd_ref[lo:lo + FFN_BLOCK, :], preferred_element_type=F32)
            self.acc = part if self.acc is None else self.acc + part

    def run(self, n):
        for _ in range(n):
            if self.done < self.N_STEPS:
                self._step(self.done)
                self.done += 1

    def result(self):
        self.run(self.N_STEPS)
        return _layer_norm(ALPHA * self.h + self.acc, self.ln_ref[2:3], self.ln_ref[3:4])


class _SplitPostBlock:
    def __init__(self, x, mix, weight_refs, groups):
        rows = x.shape[0] // groups
        self.parts = [_PostBlock(x[g * rows:(g + 1) * rows], mix[g * rows:(g + 1) * rows], *weight_refs)
                      for g in range(groups)]
        self.outs = []

    def run(self, n):
        for _ in range(n):
            if len(self.outs) == len(self.parts):
                return
            part = self.parts[len(self.outs)]
            part.run(1)
            if part.done == part.N_STEPS:
                self.outs.append(part.result())

    def result(self):
        self.run(sum(part.N_STEPS for part in self.parts))
        return jnp.concatenate(self.outs, axis=0)


def _no_op():
    pass


def _attn_chunks(chunks, alibi_ref, sink_ref, between=_no_op):
    lane = lax.broadcasted_iota(jnp.int32, (BAND, KV_W), 1)
    low = lane < HEAD_DIM
    high = jnp.logical_not(low)
    sink_rows = [jnp.where(low[0:1], sink_ref[4 * g + e], sink_ref[4 * g + 2 + e])
                 for g in range(A_KV_HEADS) for e in range(2)]

    def extend(band, swapped, g, e):
        src = band if g == e else swapped
        return jnp.where(low if e == 0 else high, src, 0.0).astype(BF16)

    scores = []
    for q, kband, _, kvbias in chunks:
        kswap = pltpu.roll(kband, HEAD_DIM, axis=1)
        for g in range(A_KV_HEADS):
            qg = jnp.concatenate([q[:, 256 * g:256 * g + 128], q[:, 256 * g + 128:256 * g + 256]],
                                 axis=0).astype(BF16)
            k2 = jnp.concatenate([extend(kband, kswap, g, e) for e in range(2)], axis=0)
            s = _dot_nt(k2, qg) + alibi_ref[g]
            for e in range(2):
                se = s[BAND * e:BAND * (e + 1)]
                scores.append(se if kvbias is None else se + kvbias)
        between()

    probs = []
    for idx, s in enumerate(scores):
        sink = sink_rows[idx % (2 * A_KV_HEADS)]
        m = jnp.maximum(jnp.max(s, axis=0, keepdims=True), sink)
        p = jnp.exp(s - m)
        den = jnp.sum(p, axis=0, keepdims=True) + jnp.exp(sink - m)
        probs.append((p * (1.0 / den)).astype(BF16))
        if idx % (2 * A_KV_HEADS) == 2 * A_KV_HEADS - 1:
            between()

    outs = []
    for ci, (_, _, vband, _) in enumerate(chunks):
        vswap = pltpu.roll(vband, HEAD_DIM, axis=1)
        parts = []
        for g in range(A_KV_HEADS):
            base = ci * 2 * A_KV_HEADS + 2 * g
            p2 = jnp.concatenate([probs[base], probs[base + 1]], axis=0)
            v2 = jnp.concatenate([extend(vband, vswap, g, e) for e in range(2)], axis=0)
            o = _dot_tn(p2, v2)
            parts += [o[0:CHUNK], o[CHUNK:2 * CHUNK]]
        outs.append(jnp.concatenate(parts, axis=1))
    return outs


def _ab_project(x, w_in_ref, b_in_ref):
    proj = _dot(x, w_in_ref[...]) + b_in_ref[...]
    q = proj[:, 0:Q_W] * (HEAD_DIM ** -0.5)
    k = proj[:, Q_W:Q_W + KV_W]
    v = proj[:, Q_W + KV_W:Q_W + 2 * KV_W]
    o = Q_W + 2 * KV_W
    bg = proj[:, o:o + B_WIDTH]
    u = proj[:, o + B_WIDTH:o + 2 * B_WIDTH] * proj[:, o + 2 * B_WIDTH:o + 3 * B_WIDTH]
    return q, k, v, bg, u


def _short_conv(u_ref, base, rows, w_ref):
    out = None
    for j in range(B_CONV):
        term = u_ref[pl.ds(base - (B_CONV - 1) + j, rows), :] * w_ref[j:j + 1, :]
        out = term if out is None else out + term
    return out


def _streamed_residual_block(first, x_ref, mix_s, w_o_ref, ln_ref, wg_ref, wu_ref, wd_ref,
                             h_ref, wg_out, wu_out, wd_out, h_s, hb_s, acc_s):
    j = pl.program_id(0)

    @pl.when(j == 0)
    def _():
        first()
        h = _layer_norm(ALPHA * x_ref[...] + _dot(mix_s[...], w_o_ref[...]), ln_ref[0:1], ln_ref[1:2])
        h_s[...] = h
        hb_s[...] = h.astype(BF16)
        acc_s[...] = jnp.zeros_like(acc_s)

    wg = wg_ref[0].astype(BF16)
    wu = wu_ref[0].astype(BF16)
    wd = wd_ref[0].astype(BF16)
    wg_out[...] = wg
    wu_out[...] = wu
    wd_out[...] = wd
    hb = hb_s[...]
    gate = jnp.dot(hb, wg, preferred_element_type=F32)
    up = jnp.dot(hb, wu, preferred_element_type=F32)
    acc_s[...] += jnp.dot((_silu(gate) * up).astype(BF16), wd, preferred_element_type=F32)

    @pl.when(j == pl.num_programs(0) - 1)
    def _():
        h_ref[...] = _layer_norm(ALPHA * h_s[...] + acc_s[...], ln_ref[2:3], ln_ref[3:4])


def _l0_short_kernel(x_ref, rowmask_ref, k0_ref, v0_ref, u0_ref, kvb_ref, alibi_ref, sink_ref,
                     convw_ref, w_in_ref, b_in_ref, w_o_ref, ln_ref, wg_ref, wu_ref, wd_ref,
                     h_ref, kout_ref, vout_ref, uout_ref, wg_out, wu_out, wd_out,
                     u_s, mix_s, h_s, hb_s, acc_s):
    mixers = functools.partial(_l0_short_mixers, x_ref, rowmask_ref, k0_ref, v0_ref, u0_ref, kvb_ref,
                               alibi_ref, sink_ref, convw_ref, w_in_ref, b_in_ref,
                               kout_ref, vout_ref, uout_ref, u_s, mix_s)
    _streamed_residual_block(mixers, x_ref, mix_s, w_o_ref, ln_ref, wg_ref, wu_ref, wd_ref,
                             h_ref, wg_out, wu_out, wd_out, h_s, hb_s, acc_s)


def _l0_short_mixers(x_ref, rowmask_ref, k0_ref, v0_ref, u0_ref, kvb_ref, alibi_ref, sink_ref,
                     convw_ref, w_in_ref, b_in_ref, kout_ref, vout_ref, uout_ref, u_s, mix_s):
    ns = k0_ref.shape[0] + 1
    x = x_ref[...]
    q, k, v, bg, u = _ab_project(x, w_in_ref, b_in_ref)
    u = u * rowmask_ref[...]

    chunks = []
    for s in range(ns):
        r0 = s * CHUNK
        meta = s == ns - 1
        empty = jnp.zeros((WINDOW, KV_W), F32)
        kband = jnp.concatenate([empty if meta else k0_ref[s], k[r0:r0 + CHUNK]], axis=0)
        vband = jnp.concatenate([empty if meta else v0_ref[s], v[r0:r0 + CHUNK]], axis=0)
        kout_ref[s] = kband[CHUNK:]
        vout_ref[s] = vband[CHUNK:]
        chunks.append((q[r0:r0 + CHUNK], kband, vband, kvb_ref[...] if s == ns - 1 else None))
    for s, att in enumerate(_attn_chunks(chunks, alibi_ref, sink_ref)):
        mix_s[s * CHUNK:(s + 1) * CHUNK, 0:Q_W] = att.astype(BF16)

    for s in range(ns):
        r0 = s * CHUNK
        us = u_s.at[s]
        us[B_OFF - 2:B_OFF, :] = jnp.zeros((B_CONV - 1, B_WIDTH), F32) if s == ns - 1 else u0_ref[s]
        us[B_OFF:B_OFF + CHUNK, :] = u[r0:r0 + CHUNK]
        cb = _short_conv(us, B_OFF, CHUNK, convw_ref)
        mix_s[r0:r0 + CHUNK, Q_W:] = (bg[r0:r0 + CHUNK] * cb).astype(BF16)
        uout_ref[s] = u[r0 + CHUNK - 2:r0 + CHUNK]


def _tile_position(nt):
    i = pl.program_id(0)
    n = pl.num_programs(0) - 1
    return i, n, lax.rem(jnp.minimum(i, n - 1), nt)


def _zero_on_first_step(i, mix_s):
    @pl.when(i == 0)
    def _():
        mix_s[...] = jnp.zeros_like(mix_s)


def _l0_main_kernel(nt, x_ref, xprev_ref, k0_ref, v0_ref, u0_ref, kvb_ref, alibi_ref, sink_ref,
                    convw_ref, w_in_ref, b_in_ref, w_o_ref, ln_ref, wg_ref, wu_ref, wd_ref,
                    h_ref, kout_ref, vout_ref, uout_ref,
                    k_s, v_s, u_s, mix_s):
    i, n, t = _tile_position(nt)
    tm = x_ref.shape[1]

    @pl.when(t == 0)
    def _():
        k_s[0:WINDOW, :] = k0_ref[0]
        v_s[0:WINDOW, :] = v0_ref[0]
        u_s[B_OFF - 2:B_OFF, :] = u0_ref[0]

    _zero_on_first_step(i, mix_s)

    post = _SplitPostBlock(xprev_ref[0], mix_s[...], (w_o_ref, ln_ref, wg_ref, wu_ref, wd_ref), POST_ROW_GROUPS)
    post.run(1)

    q, k, v, bg, u = _ab_project(x_ref[0], w_in_ref, b_in_ref)
    k_s[WINDOW:, :] = k
    v_s[WINDOW:, :] = v
    u_s[B_OFF:, :] = u
    mix_s[:, Q_W:] = (bg * _short_conv(u_s, B_OFF, tm, convw_ref)).astype(BF16)

    first = t == 0
    chunks = []
    for c in range(tm // CHUNK):
        r0 = c * CHUNK
        kvbias = jnp.where(first, kvb_ref[c], 0.0) if c < 2 else None
        chunks.append((q[r0:r0 + CHUNK], k_s[r0:r0 + BAND, :], v_s[r0:r0 + BAND, :], kvbias))
    plan = iter(L0_FFN_STEPS_BETWEEN)
    outs = _attn_chunks(chunks, alibi_ref, sink_ref, between=lambda: post.run(next(plan, 0)))
    for c, att in enumerate(outs):
        mix_s[c * CHUNK:(c + 1) * CHUNK, 0:Q_W] = att.astype(BF16)

    h_ref[0] = post.result()

    k_s[0:WINDOW, :] = k_s[tm:tm + WINDOW, :]
    v_s[0:WINDOW, :] = v_s[tm:tm + WINDOW, :]
    u_s[B_OFF - 2:B_OFF, :] = u_s[B_OFF + tm - 2:B_OFF + tm, :]

    @pl.when(jnp.logical_and(t == nt - 1, i < n))
    def _():
        kout_ref[0] = k_s[0:WINDOW, :]
        vout_ref[0] = v_s[0:WINDOW, :]
        uout_ref[0] = u_s[B_OFF - 2:B_OFF, :]


def _c_project(x, rowmask, w_in_ref, b_in_ref):
    w = C_WIDTH
    proj = _dot(x, w_in_ref[:, 0:2 * w]) + b_in_ref[:, 0:2 * w]
    u = proj[:, 0:w] * _sigmoid(proj[:, w:2 * w])
    return u if rowmask is None else u * rowmask


def _d_project(x, rowmask, w_in_ref, b_in_ref, lb_ref):
    w = D_WIDTH
    proj = _dot(x, w_in_ref[:, 2 * C_WIDTH:]) + b_in_ref[:, 2 * C_WIDTH:]
    q = proj[:, 0:w]
    d0 = lb_ref[0:1]
    d1 = lb_ref[1:2]
    mx = jnp.maximum(d0, d1)
    e0 = jnp.exp(d0 - mx)
    e1 = jnp.exp(d1 - mx)
    p0 = e0 / (e0 + e1)
    p1 = e1 / (e0 + e1)
    lb = (p0 + p1) - p0
    forget = lb + (1.0 - lb) * _sigmoid(proj[:, w:2 * w])
    kk = 1.0 - forget
    lf = jnp.log(forget)
    v = proj[:, 2 * w:3 * w]
    g = proj[:, 3 * w:4 * w]
    if rowmask is not None:
        q = q * rowmask
        kk = kk * rowmask
        v = v * rowmask
        lf = lf * rowmask
    return q, kk, v, lf, g


def _conformer_conv(c_ref, base, rows, convw_ref, cvec_ref, out_ref, out_base):
    for r0 in range(0, rows, CONV_ROWS):
        cols = []
        for c0 in range(0, C_WIDTH, LANES):
            lanes = slice(c0, c0 + LANES)
            win = c_ref[base - C_OFF + r0:base + r0 + CONV_ROWS, lanes]
            acc = None
            for phase in range(SUBLANES):
                shifted = win if phase == 0 else pltpu.roll(win, CONV_ROWS + C_OFF - phase, axis=0)
                for j in range(C_CONV):
                    off = C_OFF - C_HIST + j
                    if off % SUBLANES != phase:
                        continue
                    lo = off - phase
                    term = shifted[lo:lo + CONV_ROWS] * convw_ref[j:j + 1, lanes]
                    acc = term if acc is None else acc + term
            cols.append(acc)
        acc = jnp.concatenate(cols, axis=1)
        y = _layer_norm(acc + cvec_ref[0:1], cvec_ref[1:2], cvec_ref[2:3])
        out_ref[out_base + r0:out_base + r0 + CONV_ROWS, 0:C_WIDTH] = _silu(y).astype(out_ref.dtype)


def _hgrn_chunks(chunks, init_states, between=_no_op):
    row = lax.broadcasted_iota(jnp.int32, (CHUNK, CHUNK), 0)
    col = lax.broadcasted_iota(jnp.int32, (CHUNK, CHUNK), 1)
    tril = col <= row
    ltri = jnp.where(tril, 1.0, 0.0).astype(BF16)
    nsub = CHUNK // SUB
    rsub = jnp.right_shift(lax.broadcasted_iota(jnp.int32, (CHUNK, D_KEY), 0), SUB.bit_length() - 1)
    heads = [slice(D_KEY * h, D_KEY * (h + 1)) for h in range(D_HEADS)]

    cums = []
    for _, _, _, lf in chunks:
        hi = lf.astype(BF16)
        r1 = lf - hi.astype(F32)
        mid = r1.astype(BF16)
        lo = (r1 - mid.astype(F32)).astype(BF16)
        cums.append(jnp.dot(ltri, hi, preferred_element_type=F32)
                    + jnp.dot(ltri, mid, preferred_element_type=F32)
                    + jnp.dot(ltri, lo, preferred_element_type=F32))

    between()
    pre = []
    for (q, kk, v, _), cum in zip(chunks, cums):
        tot = cum[CHUNK - 1:CHUNK]
        bases = [jnp.zeros((1, D_WIDTH), F32)] + [cum[SUB * i - 1:SUB * i] for i in range(1, nsub)]
        bsel = jnp.concatenate([jnp.broadcast_to(b, (SUB, D_WIDTH)) for b in bases], axis=0)
        qe = (q * jnp.exp(cum)).astype(BF16)
        qd = q * jnp.exp(cum - bsel)
        kdec = (kk * jnp.exp(tot - cum)).astype(BF16)
        qblks, kdcs = [], []
        for sl in heads:
            qd_h = qd[:, sl]
            qblks.append(jnp.concatenate([jnp.where(rsub == i, qd_h, 0.0) for i in range(nsub)],
                                         axis=1).astype(BF16))
            kds = []
            for i in range(nsub):
                n = SUB * (i + 1)
                kd = kk[0:n, sl] * jnp.exp(bases[i][:, sl] - cum[0:n, sl])
                if n < CHUNK:
                    kd = jnp.concatenate([kd, jnp.zeros((CHUNK - n, D_KEY), F32)], axis=0)
                kds.append(kd)
            kdcs.append(jnp.concatenate(kds, axis=1).astype(BF16))
        pre.append((qe, kdec, v.astype(BF16), jnp.exp(tot), qblks, kdcs))

    atts, incs = [], []
    for qe, kdec, vb, etot, qblks, kdcs in pre:
        atts.append([jnp.where(tril, _dot_nt(qblks[h], kdcs[h]), 0.0).astype(BF16) for h in range(D_HEADS)])
        incs.append([_dot_tn(vb[:, sl], kdec[:, sl]) for sl in heads])

    between()
    intra = [[_dot(atts[c][h], pre[c][2][:, heads[h]]) for h in range(D_HEADS)] for c in range(len(chunks))]

    between()
    outs, ends = [], []
    state = None
    for c, (qe, kdec, vb, etot, _, _) in enumerate(pre):
        if init_states[c] is not None:
            state = init_states[c]
        o = [_dot_nt(qe[:, heads[h]], state[h]) + intra[c][h] for h in range(D_HEADS)]
        state = [etot[:, heads[h]] * state[h] + incs[c][h] for h in range(D_HEADS)]
        outs.append(jnp.concatenate(o, axis=1))
        ends.append(state)
    return outs, ends


def _hgrn_readout(o, g, norm_g):
    parts = []
    for h in range(D_HEADS):
        oh = o[:, D_VAL * h:D_VAL * (h + 1)]
        parts.append(oh * lax.rsqrt(jnp.mean(oh * oh, axis=-1, keepdims=True) + RMS_EPS))
    return jnp.concatenate(parts, axis=1) * norm_g * _silu(g)


def _l1_short_kernel(x_ref, rowmask_ref, c0_ref, s0_ref, convw_ref, cvec_ref, lb_ref, normg_ref,
                     w_in_ref, b_in_ref, w_o_ref, ln_ref, wg_ref, wu_ref, wd_ref,
                     h_ref, cout_ref, sout_ref, wg_out, wu_out, wd_out,
                     c_s, mix_s, h_s, hb_s, acc_s):
    mixers = functools.partial(_l1_short_mixers, x_ref, rowmask_ref, c0_ref, s0_ref, convw_ref, cvec_ref,
                               lb_ref, normg_ref, w_in_ref, b_in_ref, cout_ref, sout_ref, c_s, mix_s)
    _streamed_residual_block(mixers, x_ref, mix_s, w_o_ref, ln_ref, wg_ref, wu_ref, wd_ref,
                             h_ref, wg_out, wu_out, wd_out, h_s, hb_s, acc_s)


def _l1_short_mixers(x_ref, rowmask_ref, c0_ref, s0_ref, convw_ref, cvec_ref, lb_ref, normg_ref,
                     w_in_ref, b_in_ref, cout_ref, sout_ref, c_s, mix_s):
    ns = c0_ref.shape[0] + 1
    x = x_ref[...]
    rowmask = rowmask_ref[...]
    u = _c_project(x, rowmask, w_in_ref, b_in_ref)
    q, kk, v, lf, g = _d_project(x, rowmask, w_in_ref, b_in_ref, lb_ref)

    for s in range(ns):
        r0 = s * CHUNK
        cs = c_s.at[s]
        cs[C_OFF - C_HIST:C_OFF, :] = jnp.zeros((C_HIST, C_WIDTH), F32) if s == ns - 1 else c0_ref[s]
        cs[C_OFF:C_OFF + CHUNK, :] = u[r0:r0 + CHUNK]
        _conformer_conv(cs, C_OFF, CHUNK, convw_ref, cvec_ref, mix_s, r0)
        cout_ref[s] = cs[C_OFF + CHUNK - C_HIST:C_OFF + CHUNK, :]

    rows = [slice(s * CHUNK, (s + 1) * CHUNK) for s in range(ns)]
    outs, ends = _hgrn_chunks([(q[r], kk[r], v[r], lf[r]) for r in rows],
                              [[jnp.zeros((D_VAL, D_KEY), F32) if s == ns - 1 else s0_ref[s, h].T
                                for h in range(D_HEADS)] for s in range(ns)])
    for s in range(ns):
        mix_s[rows[s], C_WIDTH:] = _hgrn_readout(outs[s], g[rows[s]], normg_ref[...]).astype(BF16)
        for h in range(D_HEADS):
            sout_ref[s, h] = ends[s][h].T


def _l1_main_kernel(nt, x_ref, xprev_ref, c0_ref, s0_ref, convw_ref, cvec_ref, lb_ref, normg_ref,
                    w_in_ref, b_in_ref, w_o_ref, ln_ref, wg_ref, wu_ref, wd_ref,
                    h_ref, cout_ref, sout_ref,
                    c_s, st_s, mix_s):
    i, n, t = _tile_position(nt)
    tm = x_ref.shape[1]

    @pl.when(t == 0)
    def _():
        c_s[C_OFF - C_HIST:C_OFF, :] = c0_ref[0]
        for h in range(D_HEADS):
            st_s[h] = s0_ref[0, h].T

    _zero_on_first_step(i, mix_s)

    post = _SplitPostBlock(xprev_ref[0], mix_s[...], (w_o_ref, ln_ref, wg_ref, wu_ref, wd_ref), POST_ROW_GROUPS)
    post.run(1)

    x = x_ref[0]
    plan = iter(L1_FFN_STEPS)
    between = lambda: post.run(next(plan, 0))
    c_s[C_OFF:, :] = _c_project(x, None, w_in_ref, b_in_ref)
    q, kk, v, lf, g = _d_project(x, None, w_in_ref, b_in_ref, lb_ref)
    between()
    _conformer_conv(c_s, C_OFF, tm, convw_ref, cvec_ref, mix_s, 0)
    c_s[C_OFF - C_HIST:C_OFF, :] = c_s[C_OFF + tm - C_HIST:C_OFF + tm, :]

    nc = tm // CHUNK
    rows = [slice(c * CHUNK, (c + 1) * CHUNK) for c in range(nc)]
    outs, ends = _hgrn_chunks([(q[r], kk[r], v[r], lf[r]) for r in rows],
                              [[st_s[h] for h in range(D_HEADS)]] + [None] * (nc - 1), between)
    for c in range(nc):
        mix_s[rows[c], C_WIDTH:] = _hgrn_readout(outs[c], g[rows[c]], normg_ref[...]).astype(BF16)
    for h in range(D_HEADS):
        st_s[h] = ends[-1][h]

    h_ref[0] = post.result()

    @pl.when(jnp.logical_and(t == nt - 1, i < n))
    def _():
        cout_ref[0] = c_s[C_OFF - C_HIST:C_OFF, :]
        for h in range(D_HEADS):
            sout_ref[0, h] = st_s[h].T


def _whole(shape):
    zeros = (0,) * len(shape)
    return pl.BlockSpec(shape, lambda *_: zeros, pipeline_mode=pl.Buffered(1))


def _const_spec(a):
    return pl.BlockSpec(memory_space=pltpu.SMEM) if a.ndim == 1 else _whole(a.shape)


N_FFN_BLOCKS = FFN_HIDDEN // FFN_BLOCK


def _streamed_ffn_specs(layer):
    in_specs = [pl.BlockSpec((1, D_MODEL, FFN_BLOCK), lambda j: (layer, 0, j)),
                pl.BlockSpec((1, D_MODEL, FFN_BLOCK), lambda j: (layer, 0, N_FFN_BLOCKS + j)),
                pl.BlockSpec((1, FFN_BLOCK, D_MODEL), lambda j: (layer, j, 0))]
    out_specs = [pl.BlockSpec((D_MODEL, FFN_BLOCK), lambda j: (0, j)),
                 pl.BlockSpec((D_MODEL, FFN_BLOCK), lambda j: (0, j)),
                 pl.BlockSpec((FFN_BLOCK, D_MODEL), lambda j: (j, 0))]
    out_shape = [jax.ShapeDtypeStruct((D_MODEL, FFN_HIDDEN), BF16),
                 jax.ShapeDtypeStruct((D_MODEL, FFN_HIDDEN), BF16),
                 jax.ShapeDtypeStruct((FFN_HIDDEN, D_MODEL), BF16)]
    return in_specs, out_specs, out_shape


def _residual_scratch(rows):
    return [pltpu.VMEM((rows, D_MODEL), F32), pltpu.VMEM((rows, D_MODEL), BF16), pltpu.VMEM((rows, D_MODEL), F32)]


def _whole_out(shape):
    zeros = (0,) * len(shape)
    return pl.BlockSpec(shape, lambda *_: zeros)


def _params(n_axes):
    return pltpu.CompilerParams(dimension_semantics=("arbitrary",) * n_axes,
                                vmem_limit_bytes=VMEM_LIMIT)


def _l0_short(x, rowmask, k0, v0, u0, kvb, consts, weights, w_gu, w_down, layer):
    rows, ns = x.shape[0], k0.shape[0] + 1
    ins = (x, rowmask, k0, v0, u0, kvb) + consts + weights
    ffn_in, ffn_out, ffn_shape = _streamed_ffn_specs(layer)
    return pl.pallas_call(
        _l0_short_kernel,
        grid=(N_FFN_BLOCKS,),
        in_specs=[_const_spec(a) for a in ins] + ffn_in,
        out_specs=[_whole_out((rows, D_MODEL)), _whole_out((ns, WINDOW, KV_W)),
                   _whole_out((ns, WINDOW, KV_W)), _whole_out((ns, B_CONV - 1, B_WIDTH))] + ffn_out,
        out_shape=[jax.ShapeDtypeStruct((rows, D_MODEL), F32),
                   jax.ShapeDtypeStruct((ns, WINDOW, KV_W), F32),
                   jax.ShapeDtypeStruct((ns, WINDOW, KV_W), F32),
                   jax.ShapeDtypeStruct((ns, B_CONV - 1, B_WIDTH), F32)] + ffn_shape,
        scratch_shapes=[pltpu.VMEM((ns, B_OFF + CHUNK, B_WIDTH), F32), pltpu.VMEM((rows, D_MODEL), BF16)]
        + _residual_scratch(rows),
        compiler_params=_params(1),
        name="l0_short",
    )(*ins, w_gu, w_gu, w_down)


class _SkewedTiles:
    def __init__(self, nb, nt):
        self.nt, self.n = nt, nb * nt

    def _cur(self, i):
        return jnp.minimum(i, self.n - 1)

    def tile(self):
        return pl.BlockSpec((1, TILE, D_MODEL), lambda i: (self._cur(i) // self.nt, self._cur(i) % self.nt, 0))

    def prev_tile(self):
        prev = lambda i: jnp.maximum(i - 1, 0)
        return pl.BlockSpec((1, TILE, D_MODEL), lambda i: (prev(i) // self.nt, prev(i) % self.nt, 0))

    def fixed(self, shape, idx):
        zeros = (0,) * len(shape)
        return pl.BlockSpec((1,) + shape, lambda i: (idx,) + zeros)

    def per_prompt(self, shape):
        zeros = (0,) * len(shape)
        return pl.BlockSpec((1,) + shape, lambda i: (self._cur(i) // self.nt,) + zeros)


def _l0_main(x, k0, v0, u0, kvb, consts, weights, meta_idx):
    nb, seq, _ = x.shape
    nt = seq // TILE
    sk = _SkewedTiles(nb, nt)
    ins = (x, x, k0, v0, u0, kvb) + consts + weights
    in_specs = [sk.tile(), sk.prev_tile(),
                sk.fixed((WINDOW, KV_W), meta_idx), sk.fixed((WINDOW, KV_W), meta_idx),
                sk.fixed((B_CONV - 1, B_WIDTH), meta_idx),
                _whole(kvb.shape)] + [_const_spec(a) for a in consts + weights]
    return pl.pallas_call(
        functools.partial(_l0_main_kernel, nt),
        grid=(nb * nt + 1,),
        in_specs=in_specs,
        out_specs=[sk.prev_tile(), sk.per_prompt((WINDOW, KV_W)), sk.per_prompt((WINDOW, KV_W)),
                   sk.per_prompt((B_CONV - 1, B_WIDTH))],
        out_shape=[jax.ShapeDtypeStruct((nb, seq, D_MODEL), F32),
                   jax.ShapeDtypeStruct((nb, WINDOW, KV_W), F32),
                   jax.ShapeDtypeStruct((nb, WINDOW, KV_W), F32),
                   jax.ShapeDtypeStruct((nb, B_CONV - 1, B_WIDTH), F32)],
        scratch_shapes=[pltpu.VMEM((WINDOW + TILE, KV_W), F32),
                        pltpu.VMEM((WINDOW + TILE, KV_W), F32), pltpu.VMEM((B_OFF + TILE, B_WIDTH), F32),
                        pltpu.VMEM((TILE, D_MODEL), BF16)],
        compiler_params=_params(1),
        name="l0_main",
    )(*ins)


def _l1_short(x, rowmask, c0, s0, consts, weights, w_gu, w_down, layer):
    rows, ns = x.shape[0], c0.shape[0] + 1
    ins = (x, rowmask, c0, s0) + consts + weights
    ffn_in, ffn_out, ffn_shape = _streamed_ffn_specs(layer)
    return pl.pallas_call(
        _l1_short_kernel,
        grid=(N_FFN_BLOCKS,),
        in_specs=[_const_spec(a) for a in ins] + ffn_in,
        out_specs=[_whole_out((rows, D_MODEL)), _whole_out((ns, C_HIST, C_WIDTH)),
                   _whole_out((ns, D_HEADS, D_KEY, D_VAL))] + ffn_out,
        out_shape=[jax.ShapeDtypeStruct((rows, D_MODEL), F32),
                   jax.ShapeDtypeStruct((ns, C_HIST, C_WIDTH), F32),
                   jax.ShapeDtypeStruct((ns, D_HEADS, D_KEY, D_VAL), F32)] + ffn_shape,
        scratch_shapes=[pltpu.VMEM((ns, C_OFF + CHUNK, C_WIDTH), F32), pltpu.VMEM((rows, D_MODEL), BF16)]
        + _residual_scratch(rows),
        compiler_params=_params(1),
        name="l1_short",
    )(*ins, w_gu, w_gu, w_down)


def _l1_main(x, c0, s0, consts, weights, meta_idx):
    nb, seq, _ = x.shape
    nt = seq // TILE
    sk = _SkewedTiles(nb, nt)
    ins = (x, x, c0, s0) + consts + weights
    in_specs = [sk.tile(), sk.prev_tile(), sk.fixed((C_HIST, C_WIDTH), meta_idx),
                sk.fixed((D_HEADS, D_KEY, D_VAL), meta_idx),
                ] + [_const_spec(a) for a in consts + weights]
    return pl.pallas_call(
        functools.partial(_l1_main_kernel, nt),
        grid=(nb * nt + 1,),
        in_specs=in_specs,
        out_specs=[sk.prev_tile(), sk.per_prompt((C_HIST, C_WIDTH)),
                   sk.per_prompt((D_HEADS, D_KEY, D_VAL))],
        out_shape=[jax.ShapeDtypeStruct((nb, seq, D_MODEL), F32),
                   jax.ShapeDtypeStruct((nb, C_HIST, C_WIDTH), F32),
                   jax.ShapeDtypeStruct((nb, D_HEADS, D_KEY, D_VAL), F32)],
        scratch_shapes=[pltpu.VMEM((C_OFF + TILE, C_WIDTH), F32), pltpu.VMEM((D_HEADS, D_VAL, D_KEY), F32),
                        pltpu.VMEM((TILE, D_MODEL), BF16)],
        compiler_params=_params(1),
        name="l1_main",
    )(*ins)


def _alibi_tables():
    slopes = np.exp2(-8.0 * np.arange(1, A_HEADS + 1, dtype=np.float32) / A_HEADS).astype(np.float32)
    i = np.arange(CHUNK, dtype=np.float32)[None, :]
    j = np.arange(BAND, dtype=np.float32)[:, None]
    dist = np.abs(WINDOW + i - j).astype(np.float32)
    bias = np.zeros((A_KV_HEADS, 2 * BAND, 2 * CHUNK), np.float32)
    for g in range(A_KV_HEADS):
        for e in range(2):
            for half, head in enumerate((4 * g + e, 4 * g + 2 + e)):
                cols = slice(CHUNK * half, CHUNK * (half + 1))
                bias[g, BAND * e:BAND * (e + 1), cols] = -slopes[head] * dist
    return bias


def kernel(x_prompt, x_sample, cache_k_a, cache_v_a, state_conv_b, state_conv_c, state_hgrn, meta_tokens,
           ab_w_in, ab_b_in, a_sinks, b_conv_w, ab_w_o, cd_w_in, cd_b_in, c_conv_w, c_conv_b, c_ln_g,
           c_ln_b, d_lower_bounds, d_norm_g, cd_w_o, ln1_g, ln1_b, ln2_g, ln2_b, ffn_w_gu, ffn_w_down):
    nb = x_prompt.shape[0]
    ns = x_sample.shape[0]
    assert x_sample.shape[1] == CHUNK and cache_k_a.shape[1] == WINDOW
    assert x_prompt.shape[1] % TILE == 0
    meta_idx = ns
    n_short = ns + 1
    rows = n_short * CHUNK

    xs = jnp.concatenate([x_sample.reshape(ns * CHUNK, D_MODEL),
                          jnp.zeros((PADF, D_MODEL), F32), meta_tokens.astype(F32)], axis=0)
    rowmask = np.ones((rows, 1), np.float32)
    rowmask[ns * CHUNK:ns * CHUNK + PADF] = 0.0
    rowmask = jnp.asarray(rowmask)

    kvb_short = np.zeros((BAND, 2 * CHUNK), np.float32)
    kvb_short[:WINDOW + PADF] = NEG
    kvb_main = np.zeros((2, BAND, 2 * CHUNK), np.float32)
    kvb_main[0, :CHUNK + PADF] = NEG
    kvb_main[1, :PADF] = NEG

    a_consts = (jnp.asarray(_alibi_tables()), a_sinks.astype(F32), b_conv_w.astype(F32))

    k0 = cache_k_a.reshape(ns, WINDOW, KV_W)
    v0 = cache_v_a.reshape(ns, WINDOW, KV_W)
    u0, c0, s0 = state_conv_b, state_conv_c, state_hgrn

    def layer_weights(l, w_in, b_in, w_o):
        ln = jnp.stack([ln1_g[l], ln1_b[l], ln2_g[l], ln2_b[l]]).astype(F32)
        return (w_in.astype(BF16), b_in.astype(F32)[None, :], w_o.astype(BF16), ln)

    w0 = layer_weights(0, ab_w_in, ab_b_in, ab_w_o)
    w1 = layer_weights(1, cd_w_in, cd_b_in, cd_w_o)
    w_gu, w_down = ffn_w_gu.astype(F32), ffn_w_down.astype(F32)

    hs0, k_s, v_s, u_s, *ffn0 = _l0_short(xs, rowmask, k0, v0, u0, jnp.asarray(kvb_short), a_consts, w0,
                                          w_gu, w_down, 0)
    hp0, k_p, v_p, u_p = _l0_main(x_prompt, k_s, v_s, u_s, jnp.asarray(kvb_main), a_consts,
                                  w0 + tuple(ffn0), meta_idx)

    cvec = jnp.stack([c_conv_b, c_ln_g, c_ln_b]).astype(F32)
    c_consts = (c_conv_w.astype(F32), cvec, d_lower_bounds.astype(F32), d_norm_g.astype(F32)[None, :])
    hs1, c_s, s_s, *ffn1 = _l1_short(hs0, rowmask, c0, s0, c_consts, w1, w_gu, w_down, 1)
    hp1, c_p, s_p = _l1_main(hp0, c_s, s_s, c_consts, w1 + tuple(ffn1), meta_idx)

    kv = lambda a: a.reshape(a.shape[0], WINDOW, A_KV_HEADS, HEAD_DIM)
    y_sample = hs1[:ns * CHUNK].reshape(ns, CHUNK, D_MODEL)
    return (hp1, y_sample, kv(k_p), kv(v_p), u_p, c_p, s_p,
            kv(k_s[:ns]), kv(v_s[:ns]), u_s[:ns], c_s[:ns], s_s[:ns])
```
